```python
import math
import jax, jax.numpy as jnp
from jax import lax
import numpy as np

D_MODEL = 2048
BATCH = 2
SEQ = 8192
DEPTH = 1

MIX_WIDTH = D_MODEL
HEAD_DIM = 64
BLOCK = 128
ROPE_THETA = 10000.0
EPS = 1e-6
NEG_INF = -1e30

A_WIDTH = MIX_WIDTH // 2
A_VDIM = 2 * HEAD_DIM
A_HEADS = A_WIDTH // A_VDIM
A_SUB = HEAD_DIM

B_WIDTH = MIX_WIDTH - A_WIDTH
B_HEADS = B_WIDTH // HEAD_DIM
B_KV_HEADS = B_HEADS // 8
B_GROUP = B_HEADS // B_KV_HEADS
WINDOW = 128

SPLIT_SIZES = (
    A_HEADS * 2 * A_SUB,
    A_HEADS * 2 * A_SUB,
    A_WIDTH,
    A_WIDTH,
    B_HEADS * HEAD_DIM,
    B_KV_HEADS * HEAD_DIM,
    B_KV_HEADS * HEAD_DIM,
    B_WIDTH,
)
PROJ_WIDTH = sum(SPLIT_SIZES)
SPLIT_IDX = tuple(int(i) for i in np.cumsum(SPLIT_SIZES)[:-1])

kernel_name = "hymba_diffattn_swa_sinks_layer"


def rmsnorm(x, w):
    xf = x.astype(jnp.float32)
    y = xf * lax.rsqrt(jnp.mean(xf * xf, axis=-1, keepdims=True) + EPS)
    return (y * w.astype(jnp.float32)).astype(x.dtype)


def rope_tables(positions):
    inv_freq = ROPE_THETA ** (-(jnp.arange(0, HEAD_DIM, 2, dtype=jnp.float32) / HEAD_DIM))
    ang = positions.astype(jnp.float32)[..., None] * inv_freq
    return jnp.cos(ang), jnp.sin(ang)


def apply_rope(x, cos, sin):
    shape = cos.shape[:2] + (1,) * (x.ndim - 3) + cos.shape[-1:]
    c = cos.reshape(shape)
    s = sin.reshape(shape)
    xf = x.astype(jnp.float32)
    x1, x2 = jnp.split(xf, 2, axis=-1)
    out = jnp.concatenate([x1 * c - x2 * s, x2 * c + x1 * s], axis=-1)
    return out.astype(x.dtype)


def diff_attention(q, k, v, lam):
    b, s_len, h = q.shape[:3]
    nblk = s_len // BLOCK
    scale = A_SUB ** -0.5
    key_pos = jnp.arange(s_len)
    lam = lam.astype(jnp.float32)

    def one_block(i):
        qb = lax.dynamic_slice_in_dim(q, i * BLOCK, BLOCK, axis=1)
        sc = jnp.einsum('bthcd,bshcd->bhcts', qb, k,
                        preferred_element_type=jnp.float32) * scale
        q_pos = i * BLOCK + jnp.arange(BLOCK)
        mask = key_pos[None, :] <= q_pos[:, None]
        sc = jnp.where(mask, sc, NEG_INF)
        p = jax.nn.softmax(sc, axis=-1)
        a = p[:, :, 0] - lam * p[:, :, 1]
        return jnp.einsum('bhts,bshe->bthe', a.astype(v.dtype), v)

    out = lax.map(one_block, jnp.arange(nblk))
    return jnp.moveaxis(out, 0, 1).reshape(b, s_len, h, A_VDIM)


def swa_sinks_attention(q, k, v, sinks):
    b, s_len = q.shape[:2]
    n = s_len // BLOCK
    scale = HEAD_DIM ** -0.5
    qb = q.reshape(b, n, BLOCK, B_KV_HEADS, B_GROUP, HEAD_DIM)
    kb = k.reshape(b, n, BLOCK, B_KV_HEADS, HEAD_DIM)
    vb = v.reshape(b, n, BLOCK, B_KV_HEADS, HEAD_DIM)
    pad = ((0, 0), (1, 0), (0, 0), (0, 0), (0, 0))
    k_band = jnp.concatenate([jnp.pad(kb[:, :-1], pad), kb], axis=2)
    v_band = jnp.concatenate([jnp.pad(vb[:, :-1], pad), vb], axis=2)
    sc = jnp.einsum('bntkgd,bnskd->bnkgts', qb, k_band,
                    preferred_element_type=jnp.float32) * scale
    t_loc = jnp.arange(BLOCK)
    s_loc = jnp.arange(2 * BLOCK)
    rel = t_loc[:, None] + BLOCK - s_loc[None, :]
    in_window = (rel >= 0) & (rel < WINDOW)
    key_abs = jnp.arange(n)[:, None] * BLOCK - BLOCK + s_loc[None, :]
    mask = in_window[None, :, :] & (key_abs >= 0)[:, None, :]
    sc = jnp.where(mask[None, :, None, None], sc, NEG_INF)
    sink = sinks.astype(jnp.float32).reshape(1, 1, B_KV_HEADS, B_GROUP, 1, 1)
    sink = jnp.broadcast_to(sink, sc.shape[:-1] + (1,))
    p = jax.nn.softmax(jnp.concatenate([sc, sink], axis=-1), axis=-1)[..., :-1]
    o = jnp.einsum('bnkgts,bnskd->bntkgd', p.astype(v.dtype), v_band)
    return o.reshape(b, s_len, B_HEADS * HEAD_DIM)


def setup_inputs(seed: int = 0) -> dict:
    key = jax.random.key(seed)
    ks = jax.random.split(key, 16)
    f32 = jnp.float32
    x = jax.random.normal(ks[0], (BATCH, SEQ, D_MODEL), f32)
    positions = jnp.broadcast_to(jnp.arange(SEQ, dtype=jnp.int32), (BATCH, SEQ))
    gain = lambda k, shape: 1.0 + 0.05 * jax.random.normal(k, shape, f32)
    return {
        "x": x,
        "positions": positions,
        "norm_w": gain(ks[1], (DEPTH, D_MODEL)),
        "w_in": jax.random.normal(ks[2], (DEPTH, D_MODEL, PROJ_WIDTH), f32) * D_MODEL ** -0.5,
        "q_norm_a": gain(ks[3], (DEPTH, A_SUB)),
        "k_norm_a": gain(ks[4], (DEPTH, A_SUB)),
        "lambda_q1": 0.1 * jax.random.normal(ks[5], (DEPTH, A_SUB), f32),
        "lambda_k1": 0.1 * jax.random.normal(ks[6], (DEPTH, A_SUB), f32),
        "lambda_q2": 0.1 * jax.random.normal(ks[7], (DEPTH, A_SUB), f32),
        "lambda_k2": 0.1 * jax.random.normal(ks[8], (DEPTH, A_SUB), f32),
        "subln_w": gain(ks[9], (DEPTH, A_VDIM)),
        "q_norm_b": gain(ks[10], (DEPTH, HEAD_DIM)),
        "k_norm_b": gain(ks[11], (DEPTH, HEAD_DIM)),
        "sinks": 0.5 * jax.random.normal(ks[12], (DEPTH, B_HEADS), f32),
        "w_out": jax.random.normal(ks[13], (DEPTH, MIX_WIDTH, D_MODEL), f32) * MIX_WIDTH ** -0.5,
    }


def reference(x, positions, norm_w, w_in, q_norm_a, k_norm_a, lambda_q1, lambda_k1,
              lambda_q2, lambda_k2, subln_w, q_norm_b, k_norm_b, sinks, w_out):
    b, s_len, _ = x.shape
    cos, sin = rope_tables(positions)
    for layer in range(DEPTH):
        lam_init = 0.8 - 0.6 * math.exp(-0.3 * layer)
        h = rmsnorm(x, norm_w[layer])
        proj = jnp.einsum('bsd,dp->bsp', h, w_in[layer])
        qa, ka, va, ga, qb, kb, vb, gb = jnp.split(proj, SPLIT_IDX, axis=-1)

        qa = apply_rope(rmsnorm(qa.reshape(b, s_len, A_HEADS, 2, A_SUB), q_norm_a[layer]), cos, sin)
        ka = apply_rope(rmsnorm(ka.reshape(b, s_len, A_HEADS, 2, A_SUB), k_norm_a[layer]), cos, sin)
        va = va.reshape(b, s_len, A_HEADS, A_VDIM)
        lam = (jnp.exp(jnp.sum(lambda_q1[layer].astype(jnp.float32) * lambda_k1[layer].astype(jnp.float32)))
               - jnp.exp(jnp.sum(lambda_q2[layer].astype(jnp.float32) * lambda_k2[layer].astype(jnp.float32)))
               + lam_init)
        oa = diff_attention(qa, ka, va, lam)
        oa = rmsnorm(oa, subln_w[layer]) * (1.0 - lam_init)
        ya = oa.reshape(b, s_len, A_WIDTH) * jax.nn.silu(ga)

        qb = apply_rope(rmsnorm(qb.reshape(b, s_len, B_HEADS, HEAD_DIM), q_norm_b[layer]), cos, sin)
        kb = apply_rope(rmsnorm(kb.reshape(b, s_len, B_KV_HEADS, HEAD_DIM), k_norm_b[layer]), cos, sin)
        vb = vb.reshape(b, s_len, B_KV_HEADS, HEAD_DIM)
        ob = swa_sinks_attention(qb, kb, vb, sinks[layer])
        yb = ob * jax.nn.silu(gb)

        y = jnp.concatenate([ya, yb], axis=-1)
        x = x + jnp.einsum('bsm,md->bsd', y, w_out[layer])
    return x
```

```python
import functools
import math

import jax
import jax.numpy as jnp
from jax import lax
from jax.experimental import pallas as pl
from jax.experimental.pallas import tpu as pltpu

D_MODEL = 2048
HEAD_DIM = 64
ROPE_THETA = 10000.0
EPS = 1e-6
NEG_INF = -1e30
LOG2E = math.log2(math.e)

A_HEADS = 8
A_VDIM = 128
A_WIDTH = 1024
B_HEADS = 16
B_KV_HEADS = 2
B_GROUP = 8
B_WIDTH = 1024
WINDOW = 128
PROJ_WIDTH = 6400

OFF_QA, OFF_KA, OFF_VA, OFF_GA, OFF_QB, OFF_KB, OFF_VB, OFF_GB = 0, 1024, 2048, 3072, 4096, 5120, 5248, 5376

V7X_VMEM_LIMIT = 56 * 1024 * 1024
LANES = 128
CHUNK = 256

PROJ_BM = 512
ATT_TQ = 512
SWA_TQ = 512
OUT_BM = 512


def _in_proj_kernel(x_ref, pos_ref, nw_ref, w_ref, invf_ref, gmat_ref,
                    qna_ref, kna_ref, qnb_ref, knb_ref,
                    qa_ref, ka_ref, va_ref, ga_ref, qb_ref, kb_ref, vb_ref, gb_ref,
                    h_ref):
    bm = x_ref.shape[0]
    x = x_ref[...]
    ms = jnp.mean(x * x, axis=-1, keepdims=True)
    h_ref[...] = ((x * lax.rsqrt(ms + EPS)) * nw_ref[...]).astype(jnp.bfloat16)

    ang = pos_ref[...].astype(jnp.float32) * invf_ref[...]
    lane = lax.broadcasted_iota(jnp.int32, (bm, CHUNK), 1)
    first_half = (lane % HEAD_DIM) < (HEAD_DIM // 2)
    cos = jnp.cos(ang)
    sin_signed = jnp.where(first_half, -jnp.sin(ang), jnp.sin(ang))
    gmat = gmat_ref[...]

    def proj(c0, width=CHUNK):
        return jnp.dot(h_ref[...], w_ref[:, c0:c0 + width], preferred_element_type=jnp.float32)

    def norm_rope(t, w):
        msq = jnp.dot((t * t).astype(jnp.bfloat16), gmat, preferred_element_type=jnp.float32)
        y = t * lax.rsqrt(msq + EPS) * w
        rot = jnp.where(first_half, pltpu.roll(y, CHUNK - HEAD_DIM // 2, 1), pltpu.roll(y, HEAD_DIM // 2, 1))
        return y * cos + rot * sin_signed

    def silu(t):
        return t / (1.0 + jnp.exp(-t))

    for c in range(A_WIDTH // CHUNK):
        o = c * CHUNK
        qa_ref[:, o:o + CHUNK] = norm_rope(proj(OFF_QA + o), qna_ref[...]).astype(jnp.bfloat16)
        ka_ref[:, o:o + CHUNK] = norm_rope(proj(OFF_KA + o), kna_ref[...]).astype(jnp.bfloat16)
        va_ref[:, o:o + CHUNK] = proj(OFF_VA + o).astype(jnp.bfloat16)
        ga_ref[:, o:o + CHUNK] = silu(proj(OFF_GA + o)).astype(jnp.bfloat16)
        gb_ref[:, o:o + CHUNK] = silu(proj(OFF_GB + o)).astype(jnp.bfloat16)
        qb = norm_rope(proj(OFF_QB + o), qnb_ref[...]).astype(jnp.bfloat16)
        for j in range(CHUNK // HEAD_DIM):
            qb_ref[c * (CHUNK // HEAD_DIM) + j] = qb[:, j * HEAD_DIM:(j + 1) * HEAD_DIM]
    kv = proj(OFF_KB)
    kbn = norm_rope(kv, knb_ref[...]).astype(jnp.bfloat16)
    vbb = kv.astype(jnp.bfloat16)
    for j in range(B_KV_HEADS):
        kb_ref[j] = kbn[:, j * HEAD_DIM:(j + 1) * HEAD_DIM]
        vb_ref[j] = vbb[:, (B_KV_HEADS + j) * HEAD_DIM:(B_KV_HEADS + j + 1) * HEAD_DIM]


def _in_proj(x2, pos2, norm_w, w_bf16, invf, gmat, qna, kna, qnb, knb):
    m = x2.shape[0]
    bm = PROJ_BM
    row = lambda i: (i, 0)
    const = lambda i: (0, 0)
    hm = lambda i: (0, i, 0)
    bf = jnp.bfloat16
    out_shape = (
        jax.ShapeDtypeStruct((m, A_WIDTH), bf),
        jax.ShapeDtypeStruct((m, A_WIDTH), bf),
        jax.ShapeDtypeStruct((m, A_WIDTH), bf),
        jax.ShapeDtypeStruct((m, A_WIDTH), bf),
        jax.ShapeDtypeStruct((B_HEADS, m, HEAD_DIM), bf),
        jax.ShapeDtypeStruct((B_KV_HEADS, m, HEAD_DIM), bf),
        jax.ShapeDtypeStruct((B_KV_HEADS, m, HEAD_DIM), bf),
        jax.ShapeDtypeStruct((m, B_WIDTH), bf),
    )
    out_specs = (
        pl.BlockSpec((bm, A_WIDTH), row), pl.BlockSpec((bm, A_WIDTH), row),
        pl.BlockSpec((bm, A_WIDTH), row), pl.BlockSpec((bm, A_WIDTH), row),
        pl.BlockSpec((B_HEADS, bm, HEAD_DIM), hm),
        pl.BlockSpec((B_KV_HEADS, bm, HEAD_DIM), hm),
        pl.BlockSpec((B_KV_HEADS, bm, HEAD_DIM), hm),
        pl.BlockSpec((bm, B_WIDTH), row),
    )
    in_specs = [
        pl.BlockSpec((bm, D_MODEL), row),
        pl.BlockSpec((bm, 1), row),
        pl.BlockSpec((1, D_MODEL), const),
        pl.BlockSpec((D_MODEL, PROJ_WIDTH), const, pipeline_mode=pl.Buffered(1)),
        pl.BlockSpec((1, CHUNK), const),
        pl.BlockSpec((CHUNK, CHUNK), const),
        pl.BlockSpec((1, CHUNK), const), pl.BlockSpec((1, CHUNK), const),
        pl.BlockSpec((1, CHUNK), const), pl.BlockSpec((1, CHUNK), const),
    ]
    return pl.pallas_call(
        _in_proj_kernel,
        grid=(m // bm,),
        in_specs=in_specs,
        out_specs=out_specs,
        out_shape=out_shape,
        scratch_shapes=[pltpu.VMEM((bm, D_MODEL), jnp.bfloat16)],
        compiler_params=pltpu.CompilerParams(
            dimension_semantics=("arbitrary",), vmem_limit_bytes=V7X_VMEM_LIMIT),
        name="in_proj",
    )(x2, pos2, norm_w, w_bf16, invf, gmat, qna, kna, qnb, knb)


def _diff_attn_kernel(q_ref, k_ref, v_ref, g_ref, lq1_ref, lk1_ref, lq2_ref, lk2_ref, sw_ref,
                      o_ref, m_ref, l_ref, acc_ref, *, lam_init):
    tq = q_ref.shape[0]
    tk = tq
    qi = pl.program_id(2)

    q = q_ref[...]
    lane = lax.broadcasted_iota(jnp.int32, q.shape, 1)
    zero = jnp.zeros_like(q)
    qs = (jnp.where(lane < HEAD_DIM, q, zero), jnp.where(lane >= HEAD_DIM, q, zero))

    m_ref[...] = jnp.full(m_ref.shape, NEG_INF, jnp.float32)
    l_ref[...] = jnp.zeros(l_ref.shape, jnp.float32)
    acc_ref[...] = jnp.zeros(acc_ref.shape, jnp.float32)

    def update(j, masked):
        k = k_ref[pl.ds(pl.multiple_of(j * tk, tk), tk), :]
        v = v_ref[pl.ds(pl.multiple_of(j * tk, tk), tk), :]
        for c in range(2):
            s = lax.dot_general(qs[c], k, (((1,), (1,)), ((), ())), preferred_element_type=jnp.float32)
            if masked:
                r = lax.broadcasted_iota(jnp.int32, s.shape, 0)
                cc = lax.broadcasted_iota(jnp.int32, s.shape, 1)
                s = jnp.where(cc <= r, s, NEG_INF)
            m_old = m_ref[c]
            m_new = jnp.maximum(m_old, jnp.max(s, axis=-1, keepdims=True))
            alpha = jnp.exp2(m_old - m_new)
            p = jnp.exp2(s - m_new)
            l_ref[c] = alpha * l_ref[c] + jnp.sum(p, axis=-1, keepdims=True)
            acc_ref[c] = alpha * acc_ref[c] + jnp.dot(p.astype(jnp.bfloat16), v,
                                                      preferred_element_type=jnp.float32)
            m_ref[c] = m_new

    def body(j, carry):
        update(j, False)
        return carry

    lax.fori_loop(0, qi, body, 0)
    update(qi, True)

    lam = (jnp.exp(jnp.sum(lq1_ref[...] * lk1_ref[...], axis=-1, keepdims=True))
           - jnp.exp(jnp.sum(lq2_ref[...] * lk2_ref[...], axis=-1, keepdims=True)) + lam_init)
    o = acc_ref[0] / l_ref[0] - lam * (acc_ref[1] / l_ref[1])
    o = o * lax.rsqrt(jnp.mean(o * o, axis=-1, keepdims=True) + EPS) * sw_ref[...] * (1.0 - lam_init)
    o_ref[...] = (o * g_ref[...].astype(jnp.float32)).astype(o_ref.dtype)


def _diff_attn(qa, ka, va, ga, lq1, lk1, lq2, lk2, subln_w, batch, seq, lam_init):
    tq = ATT_TQ
    nq = seq // tq
    qmap = lambda b, h, i: (b * nq + i, h)
    kvmap = lambda b, h, i: (b, h)
    const = lambda b, h, i: (0, 0)
    vec = pl.BlockSpec((1, HEAD_DIM), const)
    return pl.pallas_call(
        functools.partial(_diff_attn_kernel, lam_init=lam_init),
        grid=(batch, A_HEADS, nq),
        in_specs=[
            pl.BlockSpec((tq, A_VDIM), qmap),
            pl.BlockSpec((seq, A_VDIM), kvmap),
            pl.BlockSpec((seq, A_VDIM), kvmap),
            pl.BlockSpec((tq, A_VDIM), qmap),
            vec, vec, vec, vec,
            pl.BlockSpec((1, A_VDIM), const),
        ],
        out_specs=pl.BlockSpec((tq, A_VDIM), qmap),
        out_shape=jax.ShapeDtypeStruct((batch * seq, A_WIDTH), jnp.bfloat16),
        scratch_shapes=[
            pltpu.VMEM((2, tq, 1), jnp.float32),
            pltpu.VMEM((2, tq, 1), jnp.float32),
            pltpu.VMEM((2, tq, A_VDIM), jnp.float32),
        ],
        compiler_params=pltpu.CompilerParams(
            dimension_semantics=("arbitrary", "arbitrary", "arbitrary"), vmem_limit_bytes=V7X_VMEM_LIMIT),
        name="diff_attn",
    )(qa, ka, va, ga, lq1, lk1, lq2, lk2, subln_w)


def _swa_kernel(q_ref, kp_ref, kc_ref, vp_ref, vc_ref, sink_ref, g_ref, o_ref):
    tq = q_ref.shape[1]
    blk = WINDOW
    i = pl.program_id(2)
    rows = B_GROUP * blk
    r = lax.broadcasted_iota(jnp.int32, (rows, 2 * blk), 0) % blk
    c = lax.broadcasted_iota(jnp.int32, (rows, 2 * blk), 1)
    rel = r + blk - c
    in_window = (rel >= 0) & (rel < WINDOW)
    sink = sink_ref[0]
    for n in range(tq // blk):
        q = q_ref[:, n * blk:(n + 1) * blk, :].reshape(rows, HEAD_DIM)
        if n == 0:
            k = jnp.concatenate([kp_ref[0], kc_ref[0, 0:blk, :]], axis=0)
            v = jnp.concatenate([vp_ref[0], vc_ref[0, 0:blk, :]], axis=0)
            mask = in_window & ((c >= blk) | (i > 0))
        else:
            k = kc_ref[0, (n - 1) * blk:(n + 1) * blk, :]
            v = vc_ref[0, (n - 1) * blk:(n + 1) * blk, :]
            mask = in_window
        s = lax.dot_general(q, k, (((1,), (1,)), ((), ())), preferred_element_type=jnp.float32)
        s = jnp.where(mask, s, NEG_INF)
        m = jnp.maximum(jnp.max(s, axis=-1, keepdims=True), sink)
        p = jnp.exp2(s - m)
        denom = jnp.sum(p, axis=-1, keepdims=True) + jnp.exp2(sink - m)
        o = jnp.dot(p.astype(jnp.bfloat16), v, preferred_element_type=jnp.float32) / denom
        o = jnp.concatenate([o[g * blk:(g + 1) * blk, :] for g in range(B_GROUP)], axis=1)
        gate = g_ref[n * blk:(n + 1) * blk, :].astype(jnp.float32)
        o_ref[n * blk:(n + 1) * blk, :] = (o * gate).astype(o_ref.dtype)


def _swa_attn(qb, kb, vb, gb, sink_col, batch, seq):
    tq = SWA_TQ
    nq = seq // tq
    r = tq // WINDOW
    gw = B_GROUP * HEAD_DIM
    qmap = lambda b, kv, i: (kv, b * nq + i, 0)
    cur = lambda b, kv, i: (kv, b * nq + i, 0)
    prev = lambda b, kv, i: (kv, jnp.maximum(b * nq * r + i * r - 1, 0), 0)
    omap = lambda b, kv, i: (b * nq + i, kv)
    return pl.pallas_call(
        _swa_kernel,
        grid=(batch, B_KV_HEADS, nq),
        in_specs=[
            pl.BlockSpec((B_GROUP, tq, HEAD_DIM), qmap),
            pl.BlockSpec((1, WINDOW, HEAD_DIM), prev),
            pl.BlockSpec((1, tq, HEAD_DIM), cur),
            pl.BlockSpec((1, WINDOW, HEAD_DIM), prev),
            pl.BlockSpec((1, tq, HEAD_DIM), cur),
            pl.BlockSpec((1, B_GROUP * WINDOW, 1), lambda b, kv, i: (kv, 0, 0)),
            pl.BlockSpec((tq, gw), omap),
        ],
        out_specs=pl.BlockSpec((tq, gw), omap),
        out_shape=jax.ShapeDtypeStruct((batch * seq, B_WIDTH), jnp.bfloat16),
        compiler_params=pltpu.CompilerParams(
            dimension_semantics=("arbitrary", "arbitrary", "arbitrary"), vmem_limit_bytes=V7X_VMEM_LIMIT),
        name="swa_attn",
    )(qb, kb, kb, vb, vb, sink_col, gb)


def _out_proj_kernel(ya_ref, yb_ref, wa_ref, wb_ref, x_ref, o_ref):
    acc = jnp.dot(ya_ref[...], wa_ref[...], preferred_element_type=jnp.float32)
    acc = acc + jnp.dot(yb_ref[...], wb_ref[...], preferred_element_type=jnp.float32)
    o_ref[...] = x_ref[...] + acc


def _out_proj(ya, yb, wa, wb, x2):
    m = x2.shape[0]
    bm = OUT_BM
    row = lambda i: (i, 0)
    const = lambda i: (0, 0)
    return pl.pallas_call(
        _out_proj_kernel,
        grid=(m // bm,),
        in_specs=[
            pl.BlockSpec((bm, A_WIDTH), row),
            pl.BlockSpec((bm, B_WIDTH), row),
            pl.BlockSpec((A_WIDTH, D_MODEL), const, pipeline_mode=pl.Buffered(1)),
            pl.BlockSpec((B_WIDTH, D_MODEL), const, pipeline_mode=pl.Buffered(1)),
            pl.BlockSpec((bm, D_MODEL), row),
        ],
        out_specs=pl.BlockSpec((bm, D_MODEL), row),
        out_shape=jax.ShapeDtypeStruct((m, D_MODEL), jnp.float32),
        compiler_params=pltpu.CompilerParams(
            dimension_semantics=("arbitrary",), vmem_limit_bytes=V7X_VMEM_LIMIT),
        name="out_proj",
    )(ya, yb, wa, wb, x2)


def kernel(x, positions, norm_w, w_in, q_norm_a, k_norm_a, lambda_q1, lambda_k1, lambda_q2, lambda_k2,
           subln_w, q_norm_b, k_norm_b, sinks, w_out):
    batch, seq, _ = x.shape
    depth = norm_w.shape[0]
    m = batch * seq
    f32 = jnp.float32

    inv_freq = ROPE_THETA ** (-(jnp.arange(0, HEAD_DIM, 2, dtype=f32) / HEAD_DIM))
    invf = jnp.tile(inv_freq, CHUNK // (HEAD_DIM // 2)).reshape(1, CHUNK)
    gid = jnp.arange(CHUNK) // HEAD_DIM
    gmat = jnp.where(gid[:, None] == gid[None, :], 1.0 / HEAD_DIM, 0.0).astype(jnp.bfloat16)
    pos2 = positions.reshape(m, 1)
    qk_scale = HEAD_DIM ** -0.5 * LOG2E
    tile = lambda w: jnp.tile(w.astype(f32), CHUNK // HEAD_DIM).reshape(1, CHUNK)

    x2 = x.reshape(m, D_MODEL)
    for layer in range(depth):
        lam_init = 0.8 - 0.6 * math.exp(-0.3 * layer)
        w_bf16 = w_in[layer].astype(jnp.bfloat16)
        qa, ka, va, ga, qb, kb, vb, gb = _in_proj(
            x2, pos2, norm_w[layer].reshape(1, D_MODEL), w_bf16, invf, gmat,
            tile(q_norm_a[layer]) * qk_scale, tile(k_norm_a[layer]),
            tile(q_norm_b[layer]) * qk_scale, tile(k_norm_b[layer]))
        vec = lambda p: p[layer].astype(f32).reshape(1, HEAD_DIM)
        ya = _diff_attn(qa, ka, va, ga, vec(lambda_q1), vec(lambda_k1), vec(lambda_q2), vec(lambda_k2),
                        subln_w[layer].astype(f32).reshape(1, A_VDIM), batch, seq, lam_init)
        sink_col = jnp.repeat(sinks[layer].astype(f32) * LOG2E, WINDOW).reshape(B_KV_HEADS, B_GROUP * WINDOW, 1)
        yb = _swa_attn(qb, kb, vb, gb, sink_col, batch, seq)
        w_o = w_out[layer].astype(jnp.bfloat16)
        x2 = _out_proj(ya, yb, w_o[:A_WIDTH], w_o[A_WIDTH:], x2)
    return x2.reshape(batch, seq, D_MODEL)
```

```python
import functools
import math

import jax
import jax.numpy as jnp
from jax import lax
from jax.experimental import pallas as pl
from jax.experimental.pallas import tpu as pltpu

D_MODEL = 2048
HEAD_DIM = 64
ROPE_THETA = 10000.0
EPS = 1e-6
NEG_INF = -1e30
LOG2E = math.log2(math.e)

A_HEADS = 8
A_VDIM = 128
A_WIDTH = 1024
B_HEADS = 16
B_KV_HEADS = 2
B_GROUP = 8
B_WIDTH = 1024
WINDOW = 128
PROJ_WIDTH = 6400

OFF_QA, OFF_KA, OFF_VA, OFF_GA, OFF_QB, OFF_KB, OFF_VB, OFF_GB = 0, 1024, 2048, 3072, 4096, 5120, 5248, 5376

V7X_VMEM_LIMIT = 56 * 1024 * 1024
LANES = 128
CHUNK = 256

PROJ_BM = 512
ATT_TQ = 1024
ATT_QS = 256
SWA_TQ = 512
OUT_BM = 512

RAW_EXP_SCORE_BOUND = 60.0


def _in_proj_kernel(x_ref, pos_ref, nw_ref, w_ref, invf_ref, gmat_ref,
                    qna_ref, kna_ref, qnb_ref, knb_ref,
                    qa_ref, ka_ref, va_ref, ga_ref, qb_ref, kb_ref, vb_ref, gb_ref,
                    h_ref):
    bm = x_ref.shape[0]
    x = x_ref[...]
    ms = jnp.mean(x * x, axis=-1, keepdims=True)
    h_ref[...] = ((x * lax.rsqrt(ms + EPS)) * nw_ref[...]).astype(jnp.bfloat16)

    ang = pos_ref[...].astype(jnp.float32) * invf_ref[...]
    lane = lax.broadcasted_iota(jnp.int32, (bm, CHUNK), 1)
    first_half = (lane % HEAD_DIM) < (HEAD_DIM // 2)
    cos = jnp.cos(ang)
    sin_signed = jnp.where(first_half, -jnp.sin(ang), jnp.sin(ang))
    gmat = gmat_ref[...]
    ones_col = (lax.broadcasted_iota(jnp.int32, (bm, LANES), 1) == 0).astype(jnp.bfloat16)

    def proj(c0, width=CHUNK):
        return jnp.dot(h_ref[...], w_ref[:, c0:c0 + width], preferred_element_type=jnp.float32)

    def norm_rope(t, w):
        msq = jnp.dot((t * t).astype(jnp.bfloat16), gmat, preferred_element_type=jnp.float32)
        y = t * lax.rsqrt(msq + EPS) * w
        rot = jnp.where(first_half, pltpu.roll(y, CHUNK - HEAD_DIM // 2, 1), pltpu.roll(y, HEAD_DIM // 2, 1))
        return y * cos + rot * sin_signed

    def silu(t):
        return t / (1.0 + jnp.exp(-t))

    for c in range(A_WIDTH // CHUNK):
        o = c * CHUNK
        qa_ref[:, o:o + CHUNK] = norm_rope(proj(OFF_QA + o), qna_ref[...]).astype(jnp.bfloat16)
        ka_ref[:, o:o + CHUNK] = norm_rope(proj(OFF_KA + o), kna_ref[...]).astype(jnp.bfloat16)
        va = proj(OFF_VA + o).astype(jnp.bfloat16)
        for j in range(CHUNK // A_VDIM):
            base = (c * (CHUNK // A_VDIM) + j) * 2 * A_VDIM
            va_ref[:, base:base + A_VDIM] = va[:, j * A_VDIM:(j + 1) * A_VDIM]
            va_ref[:, base + A_VDIM:base + 2 * A_VDIM] = ones_col
        ga_ref[:, o:o + CHUNK] = silu(proj(OFF_GA + o)).astype(jnp.bfloat16)
        gb_ref[:, o:o + CHUNK] = silu(proj(OFF_GB + o)).astype(jnp.bfloat16)
        qb = norm_rope(proj(OFF_QB + o), qnb_ref[...]).astype(jnp.bfloat16)
        for j in range(CHUNK // HEAD_DIM):
            qb_ref[c * (CHUNK // HEAD_DIM) + j] = qb[:, j * HEAD_DIM:(j + 1) * HEAD_DIM]
    kv = proj(OFF_KB)
    kbn = norm_rope(kv, knb_ref[...]).astype(jnp.bfloat16)
    vbb = kv.astype(jnp.bfloat16)
    for j in range(B_KV_HEADS):
        kb_ref[j] = kbn[:, j * HEAD_DIM:(j + 1) * HEAD_DIM]
        vb_ref[j] = vbb[:, (B_KV_HEADS + j) * HEAD_DIM:(B_KV_HEADS + j + 1) * HEAD_DIM]


def _in_proj(x2, pos2, norm_w, w_bf16, invf, gmat, qna, kna, qnb, knb):
    m = x2.shape[0]
    bm = PROJ_BM
    row = lambda i: (i, 0)
    const = lambda i: (0, 0)
    hm = lambda i: (0, i, 0)
    bf = jnp.bfloat16
    out_shape = (
        jax.ShapeDtypeStruct((m, A_WIDTH), bf),
        jax.ShapeDtypeStruct((m, A_WIDTH), bf),
        jax.ShapeDtypeStruct((m, 2 * A_WIDTH), bf),
        jax.ShapeDtypeStruct((m, A_WIDTH), bf),
        jax.ShapeDtypeStruct((B_HEADS, m, HEAD_DIM), bf),
        jax.ShapeDtypeStruct((B_KV_HEADS, m, HEAD_DIM), bf),
        jax.ShapeDtypeStruct((B_KV_HEADS, m, HEAD_DIM), bf),
        jax.ShapeDtypeStruct((m, B_WIDTH), bf),
    )
    out_specs = (
        pl.BlockSpec((bm, A_WIDTH), row), pl.BlockSpec((bm, A_WIDTH), row),
        pl.BlockSpec((bm, 2 * A_WIDTH), row), pl.BlockSpec((bm, A_WIDTH), row),
        pl.BlockSpec((B_HEADS, bm, HEAD_DIM), hm),
        pl.BlockSpec((B_KV_HEADS, bm, HEAD_DIM), hm),
        pl.BlockSpec((B_KV_HEADS, bm, HEAD_DIM), hm),
        pl.BlockSpec((bm, B_WIDTH), row),
    )
    in_specs = [
        pl.BlockSpec((bm, D_MODEL), row),
        pl.BlockSpec((bm, 1), row),
        pl.BlockSpec((1, D_MODEL), const),
        pl.BlockSpec((D_MODEL, PROJ_WIDTH), const, pipeline_mode=pl.Buffered(1)),
        pl.BlockSpec((1, CHUNK), const),
        pl.BlockSpec((CHUNK, CHUNK), const),
        pl.BlockSpec((1, CHUNK), const), pl.BlockSpec((1, CHUNK), const),
        pl.BlockSpec((1, CHUNK), const), pl.BlockSpec((1, CHUNK), const),
    ]
    return pl.pallas_call(
        _in_proj_kernel,
        grid=(m // bm,),
        in_specs=in_specs,
        out_specs=out_specs,
        out_shape=out_shape,
        scratch_shapes=[pltpu.VMEM((bm, D_MODEL), jnp.bfloat16)],
        compiler_params=pltpu.CompilerParams(
            dimension_semantics=("arbitrary",), vmem_limit_bytes=V7X_VMEM_LIMIT),
        name="in_proj",
    )(x2, pos2, norm_w, w_bf16, invf, gmat, qna, kna, qnb, knb)


def _diff_attn_kernel(q_ref, k_ref, v_ref, g_ref, lq1_ref, lk1_ref, lq2_ref, lk2_ref, sw_ref,
                      o_ref, acc_ref, m_ref, *, lam_init, online_max):
    tq = q_ref.shape[0]
    nsub = tq // ATT_QS
    qi = pl.program_id(2)

    q = q_ref[...]
    lane = lax.broadcasted_iota(jnp.int32, q.shape, 1)
    zero = jnp.zeros_like(q)
    qs = (jnp.where(lane < HEAD_DIM, q, zero), jnp.where(lane >= HEAD_DIM, q, zero))

    acc_ref[...] = jnp.zeros(acc_ref.shape, jnp.float32)
    if online_max:
        m_ref[...] = jnp.full(m_ref.shape, NEG_INF, jnp.float32)

    def unit(c, r, start, nk, masked):
        rows = slice(r * ATT_QS, (r + 1) * ATT_QS)
        k = k_ref[pl.ds(start, nk), :]
        v = v_ref[pl.ds(start, nk), :]
        s = lax.dot_general(qs[c][rows], k, (((1,), (1,)), ((), ())), preferred_element_type=jnp.float32)
        if masked:
            row = lax.broadcasted_iota(jnp.int32, s.shape, 0)
            col = lax.broadcasted_iota(jnp.int32, s.shape, 1)
            s = jnp.where(col <= row + (nk - ATT_QS), s, NEG_INF)
        if online_max:
            m_old = m_ref[c, rows, :]
            m_new = jnp.maximum(m_old, jnp.max(s, axis=-1, keepdims=True))
            alpha = jnp.exp2(m_old - m_new)
            m_ref[c, rows, :] = m_new
            p = jnp.exp2(s - jnp.tile(m_new, (1, nk // LANES)))
            pv = jnp.dot(p.astype(jnp.bfloat16), v, preferred_element_type=jnp.float32)
            acc_ref[c, rows, :] = acc_ref[c, rows, :] * jnp.tile(alpha, (1, 2)) + pv
        else:
            p = jnp.exp2(s)
            pv = jnp.dot(p.astype(jnp.bfloat16), v, preferred_element_type=jnp.float32)
            acc_ref[c, rows, :] += pv

    def body(j, carry):
        start = pl.multiple_of(j * tq, tq)
        for r in range(nsub):
            for c in range(2):
                unit(c, r, start, tq, False)
        return carry

    lax.fori_loop(0, qi, body, 0)
    diag = pl.multiple_of(qi * tq, tq)
    for r in range(nsub):
        for c in range(2):
            unit(c, r, diag, (r + 1) * ATT_QS, True)

    lam = (jnp.exp(jnp.sum(lq1_ref[...] * lk1_ref[...], axis=-1, keepdims=True))
           - jnp.exp(jnp.sum(lq2_ref[...] * lk2_ref[...], axis=-1, keepdims=True)) + lam_init)
    a1 = acc_ref[0]
    a2 = acc_ref[1]
    o = a1[:, :A_VDIM] / a1[:, A_VDIM:A_VDIM + 1] - lam * (a2[:, :A_VDIM] / a2[:, A_VDIM:A_VDIM + 1])
    o = o * lax.rsqrt(jnp.mean(o * o, axis=-1, keepdims=True) + EPS) * sw_ref[...] * (1.0 - lam_init)
    o_ref[...] = (o * g_ref[...].astype(jnp.float32)).astype(o_ref.dtype)


def _diff_attn(qa, ka, va_ext, ga, lq1, lk1, lq2, lk2, subln_w, *, batch, seq, lam_init, online_max):
    tq = ATT_TQ
    nq = seq // tq
    qmap = lambda b, h, i: (b * nq + i, h)
    kvmap = lambda b, h, i: (b, h)
    const = lambda b, h, i: (0, 0)
    vec = pl.BlockSpec((1, HEAD_DIM), const)
    return pl.pallas_call(
        functools.partial(_diff_attn_kernel, lam_init=lam_init, online_max=online_max),
        grid=(batch, A_HEADS, nq),
        in_specs=[
            pl.BlockSpec((tq, A_VDIM), qmap),
            pl.BlockSpec((seq, A_VDIM), kvmap),
            pl.BlockSpec((seq, 2 * A_VDIM), kvmap),
            pl.BlockSpec((tq, A_VDIM), qmap),
            vec, vec, vec, vec,
            pl.BlockSpec((1, A_VDIM), const),
        ],
        out_specs=pl.BlockSpec((tq, A_VDIM), qmap),
        out_shape=jax.ShapeDtypeStruct((batch * seq, A_WIDTH), jnp.bfloat16),
        scratch_shapes=[
            pltpu.VMEM((2, tq, 2 * A_VDIM), jnp.float32),
            pltpu.VMEM((2, tq, LANES), jnp.float32),
        ],
        compiler_params=pltpu.CompilerParams(
            dimension_semantics=("arbitrary", "arbitrary", "arbitrary"), vmem_limit_bytes=V7X_VMEM_LIMIT),
        name="diff_attn_online" if online_max else "diff_attn",
    )(qa, ka, va_ext, ga, lq1, lk1, lq2, lk2, subln_w)


def _swa_kernel(q_ref, kp_ref, kc_ref, vp_ref, vc_ref, sink_ref, g_ref, o_ref):
    tq = q_ref.shape[1]
    blk = WINDOW
    i = pl.program_id(2)
    rows = B_GROUP * blk
    r = lax.broadcasted_iota(jnp.int32, (rows, 2 * blk), 0) % blk
    c = lax.broadcasted_iota(jnp.int32, (rows, 2 * blk), 1)
    rel = r + blk - c
    in_window = (rel >= 0) & (rel < WINDOW)
    sink = sink_ref[0]
    for n in range(tq // blk):
        q = q_ref[:, n * blk:(n + 1) * blk, :].reshape(rows, HEAD_DIM)
        if n == 0:
            k = jnp.concatenate([kp_ref[0], kc_ref[0, 0:blk, :]], axis=0)
            v = jnp.concatenate([vp_ref[0], vc_ref[0, 0:blk, :]], axis=0)
            mask = in_window & ((c >= blk) | (i > 0))
        else:
            k = kc_ref[0, (n - 1) * blk:(n + 1) * blk, :]
            v = vc_ref[0, (n - 1) * blk:(n + 1) * blk, :]
            mask = in_window
        s = lax.dot_general(q, k, (((1,), (1,)), ((), ())), preferred_element_type=jnp.float32)
        s = jnp.where(mask, s, NEG_INF)
        m = jnp.maximum(jnp.max(s, axis=-1, keepdims=True), sink)
        p = jnp.exp2(s - m)
        denom = jnp.sum(p, axis=-1, keepdims=True) + jnp.exp2(sink - m)
        o = jnp.dot(p.astype(jnp.bfloat16), v, preferred_element_type=jnp.float32) / denom
        o = jnp.concatenate([o[g * blk:(g + 1) * blk, :] for g in range(B_GROUP)], axis=1)
        gate = g_ref[n * blk:(n + 1) * blk, :].astype(jnp.float32)
        o_ref[n * blk:(n + 1) * blk, :] = (o * gate).astype(o_ref.dtype)


def _swa_attn(qb, kb, vb, gb, sink_col, batch, seq):
    tq = SWA_TQ
    nq = seq // tq
    r = tq // WINDOW
    gw = B_GROUP * HEAD_DIM
    qmap = lambda b, kv, i: (kv, b * nq + i, 0)
    cur = lambda b, kv, i: (kv, b * nq + i, 0)
    prev = lambda b, kv, i: (kv, jnp.maximum(b * nq * r + i * r - 1, 0), 0)
    omap = lambda b, kv, i: (b * nq + i, kv)
    return pl.pallas_call(
        _swa_kernel,
        grid=(batch, B_KV_HEADS, nq),
        in_specs=[
            pl.BlockSpec((B_GROUP, tq, HEAD_DIM), qmap),
            pl.BlockSpec((1, WINDOW, HEAD_DIM), prev),
            pl.BlockSpec((1, tq, HEAD_DIM), cur),
            pl.BlockSpec((1, WINDOW, HEAD_DIM), prev),
            pl.BlockSpec((1, tq, HEAD_DIM), cur),
            pl.BlockSpec((1, B_GROUP * WINDOW, 1), lambda b, kv, i: (kv, 0, 0)),
            pl.BlockSpec((tq, gw), omap),
        ],
        out_specs=pl.BlockSpec((tq, gw), omap),
        out_shape=jax.ShapeDtypeStruct((batch * seq, B_WIDTH), jnp.bfloat16),
        compiler_params=pltpu.CompilerParams(
            dimension_semantics=("arbitrary", "arbitrary", "arbitrary"), vmem_limit_bytes=V7X_VMEM_LIMIT),
        name="swa_attn",
    )(qb, kb, kb, vb, vb, sink_col, gb)


def _out_proj_kernel(ya_ref, yb_ref, wa_ref, wb_ref, x_ref, o_ref):
    acc = jnp.dot(ya_ref[...], wa_ref[...], preferred_element_type=jnp.float32)
    acc = acc + jnp.dot(yb_ref[...], wb_ref[...], preferred_element_type=jnp.float32)
    o_ref[...] = x_ref[...] + acc


def _out_proj(ya, yb, wa, wb, x2):
    m = x2.shape[0]
    bm = OUT_BM
    row = lambda i: (i, 0)
    const = lambda i: (0, 0)
    return pl.pallas_call(
        _out_proj_kernel,
        grid=(m // bm,),
        in_specs=[
            pl.BlockSpec((bm, A_WIDTH), row),
            pl.BlockSpec((bm, B_WIDTH), row),
            pl.BlockSpec((A_WIDTH, D_MODEL), const, pipeline_mode=pl.Buffered(1)),
            pl.BlockSpec((B_WIDTH, D_MODEL), const, pipeline_mode=pl.Buffered(1)),
            pl.BlockSpec((bm, D_MODEL), row),
        ],
        out_specs=pl.BlockSpec((bm, D_MODEL), row),
        out_shape=jax.ShapeDtypeStruct((m, D_MODEL), jnp.float32),
        compiler_params=pltpu.CompilerParams(
            dimension_semantics=("arbitrary",), vmem_limit_bytes=V7X_VMEM_LIMIT),
        name="out_proj",
    )(ya, yb, wa, wb, x2)


def kernel(x, positions, norm_w, w_in, q_norm_a, k_norm_a, lambda_q1, lambda_k1, lambda_q2, lambda_k2,
           subln_w, q_norm_b, k_norm_b, sinks, w_out):
    batch, seq, _ = x.shape
    depth = norm_w.shape[0]
    m = batch * seq
    f32 = jnp.float32

    inv_freq = ROPE_THETA ** (-(jnp.arange(0, HEAD_DIM, 2, dtype=f32) / HEAD_DIM))
    invf = jnp.tile(inv_freq, CHUNK // (HEAD_DIM // 2)).reshape(1, CHUNK)
    gid = jnp.arange(CHUNK) // HEAD_DIM
    gmat = jnp.where(gid[:, None] == gid[None, :], 1.0 / HEAD_DIM, 0.0).astype(jnp.bfloat16)
    pos2 = positions.reshape(m, 1)
    qk_scale = HEAD_DIM ** -0.5 * LOG2E
    tile = lambda w: jnp.tile(w.astype(f32), CHUNK // HEAD_DIM).reshape(1, CHUNK)

    x2 = x.reshape(m, D_MODEL)
    for layer in range(depth):
        lam_init = 0.8 - 0.6 * math.exp(-0.3 * layer)
        w_bf16 = w_in[layer].astype(jnp.bfloat16)
        qa, ka, va_ext, ga, qb, kb, vb, gb = _in_proj(
            x2, pos2, norm_w[layer].reshape(1, D_MODEL), w_bf16, invf, gmat,
            tile(q_norm_a[layer]) * qk_scale, tile(k_norm_a[layer]),
            tile(q_norm_b[layer]) * qk_scale, tile(k_norm_b[layer]))
        vec = lambda p: p[layer].astype(f32).reshape(1, HEAD_DIM)
        attn_args = (qa, ka, va_ext, ga, vec(lambda_q1), vec(lambda_k1), vec(lambda_q2), vec(lambda_k2),
                     subln_w[layer].astype(f32).reshape(1, A_VDIM))
        attn = functools.partial(_diff_attn, batch=batch, seq=seq, lam_init=lam_init)
        score_bound = (HEAD_DIM * qk_scale * jnp.max(jnp.abs(q_norm_a[layer].astype(f32)))
                       * jnp.max(jnp.abs(k_norm_a[layer].astype(f32))))
        ya = lax.cond(score_bound <= RAW_EXP_SCORE_BOUND,
                      functools.partial(attn, online_max=False),
                      functools.partial(attn, online_max=True), *attn_args)
        sink_col = jnp.repeat(sinks[layer].astype(f32) * LOG2E, WINDOW).reshape(B_KV_HEADS, B_GROUP * WINDOW, 1)
        yb = _swa_attn(qb, kb, vb, gb, sink_col, batch, seq)
        w_o = w_out[layer].astype(jnp.bfloat16)
        x2 = _out_proj(ya, yb, w_o[:A_WIDTH], w_o[A_WIDTH:], x2)
    return x2.reshape(batch, seq, D_MODEL)
```

```python
import functools
import math

import jax
import jax.numpy as jnp
from jax import lax
from jax.experimental import pallas as pl
from jax.experimental.pallas import tpu as pltpu

D_MODEL = 2048
HEAD_DIM = 64
ROPE_THETA = 10000.0
EPS = 1e-6
NEG_INF = -1e30
LOG2E = math.log2(math.e)

A_HEADS = 8
A_VDIM = 128
A_WIDTH = 1024
B_HEADS = 16
B_KV_HEADS = 2
B_GROUP = 8
B_WIDTH = 1024
WINDOW = 128
PROJ_WIDTH = 6400

OFF_QA, OFF_KA, OFF_VA, OFF_GA, OFF_QB, OFF_KB, OFF_VB, OFF_GB = 0, 1024, 2048, 3072, 4096, 5120, 5248, 5376

V7X_VMEM_LIMIT = 56 * 1024 * 1024
LANES = 128
CHUNK = 256

ROPE_ROWS = 512
PROJ_BM = 512
ATT_TQ = 1024
ATT_QS = 256
SWA_TQ = 512
OUT_BM = 512

RAW_EXP_SCORE_BOUND = 60.0


def _rope_kernel(pos_ref, invf_ref, cos_ref, sin_ref):
    ang = pos_ref[...] * invf_ref[...]
    cos_ref[...] = jnp.cos(ang)
    sin_ref[...] = jnp.sin(ang)


def _rope_tables(positions, m):
    half = HEAD_DIM // 2
    f32 = jnp.float32
    inv_freq = ROPE_THETA ** (-(jnp.arange(0, HEAD_DIM, 2, dtype=f32) / HEAD_DIM))
    per_row = LANES // half
    pos_rep = jnp.repeat(positions.reshape(m).astype(f32), half).reshape(m // per_row, LANES)
    invf = jnp.tile(inv_freq, per_row).reshape(1, LANES)
    row = lambda i: (i, 0)
    cos_c, sin_c = pl.pallas_call(
        _rope_kernel,
        grid=(m // per_row // ROPE_ROWS,),
        in_specs=[pl.BlockSpec((ROPE_ROWS, LANES), row), pl.BlockSpec((1, LANES), lambda i: (0, 0))],
        out_specs=(pl.BlockSpec((ROPE_ROWS, LANES), row), pl.BlockSpec((ROPE_ROWS, LANES), row)),
        out_shape=(jax.ShapeDtypeStruct((m // per_row, LANES), f32),) * 2,
        name="rope_table",
    )(pos_rep, invf)
    sign = jnp.tile(jnp.concatenate([-jnp.ones((half,), f32), jnp.ones((half,), f32)]), LANES // HEAD_DIM)
    cos_t = jnp.tile(cos_c.reshape(m, half), (1, per_row))
    sin_t = jnp.tile(sin_c.reshape(m, half), (1, per_row)) * sign
    return cos_t, sin_t


def _in_proj_kernel(x_ref, cos_ref, sin_ref, nw_ref, w_ref, gmat_ref,
                    qna_ref, kna_ref, qnb_ref, knb_ref,
                    qa_ref, ka_ref, va_ref, ga_ref, qb_ref, kb_ref, vb_ref, gb_ref):
    bm = x_ref.shape[0]
    x = x_ref[...]
    ms = jnp.mean(x * x, axis=-1, keepdims=True)
    h = ((x * lax.rsqrt(ms + EPS)) * nw_ref[...]).astype(jnp.bfloat16)

    cos = jnp.tile(cos_ref[...], (1, CHUNK // LANES))
    sin_signed = jnp.tile(sin_ref[...], (1, CHUNK // LANES))
    lane = lax.broadcasted_iota(jnp.int32, (bm, CHUNK), 1)
    first_half = (lane % HEAD_DIM) < (HEAD_DIM // 2)
    gmat = gmat_ref[...]
    ones_col = (lax.broadcasted_iota(jnp.int32, (bm, LANES), 1) == 0).astype(jnp.bfloat16)

    def proj(c0, width=CHUNK):
        return jnp.dot(h, w_ref[:, c0:c0 + width], preferred_element_type=jnp.float32)

    def norm_rope(t, w):
        msq = jnp.dot((t * t).astype(jnp.bfloat16), gmat, preferred_element_type=jnp.float32)
        y = t * lax.rsqrt(msq + EPS) * w
        rot = jnp.where(first_half, pltpu.roll(y, CHUNK - HEAD_DIM // 2, 1), pltpu.roll(y, HEAD_DIM // 2, 1))
        return y * cos + rot * sin_signed

    def silu(t):
        return t / (1.0 + jnp.exp(-t))

    for c in range(A_WIDTH // CHUNK):
        o = c * CHUNK
        qa_ref[:, o:o + CHUNK] = norm_rope(proj(OFF_QA + o), qna_ref[...]).astype(jnp.bfloat16)
        ka_ref[:, o:o + CHUNK] = norm_rope(proj(OFF_KA + o), kna_ref[...]).astype(jnp.bfloat16)
        va = proj(OFF_VA + o).astype(jnp.bfloat16)
        for j in range(CHUNK // A_VDIM):
            base = (c * (CHUNK // A_VDIM) + j) * 2 * A_VDIM
            va_ref[:, base:base + A_VDIM] = va[:, j * A_VDIM:(j + 1) * A_VDIM]
            va_ref[:, base + A_VDIM:base + 2 * A_VDIM] = ones_col
        ga_ref[:, o:o + CHUNK] = silu(proj(OFF_GA + o)).astype(jnp.bfloat16)
        gb_ref[:, o:o + CHUNK] = silu(proj(OFF_GB + o)).astype(jnp.bfloat16)
        qb = norm_rope(proj(OFF_QB + o), qnb_ref[...]).astype(jnp.bfloat16)
        for j in range(CHUNK // HEAD_DIM):
            qb_ref[c * (CHUNK // HEAD_DIM) + j] = qb[:, j * HEAD_DIM:(j + 1) * HEAD_DIM]
    kv = proj(OFF_KB)
    kbn = norm_rope(kv, knb_ref[...]).astype(jnp.bfloat16)
    for j in range(B_KV_HEADS):
        kb_ref[j] = kbn[:, j * HEAD_DIM:(j + 1) * HEAD_DIM]
    v_lo = lane < CHUNK - HEAD_DIM
    v0 = jnp.where(v_lo, kv, pltpu.roll(kv, HEAD_DIM, 1))[:, LANES:]
    v1 = jnp.where(v_lo, pltpu.roll(kv, CHUNK - HEAD_DIM, 1), kv)[:, LANES:]
    vb_ref[0] = jnp.concatenate([v0, v0], axis=1).astype(jnp.bfloat16)
    vb_ref[1] = jnp.concatenate([v1, v1], axis=1).astype(jnp.bfloat16)


def _in_proj(x2, cos_t, sin_t, norm_w, w_bf16, gmat, qna, kna, qnb, knb):
    m = x2.shape[0]
    bm = PROJ_BM
    row = lambda i: (i, 0)
    const = lambda i: (0, 0)
    hm = lambda i: (0, i, 0)
    bf = jnp.bfloat16
    out_shape = (
        jax.ShapeDtypeStruct((m, A_WIDTH), bf),
        jax.ShapeDtypeStruct((m, A_WIDTH), bf),
        jax.ShapeDtypeStruct((m, 2 * A_WIDTH), bf),
        jax.ShapeDtypeStruct((m, A_WIDTH), bf),
        jax.ShapeDtypeStruct((B_HEADS, m, HEAD_DIM), bf),
        jax.ShapeDtypeStruct((B_KV_HEADS, m, HEAD_DIM), bf),
        jax.ShapeDtypeStruct((B_KV_HEADS, m, 4 * HEAD_DIM), bf),
        jax.ShapeDtypeStruct((m, B_WIDTH), bf),
    )
    out_specs = (
        pl.BlockSpec((bm, A_WIDTH), row), pl.BlockSpec((bm, A_WIDTH), row),
        pl.BlockSpec((bm, 2 * A_WIDTH), row), pl.BlockSpec((bm, A_WIDTH), row),
        pl.BlockSpec((B_HEADS, bm, HEAD_DIM), hm),
        pl.BlockSpec((B_KV_HEADS, bm, HEAD_DIM), hm),
        pl.BlockSpec((B_KV_HEADS, bm, 4 * HEAD_DIM), hm),
        pl.BlockSpec((bm, B_WIDTH), row),
    )
    in_specs = [
        pl.BlockSpec((bm, D_MODEL), row),
        pl.BlockSpec((bm, LANES), row),
        pl.BlockSpec((bm, LANES), row),
        pl.BlockSpec((1, D_MODEL), const),
        pl.BlockSpec((D_MODEL, PROJ_WIDTH), const, pipeline_mode=pl.Buffered(1)),
        pl.BlockSpec((CHUNK, CHUNK), const),
        pl.BlockSpec((1, CHUNK), const), pl.BlockSpec((1, CHUNK), const),
        pl.BlockSpec((1, CHUNK), const), pl.BlockSpec((1, CHUNK), const),
    ]
    return pl.pallas_call(
        _in_proj_kernel,
        grid=(m // bm,),
        in_specs=in_specs,
        out_specs=out_specs,
        out_shape=out_shape,
        compiler_params=pltpu.CompilerParams(
            dimension_semantics=("arbitrary",), vmem_limit_bytes=V7X_VMEM_LIMIT),
        name="in_proj",
    )(x2, cos_t, sin_t, norm_w, w_bf16, gmat, qna, kna, qnb, knb)


def _diff_attn_kernel(q_ref, k_ref, v_ref, g_ref, lq1_ref, lk1_ref, lq2_ref, lk2_ref, sw_ref,
                      o_ref, acc_ref, m_ref, *, lam_init, online_max):
    tq = q_ref.shape[0]
    nsub = tq // ATT_QS
    qi = pl.program_id(2)

    q = q_ref[...]
    lane = lax.broadcasted_iota(jnp.int32, q.shape, 1)
    zero = jnp.zeros_like(q)
    qs = (jnp.where(lane < HEAD_DIM, q, zero), jnp.where(lane >= HEAD_DIM, q, zero))

    acc_ref[...] = jnp.zeros(acc_ref.shape, jnp.float32)
    if online_max:
        m_ref[...] = jnp.full(m_ref.shape, NEG_INF, jnp.float32)

    def unit(c, r, start, nk, masked):
        rows = slice(r * ATT_QS, (r + 1) * ATT_QS)
        k = k_ref[pl.ds(start, nk), :]
        v = v_ref[pl.ds(start, nk), :]
        s = lax.dot_general(qs[c][rows], k, (((1,), (1,)), ((), ())), preferred_element_type=jnp.float32)
        if masked:
            row = lax.broadcasted_iota(jnp.int32, s.shape, 0)
            col = lax.broadcasted_iota(jnp.int32, s.shape, 1)
            s = jnp.where(col <= row + (nk - ATT_QS), s, NEG_INF)
        if online_max:
            m_old = m_ref[c, rows, :]
            m_new = jnp.maximum(m_old, jnp.max(s, axis=-1, keepdims=True))
            alpha = jnp.exp2(m_old - m_new)
            m_ref[c, rows, :] = m_new
            p = jnp.exp2(s - jnp.tile(m_new, (1, nk // LANES)))
            pv = jnp.dot(p.astype(jnp.bfloat16), v, preferred_element_type=jnp.float32)
            acc_ref[c, rows, :] = acc_ref[c, rows, :] * jnp.tile(alpha, (1, 2)) + pv
        else:
            p = jnp.exp2(s)
            pv = jnp.dot(p.astype(jnp.bfloat16), v, preferred_element_type=jnp.float32)
            acc_ref[c, rows, :] += pv

    def body(j, carry):
        start = pl.multiple_of(j * tq, tq)
        for r in range(nsub):
            for c in range(2):
                unit(c, r, start, tq, False)
        return carry

    lax.fori_loop(0, qi, body, 0)
    diag = pl.multiple_of(qi * tq, tq)
    for r in range(nsub):
        for c in range(2):
            unit(c, r, diag, (r + 1) * ATT_QS, True)

    lam = (jnp.exp(jnp.sum(lq1_ref[...] * lk1_ref[...], axis=-1, keepdims=True))
           - jnp.exp(jnp.sum(lq2_ref[...] * lk2_ref[...], axis=-1, keepdims=True)) + lam_init)
    a1 = acc_ref[0]
    a2 = acc_ref[1]
    o = a1[:, :A_VDIM] / a1[:, A_VDIM:A_VDIM + 1] - lam * (a2[:, :A_VDIM] / a2[:, A_VDIM:A_VDIM + 1])
    o = o * lax.rsqrt(jnp.mean(o * o, axis=-1, keepdims=True) + EPS) * sw_ref[...] * (1.0 - lam_init)
    o_ref[...] = (o * g_ref[...].astype(jnp.float32)).astype(o_ref.dtype)


def _diff_attn(qa, ka, va_ext, ga, lq1, lk1, lq2, lk2, subln_w, *, batch, seq, lam_init, online_max):
    tq = ATT_TQ
    nq = seq // tq
    qmap = lambda b, h, i: (b * nq + i, h)
    kvmap = lambda b, h, i: (b, h)
    const = lambda b, h, i: (0, 0)
    vec = pl.BlockSpec((1, HEAD_DIM), const)
    return pl.pallas_call(
        functools.partial(_diff_attn_kernel, lam_init=lam_init, online_max=online_max),
        grid=(batch, A_HEADS, nq),
        in_specs=[
            pl.BlockSpec((tq, A_VDIM), qmap),
            pl.BlockSpec((seq, A_VDIM), kvmap),
            pl.BlockSpec((seq, 2 * A_VDIM), kvmap),
            pl.BlockSpec((tq, A_VDIM), qmap),
            vec, vec, vec, vec,
            pl.BlockSpec((1, A_VDIM), const),
        ],
        out_specs=pl.BlockSpec((tq, A_VDIM), qmap),
        out_shape=jax.ShapeDtypeStruct((batch * seq, A_WIDTH), jnp.bfloat16),
        scratch_shapes=[
            pltpu.VMEM((2, tq, 2 * A_VDIM), jnp.float32),
            pltpu.VMEM((2, tq, LANES), jnp.float32),
        ],
        compiler_params=pltpu.CompilerParams(
            dimension_semantics=("arbitrary", "arbitrary", "arbitrary"), vmem_limit_bytes=V7X_VMEM_LIMIT),
        name="diff_attn_online" if online_max else "diff_attn",
    )(qa, ka, va_ext, ga, lq1, lk1, lq2, lk2, subln_w)


def _swa_kernel(q_ref, kp_ref, kc_ref, vp_ref, vc_ref, sink_ref, g_ref, o_ref):
    tq = q_ref.shape[1]
    blk = WINDOW
    i = pl.program_id(2)
    rows = B_GROUP * blk
    per_half = CHUNK // HEAD_DIM
    r = lax.broadcasted_iota(jnp.int32, (rows, 2 * blk), 0) % blk
    c = lax.broadcasted_iota(jnp.int32, (rows, 2 * blk), 1)
    rel = r + blk - c
    in_window = (rel >= 0) & (rel < WINDOW)
    sink = sink_ref[0]
    vgroup = lax.broadcasted_iota(jnp.int32, (2 * blk, 4 * HEAD_DIM), 1) // HEAD_DIM
    ogroup = lax.broadcasted_iota(jnp.int32, (blk, 4 * HEAD_DIM), 1) // HEAD_DIM
    for n in range(tq // blk):
        q = q_ref[:, n * blk:(n + 1) * blk, :].reshape(rows, HEAD_DIM)
        if n == 0:
            k = jnp.concatenate([kp_ref[0], kc_ref[0, 0:blk, :]], axis=0)
            v = jnp.concatenate([vp_ref[0], vc_ref[0, 0:blk, :]], axis=0)
            mask = in_window & ((c >= blk) | (i > 0))
        else:
            k = kc_ref[0, (n - 1) * blk:(n + 1) * blk, :]
            v = vc_ref[0, (n - 1) * blk:(n + 1) * blk, :]
            mask = in_window
        s = lax.dot_general(q, k, (((1,), (1,)), ((), ())), preferred_element_type=jnp.float32)
        s = jnp.where(mask, s, NEG_INF)
        m = jnp.maximum(sink, jnp.max(s, axis=-1, keepdims=True))
        p = jnp.exp2(s - jnp.tile(m, (1, 2)))
        denom = jnp.exp2(sink - m) + jnp.sum(p, axis=-1, keepdims=True)
        rinv = 1.0 / denom
        pb = p.astype(jnp.bfloat16)
        vsel = [jnp.where(vgroup == j, v, jnp.zeros_like(v)) for j in range(per_half)]
        halves = []
        for hf in range(B_GROUP // per_half):
            acc = None
            scale = None
            for j in range(per_half):
                g = hf * per_half + j
                part = jnp.dot(pb[g * blk:(g + 1) * blk], vsel[j], preferred_element_type=jnp.float32)
                acc = part if acc is None else acc + part
                rg = jnp.tile(rinv[g * blk:(g + 1) * blk], (1, 2))
                scale = rg if scale is None else jnp.where(ogroup == j, rg, scale)
            halves.append(acc * scale)
        o = jnp.concatenate(halves, axis=1)
        gate = g_ref[n * blk:(n + 1) * blk, :].astype(jnp.float32)
        o_ref[n * blk:(n + 1) * blk, :] = (o * gate).astype(o_ref.dtype)


def _swa_attn(qb, kb, vb_rep, gb, sink_rep, batch, seq):
    tq = SWA_TQ
    nq = seq // tq
    r = tq // WINDOW
    gw = B_GROUP * HEAD_DIM
    cur = lambda b, kv, i: (kv, b * nq + i, 0)
    prev = lambda b, kv, i: (kv, jnp.maximum(b * nq * r + i * r - 1, 0), 0)
    omap = lambda b, kv, i: (b * nq + i, kv)
    return pl.pallas_call(
        _swa_kernel,
        grid=(batch, B_KV_HEADS, nq),
        in_specs=[
            pl.BlockSpec((B_GROUP, tq, HEAD_DIM), cur),
            pl.BlockSpec((1, WINDOW, HEAD_DIM), prev),
            pl.BlockSpec((1, tq, HEAD_DIM), cur),
            pl.BlockSpec((1, WINDOW, 4 * HEAD_DIM), prev),
            pl.BlockSpec((1, tq, 4 * HEAD_DIM), cur),
            pl.BlockSpec((1, B_GROUP * WINDOW, LANES), lambda b, kv, i: (kv, 0, 0)),
            pl.BlockSpec((tq, gw), omap),
        ],
        out_specs=pl.BlockSpec((tq, gw), omap),
        out_shape=jax.ShapeDtypeStruct((batch * seq, B_WIDTH), jnp.bfloat16),
        compiler_params=pltpu.CompilerParams(
            dimension_semantics=("arbitrary", "arbitrary", "arbitrary"), vmem_limit_bytes=V7X_VMEM_LIMIT),
        name="swa_attn",
    )(qb, kb, kb, vb_rep, vb_rep, sink_rep, gb)


def _out_proj_kernel(ya_ref, yb_ref, wa_ref, wb_ref, x_ref, o_ref):
    acc = jnp.dot(ya_ref[...], wa_ref[...], preferred_element_type=jnp.float32)
    acc = acc + jnp.dot(yb_ref[...], wb_ref[...], preferred_element_type=jnp.float32)
    o_ref[...] = x_ref[...] + acc


def _out_proj(ya, yb, wa, wb, x2):
    m = x2.shape[0]
    bm = OUT_BM
    row = lambda i: (i, 0)
    const = lambda i: (0, 0)
    return pl.pallas_call(
        _out_proj_kernel,
        grid=(m // bm,),
        in_specs=[
            pl.BlockSpec((bm, A_WIDTH), row),
            pl.BlockSpec((bm, B_WIDTH), row),
            pl.BlockSpec((A_WIDTH, D_MODEL), const, pipeline_mode=pl.Buffered(1)),
            pl.BlockSpec((B_WIDTH, D_MODEL), const, pipeline_mode=pl.Buffered(1)),
            pl.BlockSpec((bm, D_MODEL), row),
        ],
        out_specs=pl.BlockSpec((bm, D_MODEL), row),
        out_shape=jax.ShapeDtypeStruct((m, D_MODEL), jnp.float32),
        compiler_params=pltpu.CompilerParams(
            dimension_semantics=("arbitrary",), vmem_limit_bytes=V7X_VMEM_LIMIT),
        name="out_proj",
    )(ya, yb, wa, wb, x2)


def kernel(x, positions, norm_w, w_in, q_norm_a, k_norm_a, lambda_q1, lambda_k1, lambda_q2, lambda_k2,
           subln_w, q_norm_b, k_norm_b, sinks, w_out):
    batch, seq, _ = x.shape
    depth = norm_w.shape[0]
    m = batch * seq
    f32 = jnp.float32

    cos_t, sin_t = _rope_tables(positions, m)
    gid = jnp.arange(CHUNK) // HEAD_DIM
    gmat = jnp.where(gid[:, None] == gid[None, :], 1.0 / HEAD_DIM, 0.0).astype(jnp.bfloat16)
    qk_scale = HEAD_DIM ** -0.5 * LOG2E
    tile = lambda w: jnp.tile(w.astype(f32), CHUNK // HEAD_DIM).reshape(1, CHUNK)

    x2 = x.reshape(m, D_MODEL)
    for layer in range(depth):
        lam_init = 0.8 - 0.6 * math.exp(-0.3 * layer)
        w_bf16 = w_in[layer].astype(jnp.bfloat16)
        qa, ka, va_ext, ga, qb, kb, vb_rep, gb = _in_proj(
            x2, cos_t, sin_t, norm_w[layer].reshape(1, D_MODEL), w_bf16, gmat,
            tile(q_norm_a[layer]) * qk_scale, tile(k_norm_a[layer]),
            tile(q_norm_b[layer]) * qk_scale, tile(k_norm_b[layer]))
        vec = lambda p: p[layer].astype(f32).reshape(1, HEAD_DIM)
        attn_args = (qa, ka, va_ext, ga, vec(lambda_q1), vec(lambda_k1), vec(lambda_q2), vec(lambda_k2),
                     subln_w[layer].astype(f32).reshape(1, A_VDIM))
        attn = functools.partial(_diff_attn, batch=batch, seq=seq, lam_init=lam_init)
        score_bound = (HEAD_DIM * qk_scale * jnp.max(jnp.abs(q_norm_a[layer].astype(f32)))
                       * jnp.max(jnp.abs(k_norm_a[layer].astype(f32))))
        ya = lax.cond(score_bound <= RAW_EXP_SCORE_BOUND,
                      functools.partial(attn, online_max=False),
                      functools.partial(attn, online_max=True), *attn_args)
        sink_rep = jnp.broadcast_to(
            jnp.repeat(sinks[layer].astype(f32) * LOG2E, WINDOW).reshape(B_KV_HEADS, B_GROUP * WINDOW, 1),
            (B_KV_HEADS, B_GROUP * WINDOW, LANES))
        yb = _swa_attn(qb, kb, vb_rep, gb, sink_rep, batch, seq)
        w_o = w_out[layer].astype(jnp.bfloat16)
        x2 = _out_proj(ya, yb, w_o[:A_WIDTH], w_o[A_WIDTH:], x2)
    return x2.reshape(batch, seq, D_MODEL)
```

```python
import functools
import math

import jax
import jax.numpy as jnp
from jax import lax
from jax.experimental import pallas as pl
from jax.experimental.pallas import tpu as pltpu

D_MODEL = 2048
HEAD_DIM = 64
ROPE_THETA = 10000.0
EPS = 1e-6
NEG_INF = -1e30
LOG2E = math.log2(math.e)

A_HEADS = 8
A_VDIM = 128
A_WIDTH = 1024
B_HEADS = 16
B_KV_HEADS = 2
B_GROUP = 8
B_WIDTH = 1024
WINDOW = 128
PROJ_WIDTH = 6400

OFF_QA, OFF_KA, OFF_VA, OFF_GA, OFF_QB, OFF_KB, OFF_VB, OFF_GB = 0, 1024, 2048, 3072, 4096, 5120, 5248, 5376

V7X_VMEM_LIMIT = 56 * 1024 * 1024
LANES = 128
CHUNK = 256

ROPE_ROWS = 512
PROJ_BM = 512
ATT_TQ = 1024
ATT_QS = 256
SWA_TQ = 512
OUT_BM = 512

RAW_EXP_SCORE_BOUND = 60.0


def _rope_kernel(pos_ref, invf_ref, cos_ref, sin_ref):
    ang = pos_ref[...] * invf_ref[...]
    cos_ref[...] = jnp.cos(ang)
    sin_ref[...] = jnp.sin(ang)


def _rope_tables(positions, m):
    half = HEAD_DIM // 2
    f32 = jnp.float32
    inv_freq = ROPE_THETA ** (-(jnp.arange(0, HEAD_DIM, 2, dtype=f32) / HEAD_DIM))
    per_row = LANES // half
    pos_rep = jnp.repeat(positions.reshape(m).astype(f32), half).reshape(m // per_row, LANES)
    invf = jnp.tile(inv_freq, per_row).reshape(1, LANES)
    row = lambda i: (i, 0)
    cos_c, sin_c = pl.pallas_call(
        _rope_kernel,
        grid=(m // per_row // ROPE_ROWS,),
        in_specs=[pl.BlockSpec((ROPE_ROWS, LANES), row), pl.BlockSpec((1, LANES), lambda i: (0, 0))],
        out_specs=(pl.BlockSpec((ROPE_ROWS, LANES), row), pl.BlockSpec((ROPE_ROWS, LANES), row)),
        out_shape=(jax.ShapeDtypeStruct((m // per_row, LANES), f32),) * 2,
        name="rope_table",
    )(pos_rep, invf)
    sign = jnp.tile(jnp.concatenate([-jnp.ones((half,), f32), jnp.ones((half,), f32)]), LANES // HEAD_DIM)
    cos_t = jnp.tile(cos_c.reshape(m, half), (1, per_row))
    sin_t = jnp.tile(sin_c.reshape(m, half), (1, per_row)) * sign
    return cos_t, sin_t


def _in_proj_kernel(x_ref, cos_ref, sin_ref, nw_ref, w_ref, gmat_ref,
                    qna_ref, kna_ref, qnb_ref, knb_ref,
                    qa_ref, ka_ref, va_ref, ga_ref, qb_ref, kb_ref, vb_ref, gb_ref):
    bm = x_ref.shape[0]
    x = x_ref[...]
    ms = jnp.mean(x * x, axis=-1, keepdims=True)
    h = ((x * lax.rsqrt(ms + EPS)) * nw_ref[...]).astype(jnp.bfloat16)

    cos = jnp.tile(cos_ref[...], (1, CHUNK // LANES))
    sin_signed = jnp.tile(sin_ref[...], (1, CHUNK // LANES))
    lane = lax.broadcasted_iota(jnp.int32, (bm, CHUNK), 1)
    first_half = (lane % HEAD_DIM) < (HEAD_DIM // 2)
    gmat = gmat_ref[...]
    ones_col = (lax.broadcasted_iota(jnp.int32, (bm, LANES), 1) == 0).astype(jnp.bfloat16)

    def proj(c0, width=CHUNK):
        return jnp.dot(h, w_ref[:, c0:c0 + width], preferred_element_type=jnp.float32)

    def norm_rope(t, w):
        msq = jnp.dot((t * t).astype(jnp.bfloat16), gmat, preferred_element_type=jnp.float32)
        y = t * lax.rsqrt(msq + EPS) * w
        rot = jnp.where(first_half, pltpu.roll(y, CHUNK - HEAD_DIM // 2, 1), pltpu.roll(y, HEAD_DIM // 2, 1))
        return y * cos + rot * sin_signed

    def silu(t):
        return t / (1.0 + jnp.exp(-t))

    def put_qa(t, o):
        qa_ref[:, o:o + CHUNK] = norm_rope(t, qna_ref[...]).astype(jnp.bfloat16)

    def put_ka(t, o):
        ka_ref[:, o:o + CHUNK] = norm_rope(t, kna_ref[...]).astype(jnp.bfloat16)

    def put_va(t, o):
        va = t.astype(jnp.bfloat16)
        for j in range(CHUNK // A_VDIM):
            base = (o // A_VDIM + j) * 2 * A_VDIM
            va_ref[:, base:base + A_VDIM] = va[:, j * A_VDIM:(j + 1) * A_VDIM]
            va_ref[:, base + A_VDIM:base + 2 * A_VDIM] = ones_col

    def put_ga(t, o):
        ga_ref[:, o:o + CHUNK] = silu(t).astype(jnp.bfloat16)

    def put_gb(t, o):
        gb_ref[:, o:o + CHUNK] = silu(t).astype(jnp.bfloat16)

    def put_qb(t, o):
        qb = norm_rope(t, qnb_ref[...]).astype(jnp.bfloat16)
        for j in range(CHUNK // HEAD_DIM):
            qb_ref[o // HEAD_DIM + j] = qb[:, j * HEAD_DIM:(j + 1) * HEAD_DIM]

    def put_kvb(kv, o):
        kbn = norm_rope(kv, knb_ref[...]).astype(jnp.bfloat16)
        for j in range(B_KV_HEADS):
            kb_ref[j] = kbn[:, j * HEAD_DIM:(j + 1) * HEAD_DIM]
        v_lo = lane < CHUNK - HEAD_DIM
        v0 = jnp.where(v_lo, kv, pltpu.roll(kv, HEAD_DIM, 1))[:, LANES:]
        v1 = jnp.where(v_lo, pltpu.roll(kv, CHUNK - HEAD_DIM, 1), kv)[:, LANES:]
        vb_ref[0] = jnp.concatenate([v0, v0], axis=1).astype(jnp.bfloat16)
        vb_ref[1] = jnp.concatenate([v1, v1], axis=1).astype(jnp.bfloat16)

    tasks = []
    for c in range(A_WIDTH // CHUNK):
        o = c * CHUNK
        tasks += [(OFF_QA + o, put_qa, o), (OFF_VA + o, put_va, o), (OFF_KA + o, put_ka, o),
                  (OFF_GA + o, put_ga, o), (OFF_QB + o, put_qb, o), (OFF_GB + o, put_gb, o)]
    tasks.append((OFF_KB, put_kvb, 0))
    pending = None
    for col, put, o in tasks:
        t = proj(col)
        if pending is not None:
            pending[0](pending[1], pending[2])
        pending = (put, t, o)
    pending[0](pending[1], pending[2])


def _in_proj(x2, cos_t, sin_t, norm_w, w_bf16, gmat, qna, kna, qnb, knb):
    m = x2.shape[0]
    bm = PROJ_BM
    row = lambda i: (i, 0)
    const = lambda i: (0, 0)
    hm = lambda i: (0, i, 0)
    bf = jnp.bfloat16
    out_shape = (
        jax.ShapeDtypeStruct((m, A_WIDTH), bf),
        jax.ShapeDtypeStruct((m, A_WIDTH), bf),
        jax.ShapeDtypeStruct((m, 2 * A_WIDTH), bf),
        jax.ShapeDtypeStruct((m, A_WIDTH), bf),
        jax.ShapeDtypeStruct((B_HEADS, m, HEAD_DIM), bf),
        jax.ShapeDtypeStruct((B_KV_HEADS, m, HEAD_DIM), bf),
        jax.ShapeDtypeStruct((B_KV_HEADS, m, 4 * HEAD_DIM), bf),
        jax.ShapeDtypeStruct((m, B_WIDTH), bf),
    )
    out_specs = (
        pl.BlockSpec((bm, A_WIDTH), row), pl.BlockSpec((bm, A_WIDTH), row),
        pl.BlockSpec((bm, 2 * A_WIDTH), row), pl.BlockSpec((bm, A_WIDTH), row),
        pl.BlockSpec((B_HEADS, bm, HEAD_DIM), hm),
        pl.BlockSpec((B_KV_HEADS, bm, HEAD_DIM), hm),
        pl.BlockSpec((B_KV_HEADS, bm, 4 * HEAD_DIM), hm),
        pl.BlockSpec((bm, B_WIDTH), row),
    )
    in_specs = [
        pl.BlockSpec((bm, D_MODEL), row),
        pl.BlockSpec((bm, LANES), row),
        pl.BlockSpec((bm, LANES), row),
        pl.BlockSpec((1, D_MODEL), const),
        pl.BlockSpec((D_MODEL, PROJ_WIDTH), const, pipeline_mode=pl.Buffered(1)),
        pl.BlockSpec((CHUNK, CHUNK), const),
        pl.BlockSpec((1, CHUNK), const), pl.BlockSpec((1, CHUNK), const),
        pl.BlockSpec((1, CHUNK), const), pl.BlockSpec((1, CHUNK), const),
    ]
    return pl.pallas_call(
        _in_proj_kernel,
        grid=(m // bm,),
        in_specs=in_specs,
        out_specs=out_specs,
        out_shape=out_shape,
        compiler_params=pltpu.CompilerParams(
            dimension_semantics=("arbitrary",), vmem_limit_bytes=V7X_VMEM_LIMIT),
        name="in_proj",
    )(x2, cos_t, sin_t, norm_w, w_bf16, gmat, qna, kna, qnb, knb)


def _diff_attn_kernel(q_ref, k_ref, v_ref, g_ref, lq1_ref, lk1_ref, lq2_ref, lk2_ref, sw_ref,
                      o_ref, acc_ref, m_ref, *, lam_init, online_max):
    tq = q_ref.shape[0]
    nsub = tq // ATT_QS
    qi = pl.program_id(2)

    q = q_ref[...]
    lane = lax.broadcasted_iota(jnp.int32, q.shape, 1)
    zero = jnp.zeros_like(q)
    qs = (jnp.where(lane < HEAD_DIM, q, zero), jnp.where(lane >= HEAD_DIM, q, zero))

    acc_ref[...] = jnp.zeros(acc_ref.shape, jnp.float32)
    if online_max:
        m_ref[...] = jnp.full(m_ref.shape, NEG_INF, jnp.float32)

    def scores(c, r, start, nk, masked):
        rows = slice(r * ATT_QS, (r + 1) * ATT_QS)
        k = k_ref[pl.ds(start, nk), :]
        s = lax.dot_general(qs[c][rows], k, (((1,), (1,)), ((), ())), preferred_element_type=jnp.float32)
        if masked:
            row = lax.broadcasted_iota(jnp.int32, s.shape, 0)
            col = lax.broadcasted_iota(jnp.int32, s.shape, 1)
            s = jnp.where(col <= row + (nk - ATT_QS), s, NEG_INF)
        return s

    def accumulate(s, c, r, start, nk):
        rows = slice(r * ATT_QS, (r + 1) * ATT_QS)
        v = v_ref[pl.ds(start, nk), :]
        if online_max:
            m_old = m_ref[c, rows, :]
            m_new = jnp.maximum(m_old, jnp.max(s, axis=-1, keepdims=True))
            alpha = jnp.exp2(m_old - m_new)
            m_ref[c, rows, :] = m_new
            p = jnp.exp2(s - jnp.tile(m_new, (1, nk // LANES)))
            pv = jnp.dot(p.astype(jnp.bfloat16), v, preferred_element_type=jnp.float32)
            acc_ref[c, rows, :] = acc_ref[c, rows, :] * jnp.tile(alpha, (1, 2)) + pv
        else:
            p = jnp.exp2(s)
            pv = jnp.dot(p.astype(jnp.bfloat16), v, preferred_element_type=jnp.float32)
            acc_ref[c, rows, :] += pv

    def sweep(start, nk_of, masked):
        units = [(c, r) for r in range(nsub) for c in range(2)]
        s_next = scores(*units[0], start, nk_of(units[0][1]), masked)
        for idx, (c, r) in enumerate(units):
            s = s_next
            if idx + 1 < len(units):
                cn, rn = units[idx + 1]
                s_next = scores(cn, rn, start, nk_of(rn), masked)
            accumulate(s, c, r, start, nk_of(r))

    def body(j, carry):
        sweep(pl.multiple_of(j * tq, tq), lambda r: tq, False)
        return carry

    lax.fori_loop(0, qi, body, 0)
    sweep(pl.multiple_of(qi * tq, tq), lambda r: (r + 1) * ATT_QS, True)

    lam = (jnp.exp(jnp.sum(lq1_ref[...] * lk1_ref[...], axis=-1, keepdims=True))
           - jnp.exp(jnp.sum(lq2_ref[...] * lk2_ref[...], axis=-1, keepdims=True)) + lam_init)
    a1 = acc_ref[0]
    a2 = acc_ref[1]
    o = a1[:, :A_VDIM] / a1[:, A_VDIM:A_VDIM + 1] - lam * (a2[:, :A_VDIM] / a2[:, A_VDIM:A_VDIM + 1])
    o = o * lax.rsqrt(jnp.mean(o * o, axis=-1, keepdims=True) + EPS) * sw_ref[...] * (1.0 - lam_init)
    o_ref[...] = (o * g_ref[...].astype(jnp.float32)).astype(o_ref.dtype)


def _diff_attn(qa, ka, va_ext, ga, lq1, lk1, lq2, lk2, subln_w, *, batch, seq, lam_init, online_max):
    tq = ATT_TQ
    nq = seq // tq
    qmap = lambda b, h, i: (b * nq + i, h)
    kvmap = lambda b, h, i: (b, h)
    const = lambda b, h, i: (0, 0)
    vec = pl.BlockSpec((1, HEAD_DIM), const)
    return pl.pallas_call(
        functools.partial(_diff_attn_kernel, lam_init=lam_init, online_max=online_max),
        grid=(batch, A_HEADS, nq),
        in_specs=[
            pl.BlockSpec((tq, A_VDIM), qmap),
            pl.BlockSpec((seq, A_VDIM), kvmap),
            pl.BlockSpec((seq, 2 * A_VDIM), kvmap),
            pl.BlockSpec((tq, A_VDIM), qmap),
            vec, vec, vec, vec,
            pl.BlockSpec((1, A_VDIM), const),
        ],
        out_specs=pl.BlockSpec((tq, A_VDIM), qmap),
        out_shape=jax.ShapeDtypeStruct((batch * seq, A_WIDTH), jnp.bfloat16),
        scratch_shapes=[
            pltpu.VMEM((2, tq, 2 * A_VDIM), jnp.float32),
            pltpu.VMEM((2, tq, LANES), jnp.float32),
        ],
        compiler_params=pltpu.CompilerParams(
            dimension_semantics=("arbitrary", "arbitrary", "arbitrary"), vmem_limit_bytes=V7X_VMEM_LIMIT),
        name="diff_attn_online" if online_max else "diff_attn",
    )(qa, ka, va_ext, ga, lq1, lk1, lq2, lk2, subln_w)


def _swa_kernel(q_ref, kp_ref, kc_ref, vp_ref, vc_ref, sink_ref, g_ref, o_ref):
    tq = q_ref.shape[1]
    blk = WINDOW
    i = pl.program_id(2)
    rows = B_GROUP * blk
    per_half = CHUNK // HEAD_DIM
    r = lax.broadcasted_iota(jnp.int32, (rows, 2 * blk), 0) % blk
    c = lax.broadcasted_iota(jnp.int32, (rows, 2 * blk), 1)
    rel = r + blk - c
    in_window = (rel >= 0) & (rel < WINDOW)
    sink = sink_ref[0]
    vgroup = lax.broadcasted_iota(jnp.int32, (2 * blk, 4 * HEAD_DIM), 1) // HEAD_DIM
    ogroup = lax.broadcasted_iota(jnp.int32, (blk, 4 * HEAD_DIM), 1) // HEAD_DIM
    for n in range(tq // blk):
        q = q_ref[:, n * blk:(n + 1) * blk, :].reshape(rows, HEAD_DIM)
        if n == 0:
            k = jnp.concatenate([kp_ref[0], kc_ref[0, 0:blk, :]], axis=0)
            v = jnp.concatenate([vp_ref[0], vc_ref[0, 0:blk, :]], axis=0)
            mask = in_window & ((c >= blk) | (i > 0))
        else:
            k = kc_ref[0, (n - 1) * blk:(n + 1) * blk, :]
            v = vc_ref[0, (n - 1) * blk:(n + 1) * blk, :]
            mask = in_window
        s = lax.dot_general(q, k, (((1,), (1,)), ((), ())), preferred_element_type=jnp.float32)
        s = jnp.where(mask, s, NEG_INF)
        m = jnp.maximum(sink, jnp.max(s, axis=-1, keepdims=True))
        p = jnp.exp2(s - jnp.tile(m, (1, 2)))
        denom = jnp.exp2(sink - m) + jnp.sum(p, axis=-1, keepdims=True)
        rinv = 1.0 / denom
        pb = p.astype(jnp.bfloat16)
        vsel = [jnp.where(vgroup == j, v, jnp.zeros_like(v)) for j in range(per_half)]
        halves = []
        for hf in range(B_GROUP // per_half):
            acc = None
            scale = None
            for j in range(per_half):
                g = hf * per_half + j
                part = jnp.dot(pb[g * blk:(g + 1) * blk], vsel[j], preferred_element_type=jnp.float32)
                acc = part if acc is None else acc + part
                rg = jnp.tile(rinv[g * blk:(g + 1) * blk], (1, 2))
                scale = rg if scale is None else jnp.where(ogroup == j, rg, scale)
            halves.append(acc * scale)
        o = jnp.concatenate(halves, axis=1)
        gate = g_ref[n * blk:(n + 1) * blk, :].astype(jnp.float32)
        o_ref[n * blk:(n + 1) * blk, :] = (o * gate).astype(o_ref.dtype)


def _swa_attn(qb, kb, vb_rep, gb, sink_rep, batch, seq):
    tq = SWA_TQ
    nq = seq // tq
    r = tq // WINDOW
    gw = B_GROUP * HEAD_DIM
    cur = lambda b, kv, i: (kv, b * nq + i, 0)
    prev = lambda b, kv, i: (kv, jnp.maximum(b * nq * r + i * r - 1, 0), 0)
    omap = lambda b, kv, i: (b * nq + i, kv)
    return pl.pallas_call(
        _swa_kernel,
        grid=(batch, B_KV_HEADS, nq),
        in_specs=[
            pl.BlockSpec((B_GROUP, tq, HEAD_DIM), cur),
            pl.BlockSpec((1, WINDOW, HEAD_DIM), prev),
            pl.BlockSpec((1, tq, HEAD_DIM), cur),
            pl.BlockSpec((1, WINDOW, 4 * HEAD_DIM), prev),
            pl.BlockSpec((1, tq, 4 * HEAD_DIM), cur),
            pl.BlockSpec((1, B_GROUP * WINDOW, LANES), lambda b, kv, i: (kv, 0, 0)),
            pl.BlockSpec((tq, gw), omap),
        ],
        out_specs=pl.BlockSpec((tq, gw), omap),
        out_shape=jax.ShapeDtypeStruct((batch * seq, B_WIDTH), jnp.bfloat16),
        compiler_params=pltpu.CompilerParams(
            dimension_semantics=("arbitrary", "arbitrary", "arbitrary"), vmem_limit_bytes=V7X_VMEM_LIMIT),
        name="swa_attn",
    )(qb, kb, kb, vb_rep, vb_rep, sink_rep, gb)


def _out_proj_kernel(ya_ref, yb_ref, wa_ref, wb_ref, x_ref, o_ref):
    acc = jnp.dot(ya_ref[...], wa_ref[...], preferred_element_type=jnp.float32)
    acc = acc + jnp.dot(yb_ref[...], wb_ref[...], preferred_element_type=jnp.float32)
    o_ref[...] = x_ref[...] + acc


def _out_proj(ya, yb, wa, wb, x2):
    m = x2.shape[0]
    bm = OUT_BM
    row = lambda i: (i, 0)
    const = lambda i: (0, 0)
    return pl.pallas_call(
        _out_proj_kernel,
        grid=(m // bm,),
        in_specs=[
            pl.BlockSpec((bm, A_WIDTH), row),
            pl.BlockSpec((bm, B_WIDTH), row),
            pl.BlockSpec((A_WIDTH, D_MODEL), const, pipeline_mode=pl.Buffered(1)),
            pl.BlockSpec((B_WIDTH, D_MODEL), const, pipeline_mode=pl.Buffered(1)),
            pl.BlockSpec((bm, D_MODEL), row),
        ],
        out_specs=pl.BlockSpec((bm, D_MODEL), row),
        out_shape=jax.ShapeDtypeStruct((m, D_MODEL), jnp.float32),
        compiler_params=pltpu.CompilerParams(
            dimension_semantics=("arbitrary",), vmem_limit_bytes=V7X_VMEM_LIMIT),
        name="out_proj",
    )(ya, yb, wa, wb, x2)


def kernel(x, positions, norm_w, w_in, q_norm_a, k_norm_a, lambda_q1, lambda_k1, lambda_q2, lambda_k2,
           subln_w, q_norm_b, k_norm_b, sinks, w_out):
    batch, seq, _ = x.shape
    depth = norm_w.shape[0]
    m = batch * seq
    f32 = jnp.float32

    cos_t, sin_t = _rope_tables(positions, m)
    gid = jnp.arange(CHUNK) // HEAD_DIM
    gmat = jnp.where(gid[:, None] == gid[None, :], 1.0 / HEAD_DIM, 0.0).astype(jnp.bfloat16)
    qk_scale = HEAD_DIM ** -0.5 * LOG2E
    tile = lambda w: jnp.tile(w.astype(f32), CHUNK // HEAD_DIM).reshape(1, CHUNK)

    x2 = x.reshape(m, D_MODEL)
    for layer in range(depth):
        lam_init = 0.8 - 0.6 * math.exp(-0.3 * layer)
        w_bf16 = w_in[layer].astype(jnp.bfloat16)
        qa, ka, va_ext, ga, qb, kb, vb_rep, gb = _in_proj(
            x2, cos_t, sin_t, norm_w[layer].reshape(1, D_MODEL), w_bf16, gmat,
            tile(q_norm_a[layer]) * qk_scale, tile(k_norm_a[layer]),
            tile(q_norm_b[layer]) * qk_scale, tile(k_norm_b[layer]))
        vec = lambda p: p[layer].astype(f32).reshape(1, HEAD_DIM)
        attn_args = (qa, ka, va_ext, ga, vec(lambda_q1), vec(lambda_k1), vec(lambda_q2), vec(lambda_k2),
                     subln_w[layer].astype(f32).reshape(1, A_VDIM))
        attn = functools.partial(_diff_attn, batch=batch, seq=seq, lam_init=lam_init)
        score_bound = (HEAD_DIM * qk_scale * jnp.max(jnp.abs(q_norm_a[layer].astype(f32)))
                       * jnp.max(jnp.abs(k_norm_a[layer].astype(f32))))
        ya = lax.cond(score_bound <= RAW_EXP_SCORE_BOUND,
                      functools.partial(attn, online_max=False),
                      functools.partial(attn, online_max=True), *attn_args)
        sink_rep = jnp.broadcast_to(
            jnp.repeat(sinks[layer].astype(f32) * LOG2E, WINDOW).reshape(B_KV_HEADS, B_GROUP * WINDOW, 1),
            (B_KV_HEADS, B_GROUP * WINDOW, LANES))
        yb = _swa_attn(qb, kb, vb_rep, gb, sink_rep, batch, seq)
        w_o = w_out[layer].astype(jnp.bfloat16)
        x2 = _out_proj(ya, yb, w_o[:A_WIDTH], w_o[A_WIDTH:], x2)
    return x2.reshape(batch, seq, D_MODEL)
```

```python
import functools
import math

import jax
import jax.numpy as jnp
from jax import lax
from jax.experimental import pallas as pl
from jax.experimental.pallas import tpu as pltpu

D_MODEL = 2048
HEAD_DIM = 64
ROPE_THETA = 10000.0
EPS = 1e-6
NEG_INF = -1e30
LOG2E = math.log2(math.e)

A_HEADS = 8
A_VDIM = 128
A_WIDTH = 1024
B_HEADS = 16
B_KV_HEADS = 2
B_GROUP = 8
B_WIDTH = 1024
WINDOW = 128
PROJ_WIDTH = 6400

OFF_QA, OFF_KA, OFF_VA, OFF_GA, OFF_QB, OFF_KB, OFF_VB, OFF_GB = 0, 1024, 2048, 3072, 4096, 5120, 5248, 5376

V7X_VMEM_LIMIT = 56 * 1024 * 1024
LANES = 128
CHUNK = 256

ROPE_ROWS = 512
PROJ_BM = 512
ATT_TQ = 2048
ATT_QS = 256
SWA_TQ = 512
OUT_BM = 512

RAW_EXP_SCORE_BOUND = 60.0


def _rope_kernel(pos_ref, invf_ref, sign_ref, cos_ref, sin_ref):
    half = HEAD_DIM // 2
    groups = LANES // half
    ang = pos_ref[...] * invf_ref[...]
    group = lax.broadcasted_iota(jnp.int32, ang.shape, 1) // half
    for src, dst, sgn in ((jnp.cos(ang), cos_ref, None), (jnp.sin(ang), sin_ref, sign_ref[...])):
        rolled = [src] + [pltpu.roll(src, half * k, 1) for k in range(1, groups)]
        for t in range(groups):
            out = rolled[(0 - t) % groups]
            for g in range(1, groups):
                out = jnp.where(group == g, rolled[(g - t) % groups], out)
            dst[t] = out if sgn is None else out * sgn


def _rope_tables(positions, m):
    half = HEAD_DIM // 2
    f32 = jnp.float32
    groups = LANES // half
    rows = m // groups
    inv_freq = ROPE_THETA ** (-(jnp.arange(0, HEAD_DIM, 2, dtype=f32) / HEAD_DIM))
    pos_rep = jnp.repeat(positions.reshape(groups, rows).astype(f32).T, half, axis=1)
    invf = jnp.tile(inv_freq, groups).reshape(1, LANES)
    sign = jnp.tile(jnp.concatenate([-jnp.ones((half,), f32), jnp.ones((half,), f32)]), LANES // HEAD_DIM)
    row = lambda i: (i, 0)
    const = lambda i: (0, 0)
    slab = lambda i: (0, i, 0)
    cos_t, sin_t = pl.pallas_call(
        _rope_kernel,
        grid=(rows // ROPE_ROWS,),
        in_specs=[pl.BlockSpec((ROPE_ROWS, LANES), row), pl.BlockSpec((1, LANES), const),
                  pl.BlockSpec((1, LANES), const)],
        out_specs=(pl.BlockSpec((groups, ROPE_ROWS, LANES), slab), pl.BlockSpec((groups, ROPE_ROWS, LANES), slab)),
        out_shape=(jax.ShapeDtypeStruct((groups, rows, LANES), f32),) * 2,
        name="rope_table",
    )(pos_rep, invf, sign.reshape(1, LANES))
    return cos_t.reshape(m, LANES), sin_t.reshape(m, LANES)


def _in_proj_kernel(x_ref, cos_ref, sin_ref, nw_ref, w_ref, gmat_ref,
                    qna_ref, kna_ref, qnb_ref, knb_ref,
                    qa_ref, ka_ref, va_ref, ga_ref, qb_ref, kb_ref, vb_ref, gb_ref):
    bm = x_ref.shape[0]
    x = x_ref[...]
    ms = jnp.mean(x * x, axis=-1, keepdims=True)
    h = ((x * lax.rsqrt(ms + EPS)) * nw_ref[...]).astype(jnp.bfloat16)

    cos = jnp.tile(cos_ref[...], (1, CHUNK // LANES))
    sin_signed = jnp.tile(sin_ref[...], (1, CHUNK // LANES))
    lane = lax.broadcasted_iota(jnp.int32, (bm, CHUNK), 1)
    first_half = (lane % HEAD_DIM) < (HEAD_DIM // 2)
    gmat = gmat_ref[...]
    ones_blk = jnp.ones((bm, LANES), jnp.bfloat16)

    def proj(c0, width=CHUNK):
        return jnp.dot(h, w_ref[:, c0:c0 + width], preferred_element_type=jnp.float32)

    def norm_rope(t, w):
        msq = jnp.dot((t * t).astype(jnp.bfloat16), gmat, preferred_element_type=jnp.float32)
        y = t * lax.rsqrt(msq + EPS) * w
        rot = jnp.where(first_half, pltpu.roll(y, CHUNK - HEAD_DIM // 2, 1), pltpu.roll(y, HEAD_DIM // 2, 1))
        return y * cos + rot * sin_signed

    def silu(t):
        return t / (1.0 + jnp.exp(-t))

    def put_qa(t, o):
        qa_ref[:, o:o + CHUNK] = norm_rope(t, qna_ref[...]).astype(jnp.bfloat16)

    def put_ka(t, o):
        ka_ref[:, o:o + CHUNK] = norm_rope(t, kna_ref[...]).astype(jnp.bfloat16)

    def put_va(t, o):
        va = t.astype(jnp.bfloat16)
        for j in range(CHUNK // A_VDIM):
            base = (o // A_VDIM + j) * 2 * A_VDIM
            va_ref[:, base:base + A_VDIM] = va[:, j * A_VDIM:(j + 1) * A_VDIM]
            va_ref[:, base + A_VDIM:base + 2 * A_VDIM] = ones_blk

    def put_ga(t, o):
        ga_ref[:, o:o + CHUNK] = silu(t).astype(jnp.bfloat16)

    def put_gb(t, o):
        gb_ref[:, o:o + CHUNK] = silu(t).astype(jnp.bfloat16)

    def put_qb(t, o):
        qb = norm_rope(t, qnb_ref[...]).astype(jnp.bfloat16)
        for j in range(CHUNK // HEAD_DIM):
            qb_ref[o // HEAD_DIM + j] = qb[:, j * HEAD_DIM:(j + 1) * HEAD_DIM]

    def put_kvb(kv, o):
        kbn = norm_rope(kv, knb_ref[...]).astype(jnp.bfloat16)
        for j in range(B_KV_HEADS):
            kb_ref[j] = kbn[:, j * HEAD_DIM:(j + 1) * HEAD_DIM]
        v_lo = lane < CHUNK - HEAD_DIM
        v0 = jnp.where(v_lo, kv, pltpu.roll(kv, HEAD_DIM, 1))[:, LANES:]
        v1 = jnp.where(v_lo, pltpu.roll(kv, CHUNK - HEAD_DIM, 1), kv)[:, LANES:]
        vb_ref[0] = jnp.concatenate([v0, v0], axis=1).astype(jnp.bfloat16)
        vb_ref[1] = jnp.concatenate([v1, v1], axis=1).astype(jnp.bfloat16)

    tasks = [(OFF_KB, put_kvb, 0)]
    for c in range(A_WIDTH // CHUNK):
        o = c * CHUNK
        tasks += [(OFF_QA + o, put_qa, o), (OFF_KA + o, put_ka, o), (OFF_QB + o, put_qb, o),
                  (OFF_VA + o, put_va, o), (OFF_GA + o, put_ga, o), (OFF_GB + o, put_gb, o)]
    pending = None
    for col, put, o in tasks:
        t = proj(col)
        if pending is not None:
            pending[0](pending[1], pending[2])
        pending = (put, t, o)
    pending[0](pending[1], pending[2])


def _in_proj(x2, cos_t, sin_t, norm_w, w_bf16, gmat, qna, kna, qnb, knb):
    m = x2.shape[0]
    bm = PROJ_BM
    row = lambda i: (i, 0)
    const = lambda i: (0, 0)
    hm = lambda i: (0, i, 0)
    bf = jnp.bfloat16
    out_shape = (
        jax.ShapeDtypeStruct((m, A_WIDTH), bf),
        jax.ShapeDtypeStruct((m, A_WIDTH), bf),
        jax.ShapeDtypeStruct((m, 2 * A_WIDTH), bf),
        jax.ShapeDtypeStruct((m, A_WIDTH), bf),
        jax.ShapeDtypeStruct((B_HEADS, m, HEAD_DIM), bf),
        jax.ShapeDtypeStruct((B_KV_HEADS, m, HEAD_DIM), bf),
        jax.ShapeDtypeStruct((B_KV_HEADS, m, 4 * HEAD_DIM), bf),
        jax.ShapeDtypeStruct((m, B_WIDTH), bf),
    )
    out_specs = (
        pl.BlockSpec((bm, A_WIDTH), row), pl.BlockSpec((bm, A_WIDTH), row),
        pl.BlockSpec((bm, 2 * A_WIDTH), row), pl.BlockSpec((bm, A_WIDTH), row),
        pl.BlockSpec((B_HEADS, bm, HEAD_DIM), hm),
        pl.BlockSpec((B_KV_HEADS, bm, HEAD_DIM), hm),
        pl.BlockSpec((B_KV_HEADS, bm, 4 * HEAD_DIM), hm),
        pl.BlockSpec((bm, B_WIDTH), row),
    )
    in_specs = [
        pl.BlockSpec((bm, D_MODEL), row),
        pl.BlockSpec((bm, LANES), row),
        pl.BlockSpec((bm, LANES), row),
        pl.BlockSpec((1, D_MODEL), const),
        pl.BlockSpec((D_MODEL, PROJ_WIDTH), const, pipeline_mode=pl.Buffered(1)),
        pl.BlockSpec((CHUNK, CHUNK), const),
        pl.BlockSpec((1, CHUNK), const), pl.BlockSpec((1, CHUNK), const),
        pl.BlockSpec((1, CHUNK), const), pl.BlockSpec((1, CHUNK), const),
    ]
    return pl.pallas_call(
        _in_proj_kernel,
        grid=(m // bm,),
        in_specs=in_specs,
        out_specs=out_specs,
        out_shape=out_shape,
        compiler_params=pltpu.CompilerParams(
            dimension_semantics=("arbitrary",), vmem_limit_bytes=V7X_VMEM_LIMIT),
        name="in_proj",
    )(x2, cos_t, sin_t, norm_w, w_bf16, gmat, qna, kna, qnb, knb)


def _diff_attn_kernel(q_ref, k_ref, v_ref, g_ref, lq1_ref, lk1_ref, lq2_ref, lk2_ref, sw_ref,
                      o_ref, acc_ref, m_ref, *, lam_init, online_max):
    tq = q_ref.shape[0]
    nsub = tq // ATT_QS
    qi = pl.program_id(2)

    q = q_ref[...]
    lane = lax.broadcasted_iota(jnp.int32, q.shape, 1)
    zero = jnp.zeros_like(q)
    qs = (jnp.where(lane < HEAD_DIM, q, zero), jnp.where(lane >= HEAD_DIM, q, zero))

    acc_ref[...] = jnp.zeros(acc_ref.shape, jnp.float32)
    if online_max:
        m_ref[...] = jnp.full(m_ref.shape, NEG_INF, jnp.float32)

    def scores(c, r, start, nk, masked):
        rows = slice(r * ATT_QS, (r + 1) * ATT_QS)
        k = k_ref[pl.ds(start, nk), :]
        s = lax.dot_general(qs[c][rows], k, (((1,), (1,)), ((), ())), preferred_element_type=jnp.float32)
        if masked:
            row = lax.broadcasted_iota(jnp.int32, s.shape, 0)
            col = lax.broadcasted_iota(jnp.int32, s.shape, 1)
            s = jnp.where(col <= row + (nk - ATT_QS), s, NEG_INF)
        return s

    def accumulate(s, c, r, start, nk):
        rows = slice(r * ATT_QS, (r + 1) * ATT_QS)
        v = v_ref[pl.ds(start, nk), :]
        if online_max:
            m_old = m_ref[c, rows, :]
            m_new = jnp.maximum(m_old, jnp.max(s, axis=-1, keepdims=True))
            alpha = jnp.exp2(m_old - m_new)
            m_ref[c, rows, :] = m_new
            p = jnp.exp2(s - jnp.tile(m_new, (1, nk // LANES)))
            pv = jnp.dot(p.astype(jnp.bfloat16), v, preferred_element_type=jnp.float32)
            acc_ref[c, rows, :] = acc_ref[c, rows, :] * jnp.tile(alpha, (1, 2)) + pv
        else:
            p = jnp.exp2(s)
            pv = jnp.dot(p.astype(jnp.bfloat16), v, preferred_element_type=jnp.float32)
            acc_ref[c, rows, :] += pv

    def sweep(start, nk_of, masked):
        units = [(c, r) for r in range(nsub) for c in range(2)]
        s_next = scores(*units[0], start, nk_of(units[0][1]), masked)
        for idx, (c, r) in enumerate(units):
            s = s_next
            if idx + 1 < len(units):
                cn, rn = units[idx + 1]
                s_next = scores(cn, rn, start, nk_of(rn), masked)
            accumulate(s, c, r, start, nk_of(r))

    def body(j, carry):
        sweep(pl.multiple_of(j * tq, tq), lambda r: tq, False)
        return carry

    lax.fori_loop(0, qi, body, 0)
    sweep(pl.multiple_of(qi * tq, tq), lambda r: (r + 1) * ATT_QS, True)

    lam = (jnp.exp(jnp.sum(lq1_ref[...] * lk1_ref[...], axis=-1, keepdims=True))
           - jnp.exp(jnp.sum(lq2_ref[...] * lk2_ref[...], axis=-1, keepdims=True)) + lam_init)
    a1 = acc_ref[0]
    a2 = acc_ref[1]
    o = a1[:, :A_VDIM] / a1[:, A_VDIM:] - lam * (a2[:, :A_VDIM] / a2[:, A_VDIM:])
    o = o * lax.rsqrt(jnp.mean(o * o, axis=-1, keepdims=True) + EPS) * sw_ref[...] * (1.0 - lam_init)
    o_ref[...] = (o * g_ref[...].astype(jnp.float32)).astype(o_ref.dtype)


def _diff_attn(qa, ka, va_ext, ga, lq1, lk1, lq2, lk2, subln_w, *, batch, seq, lam_init, online_max):
    tq = ATT_TQ
    nq = seq // tq
    qmap = lambda b, h, i: (b * nq + i, h)
    kvmap = lambda b, h, i: (b, h)
    const = lambda b, h, i: (0, 0)
    vec = pl.BlockSpec((1, HEAD_DIM), const)
    return pl.pallas_call(
        functools.partial(_diff_attn_kernel, lam_init=lam_init, online_max=online_max),
        grid=(batch, A_HEADS, nq),
        in_specs=[
            pl.BlockSpec((tq, A_VDIM), qmap),
            pl.BlockSpec((seq, A_VDIM), kvmap),
            pl.BlockSpec((seq, 2 * A_VDIM), kvmap),
            pl.BlockSpec((tq, A_VDIM), qmap),
            vec, vec, vec, vec,
            pl.BlockSpec((1, A_VDIM), const),
        ],
        out_specs=pl.BlockSpec((tq, A_VDIM), qmap),
        out_shape=jax.ShapeDtypeStruct((batch * seq, A_WIDTH), jnp.bfloat16),
        scratch_shapes=[
            pltpu.VMEM((2, tq, 2 * A_VDIM), jnp.float32),
            pltpu.VMEM((2, tq, LANES), jnp.float32),
        ],
        compiler_params=pltpu.CompilerParams(
            dimension_semantics=("arbitrary", "arbitrary", "arbitrary"), vmem_limit_bytes=V7X_VMEM_LIMIT),
        name="diff_attn_online" if online_max else "diff_attn",
    )(qa, ka, va_ext, ga, lq1, lk1, lq2, lk2, subln_w)


def _swa_kernel(q_ref, kp_ref, kc_ref, vp_ref, vc_ref, sink_ref, g_ref, o_ref):
    tq = q_ref.shape[1]
    blk = WINDOW
    i = pl.program_id(2)
    rows = B_GROUP * blk
    per_half = CHUNK // HEAD_DIM
    r = lax.broadcasted_iota(jnp.int32, (blk, 2 * blk), 0)
    c = lax.broadcasted_iota(jnp.int32, (blk, 2 * blk), 1)
    rel = r + blk - c
    in_window = (rel >= 0) & (rel < WINDOW)
    first_block_mask = in_window & ((c >= blk) | (i > 0))
    vgroup = lax.broadcasted_iota(jnp.int32, (2 * blk, CHUNK), 1) // HEAD_DIM
    ogroup = lax.broadcasted_iota(jnp.int32, (blk, CHUNK), 1) // HEAD_DIM
    for n in range(tq // blk):
        q = q_ref[:, n * blk:(n + 1) * blk, :].reshape(rows, HEAD_DIM)
        if n == 0:
            k = jnp.concatenate([kp_ref[0], kc_ref[0, 0:blk, :]], axis=0)
            v = jnp.concatenate([vp_ref[0], vc_ref[0, 0:blk, :]], axis=0)
            mask = first_block_mask
        else:
            k = kc_ref[0, (n - 1) * blk:(n + 1) * blk, :]
            v = vc_ref[0, (n - 1) * blk:(n + 1) * blk, :]
            mask = in_window
        s_all = lax.dot_general(q, k, (((1,), (1,)), ((), ())), preferred_element_type=jnp.float32)
        vsel = [jnp.where(vgroup == j, v, jnp.zeros_like(v)) for j in range(per_half)]
        halves = []
        for hf in range(B_GROUP // per_half):
            acc = None
            scale = None
            for j in range(per_half):
                g = hf * per_half + j
                sink = sink_ref[0, g * blk:(g + 1) * blk, :]
                s = jnp.where(mask, s_all[g * blk:(g + 1) * blk], NEG_INF)
                m = jnp.maximum(sink, jnp.max(s, axis=-1, keepdims=True))
                p = jnp.exp2(s - jnp.tile(m, (1, 2)))
                denom = jnp.exp2(sink - m) + jnp.sum(p, axis=-1, keepdims=True)
                part = jnp.dot(p.astype(jnp.bfloat16), vsel[j], preferred_element_type=jnp.float32)
                acc = part if acc is None else acc + part
                rg = jnp.tile(1.0 / denom, (1, 2))
                scale = rg if scale is None else jnp.where(ogroup == j, rg, scale)
            halves.append(acc * scale)
        o = jnp.concatenate(halves, axis=1)
        gate = g_ref[n * blk:(n + 1) * blk, :].astype(jnp.float32)
        o_ref[n * blk:(n + 1) * blk, :] = (o * gate).astype(o_ref.dtype)


def _swa_attn(qb, kb, vb_rep, gb, sink_rep, batch, seq):
    tq = SWA_TQ
    nq = seq // tq
    r = tq // WINDOW
    gw = B_GROUP * HEAD_DIM
    cur = lambda b, kv, i: (kv, b * nq + i, 0)
    prev = lambda b, kv, i: (kv, jnp.maximum(b * nq * r + i * r - 1, 0), 0)
    omap = lambda b, kv, i: (b * nq + i, kv)
    return pl.pallas_call(
        _swa_kernel,
        grid=(batch, B_KV_HEADS, nq),
        in_specs=[
            pl.BlockSpec((B_GROUP, tq, HEAD_DIM), cur),
            pl.BlockSpec((1, WINDOW, HEAD_DIM), prev),
            pl.BlockSpec((1, tq, HEAD_DIM), cur),
            pl.BlockSpec((1, WINDOW, 4 * HEAD_DIM), prev),
            pl.BlockSpec((1, tq, 4 * HEAD_DIM), cur),
            pl.BlockSpec((1, B_GROUP * WINDOW, LANES), lambda b, kv, i: (kv, 0, 0)),
            pl.BlockSpec((tq, gw), omap),
        ],
        out_specs=pl.BlockSpec((tq, gw), omap),
        out_shape=jax.ShapeDtypeStruct((batch * seq, B_WIDTH), jnp.bfloat16),
        compiler_params=pltpu.CompilerParams(
            dimension_semantics=("arbitrary", "arbitrary", "arbitrary"), vmem_limit_bytes=V7X_VMEM_LIMIT),
        name="swa_attn",
    )(qb, kb, kb, vb_rep, vb_rep, sink_rep, gb)


def _out_proj_kernel(ya_ref, yb_ref, wa_ref, wb_ref, x_ref, o_ref):
    acc = jnp.dot(ya_ref[...], wa_ref[...], preferred_element_type=jnp.float32)
    acc = acc + jnp.dot(yb_ref[...], wb_ref[...], preferred_element_type=jnp.float32)
    o_ref[...] = x_ref[...] + acc


def _out_proj(ya, yb, wa, wb, x2):
    m = x2.shape[0]
    bm = OUT_BM
    row = lambda i: (i, 0)
    const = lambda i: (0, 0)
    return pl.pallas_call(
        _out_proj_kernel,
        grid=(m // bm,),
        in_specs=[
            pl.BlockSpec((bm, A_WIDTH), row),
            pl.BlockSpec((bm, B_WIDTH), row),
            pl.BlockSpec((A_WIDTH, D_MODEL), const, pipeline_mode=pl.Buffered(1)),
            pl.BlockSpec((B_WIDTH, D_MODEL), const, pipeline_mode=pl.Buffered(1)),
            pl.BlockSpec((bm, D_MODEL), row),
        ],
        out_specs=pl.BlockSpec((bm, D_MODEL), row),
        out_shape=jax.ShapeDtypeStruct((m, D_MODEL), jnp.float32),
        compiler_params=pltpu.CompilerParams(
            dimension_semantics=("arbitrary",), vmem_limit_bytes=V7X_VMEM_LIMIT),
        name="out_proj",
    )(ya, yb, wa, wb, x2)


def kernel(x, positions, norm_w, w_in, q_norm_a, k_norm_a, lambda_q1, lambda_k1, lambda_q2, lambda_k2,
           subln_w, q_norm_b, k_norm_b, sinks, w_out):
    batch, seq, _ = x.shape
    depth = norm_w.shape[0]
    m = batch * seq
    f32 = jnp.float32

    cos_t, sin_t = _rope_tables(positions, m)
    gid = jnp.arange(CHUNK) // HEAD_DIM
    gmat = jnp.where(gid[:, None] == gid[None, :], 1.0 / HEAD_DIM, 0.0).astype(jnp.bfloat16)
    qk_scale = HEAD_DIM ** -0.5 * LOG2E
    tile = lambda w: jnp.tile(w.astype(f32), CHUNK // HEAD_DIM).reshape(1, CHUNK)

    x2 = x.reshape(m, D_MODEL)
    for layer in range(depth):
        lam_init = 0.8 - 0.6 * math.exp(-0.3 * layer)
        w_bf16 = w_in[layer].astype(jnp.bfloat16)
        qa, ka, va_ext, ga, qb, kb, vb_rep, gb = _in_proj(
            x2, cos_t, sin_t, norm_w[layer].reshape(1, D_MODEL), w_bf16, gmat,
            tile(q_norm_a[layer]) * qk_scale, tile(k_norm_a[layer]),
            tile(q_norm_b[layer]) * qk_scale, tile(k_norm_b[layer]))
        vec = lambda p: p[layer].astype(f32).reshape(1, HEAD_DIM)
        attn_args = (qa, ka, va_ext, ga, vec(lambda_q1), vec(lambda_k1), vec(lambda_q2), vec(lambda_k2),
                     subln_w[layer].astype(f32).reshape(1, A_VDIM))
        attn = functools.partial(_diff_attn, batch=batch, seq=seq, lam_init=lam_init)
        score_bound = (HEAD_DIM * qk_scale * jnp.max(jnp.abs(q_norm_a[layer].astype(f32)))
                       * jnp.max(jnp.abs(k_norm_a[layer].astype(f32))))
        ya = lax.cond(score_bound <= RAW_EXP_SCORE_BOUND,
                      functools.partial(attn, online_max=False),
                      functools.partial(attn, online_max=True), *attn_args)
        sink_rep = jnp.broadcast_to(
            jnp.repeat(sinks[layer].astype(f32) * LOG2E, WINDOW).reshape(B_KV_HEADS, B_GROUP * WINDOW, 1),
            (B_KV_HEADS, B_GROUP * WINDOW, LANES))
        yb = _swa_attn(qb, kb, vb_rep, gb, sink_rep, batch, seq)
        w_o = w_out[layer].astype(jnp.bfloat16)
        x2 = _out_proj(ya, yb, w_o[:A_WIDTH], w_o[A_WIDTH:], x2)
    return x2.reshape(batch, seq, D_MODEL)
```

```python
import functools
import math

import jax
import jax.numpy as jnp
from jax import lax
from jax.experimental import pallas as pl
from jax.experimental.pallas import tpu as pltpu

D_MODEL = 2048
HEAD_DIM = 64
ROPE_THETA = 10000.0
EPS = 1e-6
NEG_INF = -1e30
LOG2E = math.log2(math.e)

A_HEADS = 8
A_VDIM = 128
A_WIDTH = 1024
B_HEADS = 16
B_KV_HEADS = 2
B_GROUP = 8
B_WIDTH = 1024
WINDOW = 128
PROJ_WIDTH = 6400

OFF_QA, OFF_KA, OFF_VA, OFF_GA, OFF_QB, OFF_KB, OFF_VB, OFF_GB = 0, 1024, 2048, 3072, 4096, 5120, 5248, 5376

V7X_VMEM_LIMIT = 56 * 1024 * 1024
LANES = 128
CHUNK = 256

ROPE_ROWS = 512
PROJ_BM = 512
ATT_TQ = 2048
ATT_QS = 256
SWA_TQ = 1024
OUT_BM = 512

RAW_EXP_SCORE_BOUND = 60.0


def _rope_kernel(pos_ref, invf_ref, sign_ref, cos_ref, sin_ref):
    half = HEAD_DIM // 2
    groups = LANES // half
    ang = pos_ref[...] * invf_ref[...]
    group = lax.broadcasted_iota(jnp.int32, ang.shape, 1) // half
    for src, dst, sgn in ((jnp.cos(ang), cos_ref, None), (jnp.sin(ang), sin_ref, sign_ref[...])):
        rolled = [src] + [pltpu.roll(src, half * k, 1) for k in range(1, groups)]
        for t in range(groups):
            out = rolled[(0 - t) % groups]
            for g in range(1, groups):
                out = jnp.where(group == g, rolled[(g - t) % groups], out)
            dst[t] = out if sgn is None else out * sgn


def _rope_tables(positions, m):
    half = HEAD_DIM // 2
    f32 = jnp.float32
    groups = LANES // half
    rows = m // groups
    inv_freq = ROPE_THETA ** (-(jnp.arange(0, HEAD_DIM, 2, dtype=f32) / HEAD_DIM))
    pos_rep = jnp.repeat(positions.reshape(groups, rows).astype(f32).T, half, axis=1)
    invf = jnp.tile(inv_freq, groups).reshape(1, LANES)
    sign = jnp.tile(jnp.concatenate([-jnp.ones((half,), f32), jnp.ones((half,), f32)]), LANES // HEAD_DIM)
    row = lambda i: (i, 0)
    const = lambda i: (0, 0)
    slab = lambda i: (0, i, 0)
    cos_t, sin_t = pl.pallas_call(
        _rope_kernel,
        grid=(rows // ROPE_ROWS,),
        in_specs=[pl.BlockSpec((ROPE_ROWS, LANES), row), pl.BlockSpec((1, LANES), const),
                  pl.BlockSpec((1, LANES), const)],
        out_specs=(pl.BlockSpec((groups, ROPE_ROWS, LANES), slab), pl.BlockSpec((groups, ROPE_ROWS, LANES), slab)),
        out_shape=(jax.ShapeDtypeStruct((groups, rows, LANES), f32),) * 2,
        name="rope_table",
    )(pos_rep, invf, sign.reshape(1, LANES))
    return cos_t.reshape(m, LANES), sin_t.reshape(m, LANES)


def _in_proj_kernel(x_ref, cos_ref, sin_ref, nw_ref, w_ref, gmat_ref,
                    qna_ref, kna_ref, qnb_ref, knb_ref,
                    qa_ref, ka_ref, va_ref, ga_ref, qb_ref, kb_ref, vb_ref, gb_ref):
    bm = x_ref.shape[0]
    x = x_ref[...]
    ms = jnp.mean(x * x, axis=-1, keepdims=True)
    h = ((x * lax.rsqrt(ms + EPS)) * nw_ref[...]).astype(jnp.bfloat16)

    cos = jnp.tile(cos_ref[...], (1, CHUNK // LANES))
    sin_signed = jnp.tile(sin_ref[...], (1, CHUNK // LANES))
    lane = lax.broadcasted_iota(jnp.int32, (bm, CHUNK), 1)
    first_half = (lane % HEAD_DIM) < (HEAD_DIM // 2)
    gmat = gmat_ref[...]
    ones_blk = jnp.ones((bm, LANES), jnp.bfloat16)

    def proj(c0, width=CHUNK):
        return jnp.dot(h, w_ref[:, c0:c0 + width], preferred_element_type=jnp.float32)

    def norm_rope(t, w):
        msq = jnp.dot((t * t).astype(jnp.bfloat16), gmat, preferred_element_type=jnp.float32)
        y = t * lax.rsqrt(msq + EPS) * w
        rot = jnp.where(first_half, pltpu.roll(y, CHUNK - HEAD_DIM // 2, 1), pltpu.roll(y, HEAD_DIM // 2, 1))
        return y * cos + rot * sin_signed

    def silu(t):
        return t / (1.0 + jnp.exp(-t))

    def put_qa(t, o):
        qa_ref[:, o:o + CHUNK] = norm_rope(t, qna_ref[...]).astype(jnp.bfloat16)

    def put_ka(t, o):
        ka_ref[:, o:o + CHUNK] = norm_rope(t, kna_ref[...]).astype(jnp.bfloat16)

    def put_va(t, o):
        va = t.astype(jnp.bfloat16)
        for j in range(CHUNK // A_VDIM):
            base = (o // A_VDIM + j) * 2 * A_VDIM
            va_ref[:, base:base + A_VDIM] = va[:, j * A_VDIM:(j + 1) * A_VDIM]
            va_ref[:, base + A_VDIM:base + 2 * A_VDIM] = ones_blk

    def put_ga(t, o):
        ga_ref[:, o:o + CHUNK] = silu(t).astype(jnp.bfloat16)

    def put_gb(t, o):
        gb_ref[:, o:o + CHUNK] = silu(t).astype(jnp.bfloat16)

    def put_qb(t, o):
        qb = norm_rope(t, qnb_ref[...]).astype(jnp.bfloat16)
        for j in range(CHUNK // HEAD_DIM):
            qb_ref[o // HEAD_DIM + j] = qb[:, j * HEAD_DIM:(j + 1) * HEAD_DIM]

    def put_kvb(kv, o):
        kbn = norm_rope(kv, knb_ref[...]).astype(jnp.bfloat16)
        for j in range(B_KV_HEADS):
            kb_ref[j] = kbn[:, j * HEAD_DIM:(j + 1) * HEAD_DIM]
        v_lo = lane < CHUNK - HEAD_DIM
        v0 = jnp.where(v_lo, kv, pltpu.roll(kv, HEAD_DIM, 1))[:, LANES:]
        v1 = jnp.where(v_lo, pltpu.roll(kv, CHUNK - HEAD_DIM, 1), kv)[:, LANES:]
        vb_ref[0] = jnp.concatenate([v0, v0], axis=1).astype(jnp.bfloat16)
        vb_ref[1] = jnp.concatenate([v1, v1], axis=1).astype(jnp.bfloat16)

    tasks = [(OFF_KB, put_kvb, 0)]
    for c in range(A_WIDTH // CHUNK):
        o = c * CHUNK
        tasks += [(OFF_QA + o, put_qa, o), (OFF_KA + o, put_ka, o), (OFF_QB + o, put_qb, o),
                  (OFF_VA + o, put_va, o), (OFF_GA + o, put_ga, o), (OFF_GB + o, put_gb, o)]
    pending = None
    for col, put, o in tasks:
        t = proj(col)
        if pending is not None:
            pending[0](pending[1], pending[2])
        pending = (put, t, o)
    pending[0](pending[1], pending[2])


def _in_proj(x2, cos_t, sin_t, norm_w, w_bf16, gmat, qna, kna, qnb, knb):
    m = x2.shape[0]
    bm = PROJ_BM
    row = lambda i: (i, 0)
    const = lambda i: (0, 0)
    hm = lambda i: (0, i, 0)
    bf = jnp.bfloat16
    out_shape = (
        jax.ShapeDtypeStruct((m, A_WIDTH), bf),
        jax.ShapeDtypeStruct((m, A_WIDTH), bf),
        jax.ShapeDtypeStruct((m, 2 * A_WIDTH), bf),
        jax.ShapeDtypeStruct((m, A_WIDTH), bf),
        jax.ShapeDtypeStruct((B_HEADS, m, HEAD_DIM), bf),
        jax.ShapeDtypeStruct((B_KV_HEADS, m, HEAD_DIM), bf),
        jax.ShapeDtypeStruct((B_KV_HEADS, m, 4 * HEAD_DIM), bf),
        jax.ShapeDtypeStruct((m, B_WIDTH), bf),
    )
    out_specs = (
        pl.BlockSpec((bm, A_WIDTH), row), pl.BlockSpec((bm, A_WIDTH), row),
        pl.BlockSpec((bm, 2 * A_WIDTH), row), pl.BlockSpec((bm, A_WIDTH), row),
        pl.BlockSpec((B_HEADS, bm, HEAD_DIM), hm),
        pl.BlockSpec((B_KV_HEADS, bm, HEAD_DIM), hm),
        pl.BlockSpec((B_KV_HEADS, bm, 4 * HEAD_DIM), hm),
        pl.BlockSpec((bm, B_WIDTH), row),
    )
    in_specs = [
        pl.BlockSpec((bm, D_MODEL), row),
        pl.BlockSpec((bm, LANES), row),
        pl.BlockSpec((bm, LANES), row),
        pl.BlockSpec((1, D_MODEL), const),
        pl.BlockSpec((D_MODEL, PROJ_WIDTH), const, pipeline_mode=pl.Buffered(1)),
        pl.BlockSpec((CHUNK, CHUNK), const),
        pl.BlockSpec((1, CHUNK), const), pl.BlockSpec((1, CHUNK), const),
        pl.BlockSpec((1, CHUNK), const), pl.BlockSpec((1, CHUNK), const),
    ]
    return pl.pallas_call(
        _in_proj_kernel,
        grid=(m // bm,),
        in_specs=in_specs,
        out_specs=out_specs,
        out_shape=out_shape,
        compiler_params=pltpu.CompilerParams(
            dimension_semantics=("arbitrary",), vmem_limit_bytes=V7X_VMEM_LIMIT),
        name="in_proj",
    )(x2, cos_t, sin_t, norm_w, w_bf16, gmat, qna, kna, qnb, knb)


def _diff_attn_kernel(q_ref, k_ref, v_ref, g_ref, lq1_ref, lk1_ref, lq2_ref, lk2_ref, sw_ref,
                      o_ref, acc_ref, m_ref, *, lam_init, online_max):
    tq = q_ref.shape[0]
    nsub = tq // ATT_QS
    qi = pl.program_id(2)

    q = q_ref[...]
    lane = lax.broadcasted_iota(jnp.int32, q.shape, 1)
    zero = jnp.zeros_like(q)
    qs = (jnp.where(lane < HEAD_DIM, q, zero), jnp.where(lane >= HEAD_DIM, q, zero))

    acc_ref[...] = jnp.zeros(acc_ref.shape, jnp.float32)
    if online_max:
        m_ref[...] = jnp.full(m_ref.shape, NEG_INF, jnp.float32)

    def scores(c, r, start, nk, masked):
        rows = slice(r * ATT_QS, (r + 1) * ATT_QS)
        k = k_ref[pl.ds(start, nk), :]
        s = lax.dot_general(qs[c][rows], k, (((1,), (1,)), ((), ())), preferred_element_type=jnp.float32)
        if masked:
            row = lax.broadcasted_iota(jnp.int32, s.shape, 0)
            col = lax.broadcasted_iota(jnp.int32, s.shape, 1)
            s = jnp.where(col <= row + (nk - ATT_QS), s, NEG_INF)
        return s

    def accumulate(s, c, r, start, nk):
        rows = slice(r * ATT_QS, (r + 1) * ATT_QS)
        v = v_ref[pl.ds(start, nk), :]
        if online_max:
            m_old = m_ref[c, rows, :]
            m_new = jnp.maximum(m_old, jnp.max(s, axis=-1, keepdims=True))
            alpha = jnp.exp2(m_old - m_new)
            m_ref[c, rows, :] = m_new
            p = jnp.exp2(s - jnp.tile(m_new, (1, nk // LANES)))
            pv = jnp.dot(p.astype(jnp.bfloat16), v, preferred_element_type=jnp.float32)
            acc_ref[c, rows, :] = acc_ref[c, rows, :] * jnp.tile(alpha, (1, 2)) + pv
        else:
            p = jnp.exp2(s)
            pv = jnp.dot(p.astype(jnp.bfloat16), v, preferred_element_type=jnp.float32)
            acc_ref[c, rows, :] += pv

    def sweep(start, nk_of, masked):
        units = [(c, r) for r in range(nsub) for c in range(2)]
        s_next = scores(*units[0], start, nk_of(units[0][1]), masked)
        for idx, (c, r) in enumerate(units):
            s = s_next
            if idx + 1 < len(units):
                cn, rn = units[idx + 1]
                s_next = scores(cn, rn, start, nk_of(rn), masked)
            accumulate(s, c, r, start, nk_of(r))

    def body(j, carry):
        sweep(pl.multiple_of(j * tq, tq), lambda r: tq, False)
        return carry

    lax.fori_loop(0, qi, body, 0)
    sweep(pl.multiple_of(qi * tq, tq), lambda r: (r + 1) * ATT_QS, True)

    lam = (jnp.exp(jnp.sum(lq1_ref[...] * lk1_ref[...], axis=-1, keepdims=True))
           - jnp.exp(jnp.sum(lq2_ref[...] * lk2_ref[...], axis=-1, keepdims=True)) + lam_init)
    a1 = acc_ref[0]
    a2 = acc_ref[1]
    o = a1[:, :A_VDIM] / a1[:, A_VDIM:] - lam * (a2[:, :A_VDIM] / a2[:, A_VDIM:])
    o = o * lax.rsqrt(jnp.mean(o * o, axis=-1, keepdims=True) + EPS) * sw_ref[...] * (1.0 - lam_init)
    o_ref[...] = (o * g_ref[...].astype(jnp.float32)).astype(o_ref.dtype)


def _diff_attn(qa, ka, va_ext, ga, lq1, lk1, lq2, lk2, subln_w, *, batch, seq, lam_init, online_max):
    tq = ATT_TQ
    nq = seq // tq
    qmap = lambda b, h, i: (b * nq + i, h)
    kvmap = lambda b, h, i: (b, h)
    const = lambda b, h, i: (0, 0)
    vec = pl.BlockSpec((1, HEAD_DIM), const)
    return pl.pallas_call(
        functools.partial(_diff_attn_kernel, lam_init=lam_init, online_max=online_max),
        grid=(batch, A_HEADS, nq),
        in_specs=[
            pl.BlockSpec((tq, A_VDIM), qmap),
            pl.BlockSpec((seq, A_VDIM), kvmap),
            pl.BlockSpec((seq, 2 * A_VDIM), kvmap),
            pl.BlockSpec((tq, A_VDIM), qmap),
            vec, vec, vec, vec,
            pl.BlockSpec((1, A_VDIM), const),
        ],
        out_specs=pl.BlockSpec((tq, A_VDIM), qmap),
        out_shape=jax.ShapeDtypeStruct((batch * seq, A_WIDTH), jnp.bfloat16),
        scratch_shapes=[
            pltpu.VMEM((2, tq, 2 * A_VDIM), jnp.float32),
            pltpu.VMEM((2, tq, LANES), jnp.float32),
        ],
        compiler_params=pltpu.CompilerParams(
            dimension_semantics=("arbitrary", "arbitrary", "arbitrary"), vmem_limit_bytes=V7X_VMEM_LIMIT),
        name="diff_attn_online" if online_max else "diff_attn",
    )(qa, ka, va_ext, ga, lq1, lk1, lq2, lk2, subln_w)


def _swa_kernel(q_ref, kp_ref, kc_ref, vp_ref, vc_ref, sink_ref, g_ref, o_ref):
    tq = q_ref.shape[1]
    blk = WINDOW
    i = pl.program_id(2)
    rows = B_GROUP * blk
    per_half = CHUNK // HEAD_DIM
    own = (lax.broadcasted_iota(jnp.int32, (blk, blk), 1) <= lax.broadcasted_iota(jnp.int32, (blk, blk), 0))
    vgroup = lax.broadcasted_iota(jnp.int32, (2 * blk, CHUNK), 1) // HEAD_DIM
    ogroup = lax.broadcasted_iota(jnp.int32, (blk, CHUNK), 1) // HEAD_DIM
    zero = jnp.zeros((blk, blk), jnp.bfloat16)
    for n in range(tq // blk):
        q = q_ref[:, n * blk:(n + 1) * blk, :].reshape(rows, HEAD_DIM)
        if n == 0:
            k = jnp.concatenate([kp_ref[0], kc_ref[0, 0:blk, :]], axis=0)
            v = jnp.concatenate([vp_ref[0], vc_ref[0, 0:blk, :]], axis=0)
            prev_bias = jnp.where(i > 0, 0.0, NEG_INF)
        else:
            k = kc_ref[0, (n - 1) * blk:(n + 1) * blk, :]
            v = vc_ref[0, (n - 1) * blk:(n + 1) * blk, :]
            prev_bias = None
        s_all = lax.dot_general(q, k, (((1,), (1,)), ((), ())), preferred_element_type=jnp.float32)
        vstack = jnp.concatenate([jnp.where(vgroup == j, v, jnp.zeros_like(v)) for j in range(per_half)], axis=0)
        halves = []
        for hf in range(B_GROUP // per_half):
            probs = []
            scale = None
            for j in range(per_half):
                g = hf * per_half + j
                sink = sink_ref[0, g * blk:(g + 1) * blk, :]
                s_prev = s_all[g * blk:(g + 1) * blk, :blk]
                if prev_bias is not None:
                    s_prev = s_prev + prev_bias
                s = jnp.where(own, s_all[g * blk:(g + 1) * blk, blk:], s_prev)
                m = jnp.maximum(sink, jnp.max(s, axis=-1, keepdims=True))
                p = jnp.exp2(s - m)
                denom = jnp.exp2(sink - m) + jnp.sum(p, axis=-1, keepdims=True)
                pb = p.astype(jnp.bfloat16)
                probs += [jnp.where(own, zero, pb), jnp.where(own, pb, zero)]
                rg = jnp.tile(1.0 / denom, (1, 2))
                scale = rg if scale is None else jnp.where(ogroup == j, rg, scale)
            acc = jnp.dot(jnp.concatenate(probs, axis=1), vstack, preferred_element_type=jnp.float32)
            halves.append(acc * scale)
        o = jnp.concatenate(halves, axis=1)
        gate = g_ref[n * blk:(n + 1) * blk, :].astype(jnp.float32)
        o_ref[n * blk:(n + 1) * blk, :] = (o * gate).astype(o_ref.dtype)


def _swa_attn(qb, kb, vb_rep, gb, sink_rep, batch, seq):
    tq = SWA_TQ
    nq = seq // tq
    r = tq // WINDOW
    gw = B_GROUP * HEAD_DIM
    cur = lambda b, kv, i: (kv, b * nq + i, 0)
    prev = lambda b, kv, i: (kv, jnp.maximum(b * nq * r + i * r - 1, 0), 0)
    omap = lambda b, kv, i: (b * nq + i, kv)
    return pl.pallas_call(
        _swa_kernel,
        grid=(batch, B_KV_HEADS, nq),
        in_specs=[
            pl.BlockSpec((B_GROUP, tq, HEAD_DIM), cur),
            pl.BlockSpec((1, WINDOW, HEAD_DIM), prev),
            pl.BlockSpec((1, tq, HEAD_DIM), cur),
            pl.BlockSpec((1, WINDOW, 4 * HEAD_DIM), prev),
            pl.BlockSpec((1, tq, 4 * HEAD_DIM), cur),
            pl.BlockSpec((1, B_GROUP * WINDOW, LANES), lambda b, kv, i: (kv, 0, 0)),
            pl.BlockSpec((tq, gw), omap),
        ],
        out_specs=pl.BlockSpec((tq, gw), omap),
        out_shape=jax.ShapeDtypeStruct((batch * seq, B_WIDTH), jnp.bfloat16),
        compiler_params=pltpu.CompilerParams(
            dimension_semantics=("arbitrary", "arbitrary", "arbitrary"), vmem_limit_bytes=V7X_VMEM_LIMIT),
        name="swa_attn",
    )(qb, kb, kb, vb_rep, vb_rep, sink_rep, gb)


def _out_proj_kernel(ya_ref, yb_ref, wa_ref, wb_ref, x_ref, o_ref):
    acc = jnp.dot(ya_ref[...], wa_ref[...], preferred_element_type=jnp.float32)
    acc = acc + jnp.dot(yb_ref[...], wb_ref[...], preferred_element_type=jnp.float32)
    o_ref[...] = x_ref[...] + acc


def _out_proj(ya, yb, w_o, x2):
    m = x2.shape[0]
    bm = OUT_BM
    row = lambda i: (i, 0)
    return pl.pallas_call(
        _out_proj_kernel,
        grid=(m // bm,),
        in_specs=[
            pl.BlockSpec((bm, A_WIDTH), row),
            pl.BlockSpec((bm, B_WIDTH), row),
            pl.BlockSpec((A_WIDTH, D_MODEL), lambda i: (0, 0), pipeline_mode=pl.Buffered(1)),
            pl.BlockSpec((B_WIDTH, D_MODEL), lambda i: (1, 0), pipeline_mode=pl.Buffered(1)),
            pl.BlockSpec((bm, D_MODEL), row),
        ],
        out_specs=pl.BlockSpec((bm, D_MODEL), row),
        out_shape=jax.ShapeDtypeStruct((m, D_MODEL), jnp.float32),
        compiler_params=pltpu.CompilerParams(
            dimension_semantics=("arbitrary",), vmem_limit_bytes=V7X_VMEM_LIMIT),
        name="out_proj",
    )(ya, yb, w_o, w_o, x2)


def kernel(x, positions, norm_w, w_in, q_norm_a, k_norm_a, lambda_q1, lambda_k1, lambda_q2, lambda_k2,
           subln_w, q_norm_b, k_norm_b, sinks, w_out):
    batch, seq, _ = x.shape
    depth = norm_w.shape[0]
    m = batch * seq
    f32 = jnp.float32

    cos_t, sin_t = _rope_tables(positions, m)
    gid = jnp.arange(CHUNK) // HEAD_DIM
    gmat = jnp.where(gid[:, None] == gid[None, :], 1.0 / HEAD_DIM, 0.0).astype(jnp.bfloat16)
    qk_scale = HEAD_DIM ** -0.5 * LOG2E
    tile = lambda w: jnp.tile(w.astype(f32), CHUNK // HEAD_DIM).reshape(1, CHUNK)

    x2 = x.reshape(m, D_MODEL)
    for layer in range(depth):
        lam_init = 0.8 - 0.6 * math.exp(-0.3 * layer)
        w_bf16 = w_in[layer].astype(jnp.bfloat16)
        qa, ka, va_ext, ga, qb, kb, vb_rep, gb = _in_proj(
            x2, cos_t, sin_t, norm_w[layer].reshape(1, D_MODEL), w_bf16, gmat,
            tile(q_norm_a[layer]) * qk_scale, tile(k_norm_a[layer]),
            tile(q_norm_b[layer]) * qk_scale, tile(k_norm_b[layer]))
        vec = lambda p: p[layer].astype(f32).reshape(1, HEAD_DIM)
        attn_args = (qa, ka, va_ext, ga, vec(lambda_q1), vec(lambda_k1), vec(lambda_q2), vec(lambda_k2),
                     subln_w[layer].astype(f32).reshape(1, A_VDIM))
        attn = functools.partial(_diff_attn, batch=batch, seq=seq, lam_init=lam_init)
        score_bound = (HEAD_DIM * qk_scale * jnp.max(jnp.abs(q_norm_a[layer].astype(f32)))
                       * jnp.max(jnp.abs(k_norm_a[layer].astype(f32))))
        ya = lax.cond(score_bound <= RAW_EXP_SCORE_BOUND,
                      functools.partial(attn, online_max=False),
                      functools.partial(attn, online_max=True), *attn_args)
        sink_rep = jnp.broadcast_to(
            jnp.repeat(sinks[layer].astype(f32) * LOG2E, WINDOW).reshape(B_KV_HEADS, B_GROUP * WINDOW, 1),
            (B_KV_HEADS, B_GROUP * WINDOW, LANES))
        yb = _swa_attn(qb, kb, vb_rep, gb, sink_rep, batch, seq)
        x2 = _out_proj(ya, yb, w_out[layer].astype(jnp.bfloat16), x2)
    return x2.reshape(batch, seq, D_MODEL)
```

```python
import functools
import math

import jax
import jax.numpy as jnp
from jax import lax
from jax.experimental import pallas as pl
from jax.experimental.pallas import tpu as pltpu

D_MODEL = 2048
HEAD_DIM = 64
ROPE_THETA = 10000.0
EPS = 1e-6
NEG_INF = -1e30
LOG2E = math.log2(math.e)

A_HEADS = 8
A_VDIM = 128
A_WIDTH = 1024
B_HEADS = 16
B_KV_HEADS = 2
B_GROUP = 8
B_WIDTH = 1024
WINDOW = 128
PROJ_WIDTH = 6400

OFF_QA, OFF_KA, OFF_VA, OFF_GA, OFF_QB, OFF_KB, OFF_VB, OFF_GB = 0, 1024, 2048, 3072, 4096, 5120, 5248, 5376

V7X_VMEM_LIMIT = 56 * 1024 * 1024
LANES = 128
CHUNK = 256

ROPE_ROWS = 512
PROJ_BM = 512
ATT_TQ = 2048
ATT_QS = 256
SWA_TQ = 1024
OUT_BM = 512
BF16_SUBLANES = 16
VT_ROWS = A_VDIM + BF16_SUBLANES

RAW_EXP_SCORE_BOUND = 60.0


def _rope_kernel(pos_ref, invf_ref, sign_ref, cos_ref, sin_ref):
    half = HEAD_DIM // 2
    groups = LANES // half
    ang = pos_ref[...] * invf_ref[...]
    group = lax.broadcasted_iota(jnp.int32, ang.shape, 1) // half
    for src, dst, sgn in ((jnp.cos(ang), cos_ref, None), (jnp.sin(ang), sin_ref, sign_ref[...])):
        rolled = [src] + [pltpu.roll(src, half * k, 1) for k in range(1, groups)]
        for t in range(groups):
            out = rolled[(0 - t) % groups]
            for g in range(1, groups):
                out = jnp.where(group == g, rolled[(g - t) % groups], out)
            dst[t] = out if sgn is None else out * sgn


def _rope_tables(positions, m):
    half = HEAD_DIM // 2
    f32 = jnp.float32
    groups = LANES // half
    rows = m // groups
    inv_freq = ROPE_THETA ** (-(jnp.arange(0, HEAD_DIM, 2, dtype=f32) / HEAD_DIM))
    pos_rep = jnp.repeat(positions.reshape(groups, rows).astype(f32).T, half, axis=1)
    invf = jnp.tile(inv_freq, groups).reshape(1, LANES)
    sign = jnp.tile(jnp.concatenate([-jnp.ones((half,), f32), jnp.ones((half,), f32)]), LANES // HEAD_DIM)
    row = lambda i: (i, 0)
    const = lambda i: (0, 0)
    slab = lambda i: (0, i, 0)
    cos_t, sin_t = pl.pallas_call(
        _rope_kernel,
        grid=(rows // ROPE_ROWS,),
        in_specs=[pl.BlockSpec((ROPE_ROWS, LANES), row), pl.BlockSpec((1, LANES), const),
                  pl.BlockSpec((1, LANES), const)],
        out_specs=(pl.BlockSpec((groups, ROPE_ROWS, LANES), slab), pl.BlockSpec((groups, ROPE_ROWS, LANES), slab)),
        out_shape=(jax.ShapeDtypeStruct((groups, rows, LANES), f32),) * 2,
        name="rope_table",
    )(pos_rep, invf, sign.reshape(1, LANES))
    return cos_t.reshape(m, LANES), sin_t.reshape(m, LANES)


def _in_proj_kernel(x_ref, cos_ref, sin_ref, nw_ref, w_ref, gmat_ref,
                    qna_ref, kna_ref, qnb_ref, knb_ref,
                    qa_ref, ka_ref, va_ref, ga_ref, qb_ref, kb_ref, vb_ref, gb_ref):
    bm = x_ref.shape[0]
    x = x_ref[...]
    ms = jnp.mean(x * x, axis=-1, keepdims=True)
    h = ((x * lax.rsqrt(ms + EPS)) * nw_ref[...]).astype(jnp.bfloat16)

    cos = jnp.tile(cos_ref[...], (1, CHUNK // LANES))
    sin_signed = jnp.tile(sin_ref[...], (1, CHUNK // LANES))
    lane = lax.broadcasted_iota(jnp.int32, (bm, CHUNK), 1)
    first_half = (lane % HEAD_DIM) < (HEAD_DIM // 2)
    gmat = gmat_ref[...]
    ones_blk = jnp.ones((BF16_SUBLANES, bm), jnp.bfloat16)

    def proj(c0, width=CHUNK):
        return jnp.dot(h, w_ref[:, c0:c0 + width], preferred_element_type=jnp.float32)

    def norm_rope(t, w):
        msq = jnp.dot((t * t).astype(jnp.bfloat16), gmat, preferred_element_type=jnp.float32)
        y = t * lax.rsqrt(msq + EPS) * w
        rot = jnp.where(first_half, pltpu.roll(y, CHUNK - HEAD_DIM // 2, 1), pltpu.roll(y, HEAD_DIM // 2, 1))
        return y * cos + rot * sin_signed

    def silu(t):
        return t / (1.0 + jnp.exp(-t))

    def put_qa(t, o):
        qa_ref[:, o:o + CHUNK] = norm_rope(t, qna_ref[...]).astype(jnp.bfloat16)

    def put_ka(t, o):
        ka_ref[:, o:o + CHUNK] = norm_rope(t, kna_ref[...]).astype(jnp.bfloat16)

    def put_va(t, o):
        vat = t.T.astype(jnp.bfloat16)
        for j in range(CHUNK // A_VDIM):
            head = o // A_VDIM + j
            va_ref[head, 0, 0:A_VDIM, :] = vat[j * A_VDIM:(j + 1) * A_VDIM, :]
            va_ref[head, 0, A_VDIM:VT_ROWS, :] = ones_blk

    def put_ga(t, o):
        ga_ref[:, o:o + CHUNK] = silu(t).astype(jnp.bfloat16)

    def put_gb(t, o):
        gb_ref[:, o:o + CHUNK] = silu(t).astype(jnp.bfloat16)

    def put_qb(t, o):
        qb = norm_rope(t, qnb_ref[...]).astype(jnp.bfloat16)
        for j in range(CHUNK // HEAD_DIM):
            qb_ref[o // HEAD_DIM + j] = qb[:, j * HEAD_DIM:(j + 1) * HEAD_DIM]

    def put_kvb(kv, o):
        kbn = norm_rope(kv, knb_ref[...]).astype(jnp.bfloat16)
        for j in range(B_KV_HEADS):
            kb_ref[j] = kbn[:, j * HEAD_DIM:(j + 1) * HEAD_DIM]
        v_lo = lane < CHUNK - HEAD_DIM
        v0 = jnp.where(v_lo, kv, pltpu.roll(kv, HEAD_DIM, 1))[:, LANES:]
        v1 = jnp.where(v_lo, pltpu.roll(kv, CHUNK - HEAD_DIM, 1), kv)[:, LANES:]
        vb_ref[0] = jnp.concatenate([v0, v0], axis=1).astype(jnp.bfloat16)
        vb_ref[1] = jnp.concatenate([v1, v1], axis=1).astype(jnp.bfloat16)

    tasks = [(OFF_KB, put_kvb, 0)]
    for c in range(A_WIDTH // CHUNK):
        o = c * CHUNK
        tasks += [(OFF_QA + o, put_qa, o), (OFF_KA + o, put_ka, o), (OFF_QB + o, put_qb, o),
                  (OFF_VA + o, put_va, o), (OFF_GA + o, put_ga, o), (OFF_GB + o, put_gb, o)]
    pending = None
    for col, put, o in tasks:
        t = proj(col)
        if pending is not None:
            pending[0](pending[1], pending[2])
        pending = (put, t, o)
    pending[0](pending[1], pending[2])


def _in_proj(x2, cos_t, sin_t, norm_w, w_bf16, gmat, qna, kna, qnb, knb):
    m = x2.shape[0]
    bm = PROJ_BM
    row = lambda i: (i, 0)
    const = lambda i: (0, 0)
    hm = lambda i: (0, i, 0)
    bf = jnp.bfloat16
    out_shape = (
        jax.ShapeDtypeStruct((m, A_WIDTH), bf),
        jax.ShapeDtypeStruct((m, A_WIDTH), bf),
        jax.ShapeDtypeStruct((A_HEADS, m // ATT_TQ, VT_ROWS, ATT_TQ), bf),
        jax.ShapeDtypeStruct((m, A_WIDTH), bf),
        jax.ShapeDtypeStruct((B_HEADS, m, HEAD_DIM), bf),
        jax.ShapeDtypeStruct((B_KV_HEADS, m, HEAD_DIM), bf),
        jax.ShapeDtypeStruct((B_KV_HEADS, m, 4 * HEAD_DIM), bf),
        jax.ShapeDtypeStruct((m, B_WIDTH), bf),
    )
    out_specs = (
        pl.BlockSpec((bm, A_WIDTH), row), pl.BlockSpec((bm, A_WIDTH), row),
        pl.BlockSpec((A_HEADS, 1, VT_ROWS, bm), lambda i: (0, i // (ATT_TQ // bm), 0, i % (ATT_TQ // bm))),
        pl.BlockSpec((bm, A_WIDTH), row),
        pl.BlockSpec((B_HEADS, bm, HEAD_DIM), hm),
        pl.BlockSpec((B_KV_HEADS, bm, HEAD_DIM), hm),
        pl.BlockSpec((B_KV_HEADS, bm, 4 * HEAD_DIM), hm),
        pl.BlockSpec((bm, B_WIDTH), row),
    )
    in_specs = [
        pl.BlockSpec((bm, D_MODEL), row),
        pl.BlockSpec((bm, LANES), row),
        pl.BlockSpec((bm, LANES), row),
        pl.BlockSpec((1, D_MODEL), const),
        pl.BlockSpec((D_MODEL, PROJ_WIDTH), const, pipeline_mode=pl.Buffered(1)),
        pl.BlockSpec((CHUNK, CHUNK), const),
        pl.BlockSpec((1, CHUNK), const), pl.BlockSpec((1, CHUNK), const),
        pl.BlockSpec((1, CHUNK), const), pl.BlockSpec((1, CHUNK), const),
    ]
    return pl.pallas_call(
        _in_proj_kernel,
        grid=(m // bm,),
        in_specs=in_specs,
        out_specs=out_specs,
        out_shape=out_shape,
        compiler_params=pltpu.CompilerParams(
            dimension_semantics=("arbitrary",), vmem_limit_bytes=V7X_VMEM_LIMIT),
        name="in_proj",
    )(x2, cos_t, sin_t, norm_w, w_bf16, gmat, qna, kna, qnb, knb)


def _diff_attn_kernel(q_ref, k_ref, vt_ref, g_ref, lq1_ref, lk1_ref, lq2_ref, lk2_ref, sw_ref,
                      o_ref, acc_ref, m_ref, *, lam_init, online_max):
    tq = q_ref.shape[0]
    nsub = tq // ATT_QS
    qi = pl.program_id(2)

    q = q_ref[...]
    lane = lax.broadcasted_iota(jnp.int32, q.shape, 1)
    zero = jnp.zeros_like(q)
    qs = (jnp.where(lane < HEAD_DIM, q, zero), jnp.where(lane >= HEAD_DIM, q, zero))

    acc_ref[...] = jnp.zeros(acc_ref.shape, jnp.float32)
    if online_max:
        m_ref[...] = jnp.full(m_ref.shape, NEG_INF, jnp.float32)

    def scores(c, r, j, nk, masked):
        cols = slice(r * ATT_QS, (r + 1) * ATT_QS)
        k = k_ref[pl.ds(pl.multiple_of(j * tq, tq), nk), :]
        s = lax.dot_general(k, qs[c][cols], (((1,), (1,)), ((), ())), preferred_element_type=jnp.float32)
        if masked:
            key = lax.broadcasted_iota(jnp.int32, s.shape, 0)
            qry = lax.broadcasted_iota(jnp.int32, s.shape, 1)
            s = jnp.where(key <= qry + (nk - ATT_QS), s, NEG_INF)
        return s

    def accumulate(s, c, r, j, nk):
        cols = slice(r * ATT_QS, (r + 1) * ATT_QS)
        vt = vt_ref[0, j, :, 0:nk]
        if online_max:
            m_old = m_ref[c, :, cols]
            m_new = jnp.maximum(m_old, jnp.max(s, axis=0, keepdims=True))
            alpha = jnp.exp2(m_old - m_new)
            m_ref[c, :, cols] = m_new
            pv = jnp.dot(vt, jnp.exp2(s - m_new).astype(jnp.bfloat16), preferred_element_type=jnp.float32)
            acc_ref[c, :, cols] = acc_ref[c, :, cols] * alpha + pv
        else:
            pv = jnp.dot(vt, jnp.exp2(s).astype(jnp.bfloat16), preferred_element_type=jnp.float32)
            acc_ref[c, :, cols] += pv

    def sweep(j, nk_of, masked):
        units = [(c, r) for r in range(nsub) for c in range(2)]
        s_next = scores(*units[0], j, nk_of(units[0][1]), masked)
        for idx, (c, r) in enumerate(units):
            s = s_next
            if idx + 1 < len(units):
                cn, rn = units[idx + 1]
                s_next = scores(cn, rn, j, nk_of(rn), masked)
            accumulate(s, c, r, j, nk_of(r))

    def body(j, carry):
        sweep(j, lambda r: tq, False)
        return carry

    lax.fori_loop(0, qi, body, 0)
    sweep(qi, lambda r: (r + 1) * ATT_QS, True)

    lam = (jnp.exp(jnp.sum(lq1_ref[...] * lk1_ref[...], axis=-1, keepdims=True))
           - jnp.exp(jnp.sum(lq2_ref[...] * lk2_ref[...], axis=-1, keepdims=True)) + lam_init)
    a1 = acc_ref[0]
    a2 = acc_ref[1]
    ot = a1[:A_VDIM] / a1[A_VDIM:A_VDIM + 1] - lam * (a2[:A_VDIM] / a2[A_VDIM:A_VDIM + 1])
    o = ot.T
    o = o * lax.rsqrt(jnp.mean(o * o, axis=-1, keepdims=True) + EPS) * sw_ref[...] * (1.0 - lam_init)
    o_ref[...] = (o * g_ref[...].astype(jnp.float32)).astype(o_ref.dtype)


def _diff_attn(qa, ka, va_ext, ga, lq1, lk1, lq2, lk2, subln_w, *, batch, seq, lam_init, online_max):
    tq = ATT_TQ
    nq = seq // tq
    qmap = lambda b, h, i: (b * nq + i, h)
    kvmap = lambda b, h, i: (b, h)
    const = lambda b, h, i: (0, 0)
    vec = pl.BlockSpec((1, HEAD_DIM), const)
    return pl.pallas_call(
        functools.partial(_diff_attn_kernel, lam_init=lam_init, online_max=online_max),
        grid=(batch, A_HEADS, nq),
        in_specs=[
            pl.BlockSpec((tq, A_VDIM), qmap),
            pl.BlockSpec((seq, A_VDIM), kvmap),
            pl.BlockSpec((1, nq, VT_ROWS, tq), lambda b, h, i: (h, b, 0, 0)),
            pl.BlockSpec((tq, A_VDIM), qmap),
            vec, vec, vec, vec,
            pl.BlockSpec((1, A_VDIM), const),
        ],
        out_specs=pl.BlockSpec((tq, A_VDIM), qmap),
        out_shape=jax.ShapeDtypeStruct((batch * seq, A_WIDTH), jnp.bfloat16),
        scratch_shapes=[
            pltpu.VMEM((2, VT_ROWS, tq), jnp.float32),
            pltpu.VMEM((2, 1, tq), jnp.float32),
        ],
        compiler_params=pltpu.CompilerParams(
            dimension_semantics=("arbitrary", "arbitrary", "arbitrary"), vmem_limit_bytes=V7X_VMEM_LIMIT),
        name="diff_attn_online" if online_max else "diff_attn",
    )(qa, ka, va_ext, ga, lq1, lk1, lq2, lk2, subln_w)


def _swa_kernel(q_ref, kp_ref, kc_ref, vp_ref, vc_ref, sink_ref, g_ref, o_ref):
    tq = q_ref.shape[1]
    blk = WINDOW
    i = pl.program_id(2)
    rows = B_GROUP * blk
    per_half = CHUNK // HEAD_DIM
    own = (lax.broadcasted_iota(jnp.int32, (blk, blk), 1) <= lax.broadcasted_iota(jnp.int32, (blk, blk), 0))
    vgroup = lax.broadcasted_iota(jnp.int32, (2 * blk, CHUNK), 1) // HEAD_DIM
    ogroup = lax.broadcasted_iota(jnp.int32, (blk, CHUNK), 1) // HEAD_DIM
    zero = jnp.zeros((blk, blk), jnp.bfloat16)
    for n in range(tq // blk):
        q = q_ref[:, n * blk:(n + 1) * blk, :].reshape(rows, HEAD_DIM)
        if n == 0:
            k = jnp.concatenate([kp_ref[0], kc_ref[0, 0:blk, :]], axis=0)
            v = jnp.concatenate([vp_ref[0], vc_ref[0, 0:blk, :]], axis=0)
            prev_bias = jnp.where(i > 0, 0.0, NEG_INF)
        else:
            k = kc_ref[0, (n - 1) * blk:(n + 1) * blk, :]
            v = vc_ref[0, (n - 1) * blk:(n + 1) * blk, :]
            prev_bias = None
        s_all = lax.dot_general(q, k, (((1,), (1,)), ((), ())), preferred_element_type=jnp.float32)
        vstack = jnp.concatenate([jnp.where(vgroup == j, v, jnp.zeros_like(v)) for j in range(per_half)], axis=0)
        halves = []
        for hf in range(B_GROUP // per_half):
            probs = []
            scale = None
            for j in range(per_half):
                g = hf * per_half + j
                sink = sink_ref[0, g * blk:(g + 1) * blk, :]
                s_prev = s_all[g * blk:(g + 1) * blk, :blk]
                if prev_bias is not None:
                    s_prev = s_prev + prev_bias
                s = jnp.where(own, s_all[g * blk:(g + 1) * blk, blk:], s_prev)
                m = jnp.maximum(sink, jnp.max(s, axis=-1, keepdims=True))
                p = jnp.exp2(s - m)
                denom = jnp.exp2(sink - m) + jnp.sum(p, axis=-1, keepdims=True)
                pb = p.astype(jnp.bfloat16)
                probs += [jnp.where(own, zero, pb), jnp.where(own, pb, zero)]
                rg = jnp.tile(1.0 / denom, (1, 2))
                scale = rg if scale is None else jnp.where(ogroup == j, rg, scale)
            acc = jnp.dot(jnp.concatenate(probs, axis=1), vstack, preferred_element_type=jnp.float32)
            halves.append(acc * scale)
        o = jnp.concatenate(halves, axis=1)
        gate = g_ref[n * blk:(n + 1) * blk, :].astype(jnp.float32)
        o_ref[n * blk:(n + 1) * blk, :] = (o * gate).astype(o_ref.dtype)


def _swa_attn(qb, kb, vb_rep, gb, sink_rep, batch, seq):
    tq = SWA_TQ
    nq = seq // tq
    r = tq // WINDOW
    gw = B_GROUP * HEAD_DIM
    cur = lambda b, kv, i: (kv, b * nq + i, 0)
    prev = lambda b, kv, i: (kv, jnp.maximum(b * nq * r + i * r - 1, 0), 0)
    omap = lambda b, kv, i: (b * nq + i, kv)
    return pl.pallas_call(
        _swa_kernel,
        grid=(batch, B_KV_HEADS, nq),
        in_specs=[
            pl.BlockSpec((B_GROUP, tq, HEAD_DIM), cur),
            pl.BlockSpec((1, WINDOW, HEAD_DIM), prev),
            pl.BlockSpec((1, tq, HEAD_DIM), cur),
            pl.BlockSpec((1, WINDOW, 4 * HEAD_DIM), prev),
            pl.BlockSpec((1, tq, 4 * HEAD_DIM), cur),
            pl.BlockSpec((1, B_GROUP * WINDOW, LANES), lambda b, kv, i: (kv, 0, 0)),
            pl.BlockSpec((tq, gw), omap),
        ],
        out_specs=pl.BlockSpec((tq, gw), omap),
        out_shape=jax.ShapeDtypeStruct((batch * seq, B_WIDTH), jnp.bfloat16),
        compiler_params=pltpu.CompilerParams(
            dimension_semantics=("arbitrary", "arbitrary", "arbitrary"), vmem_limit_bytes=V7X_VMEM_LIMIT),
        name="swa_attn",
    )(qb, kb, kb, vb_rep, vb_rep, sink_rep, gb)


def _out_proj_kernel(ya_ref, yb_ref, wa_ref, wb_ref, x_ref, o_ref):
    acc = jnp.dot(ya_ref[...], wa_ref[...], preferred_element_type=jnp.float32)
    acc = acc + jnp.dot(yb_ref[...], wb_ref[...], preferred_element_type=jnp.float32)
    o_ref[...] = x_ref[...] + acc


def _out_proj(ya, yb, w_o, x2):
    m = x2.shape[0]
    bm = OUT_BM
    row = lambda i: (i, 0)
    return pl.pallas_call(
        _out_proj_kernel,
        grid=(m // bm,),
        in_specs=[
            pl.BlockSpec((bm, A_WIDTH), row),
            pl.BlockSpec((bm, B_WIDTH), row),
            pl.BlockSpec((A_WIDTH, D_MODEL), lambda i: (0, 0), pipeline_mode=pl.Buffered(1)),
            pl.BlockSpec((B_WIDTH, D_MODEL), lambda i: (1, 0), pipeline_mode=pl.Buffered(1)),
            pl.BlockSpec((bm, D_MODEL), row),
        ],
        out_specs=pl.BlockSpec((bm, D_MODEL), row),
        out_shape=jax.ShapeDtypeStruct((m, D_MODEL), jnp.float32),
        compiler_params=pltpu.CompilerParams(
            dimension_semantics=("arbitrary",), vmem_limit_bytes=V7X_VMEM_LIMIT),
        name="out_proj",
    )(ya, yb, w_o, w_o, x2)


def kernel(x, positions, norm_w, w_in, q_norm_a, k_norm_a, lambda_q1, lambda_k1, lambda_q2, lambda_k2,
           subln_w, q_norm_b, k_norm_b, sinks, w_out):
    batch, seq, _ = x.shape
    depth = norm_w.shape[0]
    m = batch * seq
    f32 = jnp.float32

    cos_t, sin_t = _rope_tables(positions, m)
    gid = jnp.arange(CHUNK) // HEAD_DIM
    gmat = jnp.where(gid[:, None] == gid[None, :], 1.0 / HEAD_DIM, 0.0).astype(jnp.bfloat16)
    qk_scale = HEAD_DIM ** -0.5 * LOG2E
    tile = lambda w: jnp.tile(w.astype(f32), CHUNK // HEAD_DIM).reshape(1, CHUNK)

    x2 = x.reshape(m, D_MODEL)
    for layer in range(depth):
        lam_init = 0.8 - 0.6 * math.exp(-0.3 * layer)
        w_bf16 = w_in[layer].astype(jnp.bfloat16)
        qa, ka, va_ext, ga, qb, kb, vb_rep, gb = _in_proj(
            x2, cos_t, sin_t, norm_w[layer].reshape(1, D_MODEL), w_bf16, gmat,
            tile(q_norm_a[layer]) * qk_scale, tile(k_norm_a[layer]),
            tile(q_norm_b[layer]) * qk_scale, tile(k_norm_b[layer]))
        vec = lambda p: p[layer].astype(f32).reshape(1, HEAD_DIM)
        attn_args = (qa, ka, va_ext, ga, vec(lambda_q1), vec(lambda_k1), vec(lambda_q2), vec(lambda_k2),
                     subln_w[layer].astype(f32).reshape(1, A_VDIM))
        attn = functools.partial(_diff_attn, batch=batch, seq=seq, lam_init=lam_init)
        score_bound = (HEAD_DIM * qk_scale * jnp.max(jnp.abs(q_norm_a[layer].astype(f32)))
                       * jnp.max(jnp.abs(k_norm_a[layer].astype(f32))))
        ya = lax.cond(score_bound <= RAW_EXP_SCORE_BOUND,
                      functools.partial(attn, online_max=False),
                      functools.partial(attn, online_max=True), *attn_args)
        sink_rep = jnp.broadcast_to(
            jnp.repeat(sinks[layer].astype(f32) * LOG2E, WINDOW).reshape(B_KV_HEADS, B_GROUP * WINDOW, 1),
            (B_KV_HEADS, B_GROUP * WINDOW, LANES))
        yb = _swa_attn(qb, kb, vb_rep, gb, sink_rep, batch, seq)
        x2 = _out_proj(ya, yb, w_out[layer].astype(jnp.bfloat16), x2)
    return x2.reshape(batch, seq, D_MODEL)
```

```python
import functools
import math

import jax
import jax.numpy as jnp
from jax import lax
from jax.experimental import pallas as pl
from jax.experimental.pallas import tpu as pltpu

D_MODEL = 2048
HEAD_DIM = 64
ROPE_THETA = 10000.0
EPS = 1e-6
NEG_INF = -1e30
LOG2E = math.log2(math.e)

A_HEADS = 8
A_VDIM = 128
A_WIDTH = 1024
B_HEADS = 16
B_KV_HEADS = 2
B_GROUP = 8
B_WIDTH = 1024
WINDOW = 128
PROJ_WIDTH = 6400

OFF_QA, OFF_KA, OFF_VA, OFF_GA, OFF_QB, OFF_KB, OFF_VB, OFF_GB = 0, 1024, 2048, 3072, 4096, 5120, 5248, 5376

V7X_VMEM_LIMIT = 56 * 1024 * 1024
LANES = 128
CHUNK = 256

ROPE_ROWS = 512
PROJ_BM = 512
ATT_TQ = 2048
ATT_QS = 512
ATT_QS_DIAG = 256
SWA_TQ = 1024
OUT_BM = 512
F32_SUBLANES = 8

RAW_EXP_SCORE_BOUND = 60.0


def _rope_kernel(pos_ref, invf_ref, sign_ref, cos_ref, sin_ref):
    half = HEAD_DIM // 2
    groups = LANES // half
    ang = pos_ref[...] * invf_ref[...]
    group = lax.broadcasted_iota(jnp.int32, ang.shape, 1) // half
    for src, dst, sgn in ((jnp.cos(ang), cos_ref, None), (jnp.sin(ang), sin_ref, sign_ref[...])):
        rolled = [src] + [pltpu.roll(src, half * k, 1) for k in range(1, groups)]
        for t in range(groups):
            out = rolled[(0 - t) % groups]
            for g in range(1, groups):
                out = jnp.where(group == g, rolled[(g - t) % groups], out)
            dst[t] = out if sgn is None else out * sgn


def _rope_tables(positions, m):
    half = HEAD_DIM // 2
    f32 = jnp.float32
    groups = LANES // half
    rows = m // groups
    inv_freq = ROPE_THETA ** (-(jnp.arange(0, HEAD_DIM, 2, dtype=f32) / HEAD_DIM))
    pos_rep = jnp.repeat(positions.reshape(groups, rows).astype(f32).T, half, axis=1)
    invf = jnp.tile(inv_freq, groups).reshape(1, LANES)
    sign = jnp.tile(jnp.concatenate([-jnp.ones((half,), f32), jnp.ones((half,), f32)]), LANES // HEAD_DIM)
    row = lambda i: (i, 0)
    const = lambda i: (0, 0)
    slab = lambda i: (0, i, 0)
    cos_t, sin_t = pl.pallas_call(
        _rope_kernel,
        grid=(rows // ROPE_ROWS,),
        in_specs=[pl.BlockSpec((ROPE_ROWS, LANES), row), pl.BlockSpec((1, LANES), const),
                  pl.BlockSpec((1, LANES), const)],
        out_specs=(pl.BlockSpec((groups, ROPE_ROWS, LANES), slab), pl.BlockSpec((groups, ROPE_ROWS, LANES), slab)),
        out_shape=(jax.ShapeDtypeStruct((groups, rows, LANES), f32),) * 2,
        name="rope_table",
    )(pos_rep, invf, sign.reshape(1, LANES))
    return cos_t.reshape(m, LANES), sin_t.reshape(m, LANES)


def _in_proj_kernel(x_ref, cos_ref, sin_ref, nw_ref, w_ref,
                    qna_ref, kna_ref, qnb_ref, knb_ref,
                    qa_ref, ka_ref, va_ref, ga_ref, qb_ref, kb_ref, vb_ref, gb_ref):
    bm = x_ref.shape[0]
    x = x_ref[...]
    ms = jnp.mean(x * x, axis=-1, keepdims=True)
    h = ((x * lax.rsqrt(ms + EPS)) * nw_ref[...]).astype(jnp.bfloat16)

    cos = jnp.tile(cos_ref[...], (1, CHUNK // LANES))
    sin_signed = jnp.tile(sin_ref[...], (1, CHUNK // LANES))
    lane = lax.broadcasted_iota(jnp.int32, (bm, CHUNK), 1)
    first_half = (lane % HEAD_DIM) < (HEAD_DIM // 2)
    head_group = lane // HEAD_DIM

    def proj(c0, width=CHUNK):
        return jnp.dot(h, w_ref[:, c0:c0 + width], preferred_element_type=jnp.float32)

    def norm_rope(t, w):
        t2 = t * t
        msq = None
        for g in range(CHUNK // HEAD_DIM):
            in_g = head_group == g
            part = jnp.sum(jnp.where(in_g, t2, 0.0), axis=-1, keepdims=True) * (1.0 / HEAD_DIM)
            msq = part if msq is None else jnp.where(in_g, part, msq)
        y = t * lax.rsqrt(msq + EPS) * w
        rot = jnp.where(first_half, pltpu.roll(y, CHUNK - HEAD_DIM // 2, 1), pltpu.roll(y, HEAD_DIM // 2, 1))
        return y * cos + rot * sin_signed

    def silu(t):
        return t / (1.0 + jnp.exp(-t))

    def put_qa(t, o):
        qa_ref[:, o:o + CHUNK] = norm_rope(t, qna_ref[...]).astype(jnp.bfloat16)

    def put_ka(t, o):
        ka_ref[:, o:o + CHUNK] = norm_rope(t, kna_ref[...]).astype(jnp.bfloat16)

    def put_va(t, o):
        vat = t.T.astype(jnp.bfloat16)
        for j in range(CHUNK // A_VDIM):
            va_ref[o // A_VDIM + j, 0] = vat[j * A_VDIM:(j + 1) * A_VDIM, :]

    def put_ga(t, o):
        ga_ref[:, o:o + CHUNK] = silu(t).astype(jnp.bfloat16)

    def put_gb(t, o):
        gb_ref[:, o:o + CHUNK] = silu(t).astype(jnp.bfloat16)

    def put_qb(t, o):
        qb = norm_rope(t, qnb_ref[...]).astype(jnp.bfloat16)
        for j in range(CHUNK // HEAD_DIM):
            qb_ref[o // HEAD_DIM + j] = qb[:, j * HEAD_DIM:(j + 1) * HEAD_DIM]

    def put_kvb(kv, o):
        kbn = norm_rope(kv, knb_ref[...]).astype(jnp.bfloat16)
        for j in range(B_KV_HEADS):
            kb_ref[j] = kbn[:, j * HEAD_DIM:(j + 1) * HEAD_DIM]
        v_lo = lane < CHUNK - HEAD_DIM
        v0 = jnp.where(v_lo, kv, pltpu.roll(kv, HEAD_DIM, 1))[:, LANES:]
        v1 = jnp.where(v_lo, pltpu.roll(kv, CHUNK - HEAD_DIM, 1), kv)[:, LANES:]
        vb_ref[0] = jnp.concatenate([v0, v0], axis=1).astype(jnp.bfloat16)
        vb_ref[1] = jnp.concatenate([v1, v1], axis=1).astype(jnp.bfloat16)

    tasks = [(OFF_KB, put_kvb, 0)]
    for c in range(A_WIDTH // CHUNK):
        o = c * CHUNK
        tasks += [(OFF_QA + o, put_qa, o), (OFF_KA + o, put_ka, o), (OFF_QB + o, put_qb, o),
                  (OFF_VA + o, put_va, o), (OFF_GA + o, put_ga, o), (OFF_GB + o, put_gb, o)]
    pending = None
    for col, put, o in tasks:
        t = proj(col)
        if pending is not None:
            pending[0](pending[1], pending[2])
        pending = (put, t, o)
    pending[0](pending[1], pending[2])


def _in_proj(x2, cos_t, sin_t, norm_w, w_bf16, qna, kna, qnb, knb):
    m = x2.shape[0]
    bm = PROJ_BM
    row = lambda i: (i, 0)
    const = lambda i: (0, 0)
    hm = lambda i: (0, i, 0)
    bf = jnp.bfloat16
    out_shape = (
        jax.ShapeDtypeStruct((m, A_WIDTH), bf),
        jax.ShapeDtypeStruct((m, A_WIDTH), bf),
        jax.ShapeDtypeStruct((A_HEADS, m // ATT_TQ, A_VDIM, ATT_TQ), bf),
        jax.ShapeDtypeStruct((m, A_WIDTH), bf),
        jax.ShapeDtypeStruct((B_HEADS, m, HEAD_DIM), bf),
        jax.ShapeDtypeStruct((B_KV_HEADS, m, HEAD_DIM), bf),
        jax.ShapeDtypeStruct((B_KV_HEADS, m, 4 * HEAD_DIM), bf),
        jax.ShapeDtypeStruct((m, B_WIDTH), bf),
    )
    out_specs = (
        pl.BlockSpec((bm, A_WIDTH), row), pl.BlockSpec((bm, A_WIDTH), row),
        pl.BlockSpec((A_HEADS, 1, A_VDIM, bm), lambda i: (0, i // (ATT_TQ // bm), 0, i % (ATT_TQ // bm))),
        pl.BlockSpec((bm, A_WIDTH), row),
        pl.BlockSpec((B_HEADS, bm, HEAD_DIM), hm),
        pl.BlockSpec((B_KV_HEADS, bm, HEAD_DIM), hm),
        pl.BlockSpec((B_KV_HEADS, bm, 4 * HEAD_DIM), hm),
        pl.BlockSpec((bm, B_WIDTH), row),
    )
    in_specs = [
        pl.BlockSpec((bm, D_MODEL), row),
        pl.BlockSpec((bm, LANES), row),
        pl.BlockSpec((bm, LANES), row),
        pl.BlockSpec((1, D_MODEL), const),
        pl.BlockSpec((D_MODEL, PROJ_WIDTH), const, pipeline_mode=pl.Buffered(1)),
        pl.BlockSpec((1, CHUNK), const), pl.BlockSpec((1, CHUNK), const),
        pl.BlockSpec((1, CHUNK), const), pl.BlockSpec((1, CHUNK), const),
    ]
    return pl.pallas_call(
        _in_proj_kernel,
        grid=(m // bm,),
        in_specs=in_specs,
        out_specs=out_specs,
        out_shape=out_shape,
        compiler_params=pltpu.CompilerParams(
            dimension_semantics=("arbitrary",), vmem_limit_bytes=V7X_VMEM_LIMIT),
        name="in_proj",
    )(x2, cos_t, sin_t, norm_w, w_bf16, qna, kna, qnb, knb)


def _diff_attn_kernel(q_ref, k_ref, vt_ref, g_ref, lq1_ref, lk1_ref, lq2_ref, lk2_ref, sw_ref,
                      o_ref, acc_ref, l_ref, m_ref, *, lam_init, online_max):
    tq = q_ref.shape[0]
    qi = pl.program_id(2)

    q = q_ref[...]
    lane = lax.broadcasted_iota(jnp.int32, q.shape, 1)
    zero = jnp.zeros_like(q)
    qs = (jnp.where(lane < HEAD_DIM, q, zero), jnp.where(lane >= HEAD_DIM, q, zero))

    acc_ref[...] = jnp.zeros(acc_ref.shape, jnp.float32)
    l_ref[...] = jnp.zeros(l_ref.shape, jnp.float32)
    if online_max:
        m_ref[...] = jnp.full(m_ref.shape, NEG_INF, jnp.float32)

    def scores(c, cols, j, nk, masked):
        k = k_ref[pl.ds(pl.multiple_of(j * tq, tq), nk), :]
        s = lax.dot_general(k, qs[c][cols], (((1,), (1,)), ((), ())), preferred_element_type=jnp.float32)
        if masked:
            key = lax.broadcasted_iota(jnp.int32, s.shape, 0)
            qry = lax.broadcasted_iota(jnp.int32, s.shape, 1)
            s = jnp.where(key <= qry + (nk - s.shape[1]), s, NEG_INF)
        return s

    def accumulate(s, c, cols, j):
        nk, nq_unit = s.shape
        vt = vt_ref[0, j, :, 0:nk]
        if online_max:
            m_old = m_ref[c, :, cols]
            m_new = jnp.maximum(m_old, jnp.max(s, axis=0, keepdims=True))
            alpha = jnp.exp2(m_old - m_new)
            m_ref[c, :, cols] = m_new
            p = jnp.exp2(s - m_new)
            pv = jnp.dot(vt, p.astype(jnp.bfloat16), preferred_element_type=jnp.float32)
            acc_ref[c, :, cols] = acc_ref[c, :, cols] * alpha + pv
            l_ref[c, :, cols] = (l_ref[c, :, cols] * alpha
                                 + p.reshape(nk // F32_SUBLANES, F32_SUBLANES, nq_unit).sum(axis=0))
        else:
            p = jnp.exp2(s)
            pv = jnp.dot(vt, p.astype(jnp.bfloat16), preferred_element_type=jnp.float32)
            acc_ref[c, :, cols] += pv
            l_ref[c, :, cols] += p.reshape(nk // F32_SUBLANES, F32_SUBLANES, nq_unit).sum(axis=0)

    def sweep(j, width, masked):
        units = [(c, slice(r * width, (r + 1) * width)) for r in range(tq // width) for c in range(2)]
        nk_of = (lambda cols: cols.stop) if masked else (lambda cols: tq)
        s_next = scores(*units[0], j, nk_of(units[0][1]), masked)
        for idx, (c, cols) in enumerate(units):
            s = s_next
            if idx + 1 < len(units):
                cn, colsn = units[idx + 1]
                s_next = scores(cn, colsn, j, nk_of(colsn), masked)
            accumulate(s, c, cols, j)

    def body(j, carry):
        sweep(j, ATT_QS, False)
        return carry

    lax.fori_loop(0, qi, body, 0)
    sweep(qi, ATT_QS_DIAG, True)

    lam = (jnp.exp(jnp.sum(lq1_ref[...] * lk1_ref[...], axis=-1, keepdims=True))
           - jnp.exp(jnp.sum(lq2_ref[...] * lk2_ref[...], axis=-1, keepdims=True)) + lam_init)
    l1 = jnp.sum(l_ref[0], axis=0, keepdims=True)
    l2 = jnp.sum(l_ref[1], axis=0, keepdims=True)
    ot = acc_ref[0] / l1 - lam * (acc_ref[1] / l2)
    o = ot.T
    o = o * lax.rsqrt(jnp.mean(o * o, axis=-1, keepdims=True) + EPS) * sw_ref[...] * (1.0 - lam_init)
    o_ref[...] = (o * g_ref[...].astype(jnp.float32)).astype(o_ref.dtype)


def _diff_attn(qa, ka, va_ext, ga, lq1, lk1, lq2, lk2, subln_w, *, batch, seq, lam_init, online_max):
    tq = ATT_TQ
    nq = seq // tq
    qmap = lambda b, h, i: (b * nq + i, h)
    kvmap = lambda b, h, i: (b, h)
    const = lambda b, h, i: (0, 0)
    vec = pl.BlockSpec((1, HEAD_DIM), const)
    return pl.pallas_call(
        functools.partial(_diff_attn_kernel, lam_init=lam_init, online_max=online_max),
        grid=(batch, A_HEADS, nq),
        in_specs=[
            pl.BlockSpec((tq, A_VDIM), qmap),
            pl.BlockSpec((seq, A_VDIM), kvmap),
            pl.BlockSpec((1, nq, A_VDIM, tq), lambda b, h, i: (h, b, 0, 0)),
            pl.BlockSpec((tq, A_VDIM), qmap),
            vec, vec, vec, vec,
            pl.BlockSpec((1, A_VDIM), const),
        ],
        out_specs=pl.BlockSpec((tq, A_VDIM), qmap),
        out_shape=jax.ShapeDtypeStruct((batch * seq, A_WIDTH), jnp.bfloat16),
        scratch_shapes=[
            pltpu.VMEM((2, A_VDIM, tq), jnp.float32),
            pltpu.VMEM((2, F32_SUBLANES, tq), jnp.float32),
            pltpu.VMEM((2, 1, tq), jnp.float32),
        ],
        compiler_params=pltpu.CompilerParams(
            dimension_semantics=("arbitrary", "arbitrary", "arbitrary"), vmem_limit_bytes=V7X_VMEM_LIMIT),
        name="diff_attn_online" if online_max else "diff_attn",
    )(qa, ka, va_ext, ga, lq1, lk1, lq2, lk2, subln_w)


def _swa_kernel(q_ref, kp_ref, kc_ref, vp_ref, vc_ref, sink_ref, g_ref, o_ref):
    tq = q_ref.shape[1]
    blk = WINDOW
    i = pl.program_id(2)
    rows = B_GROUP * blk
    per_half = CHUNK // HEAD_DIM
    own = (lax.broadcasted_iota(jnp.int32, (blk, blk), 1) <= lax.broadcasted_iota(jnp.int32, (blk, blk), 0))
    vgroup = lax.broadcasted_iota(jnp.int32, (2 * blk, CHUNK), 1) // HEAD_DIM
    ogroup = lax.broadcasted_iota(jnp.int32, (blk, CHUNK), 1) // HEAD_DIM
    zero = jnp.zeros((blk, blk), jnp.bfloat16)
    for n in range(tq // blk):
        q = q_ref[:, n * blk:(n + 1) * blk, :].reshape(rows, HEAD_DIM)
        if n == 0:
            k = jnp.concatenate([kp_ref[0], kc_ref[0, 0:blk, :]], axis=0)
            v = jnp.concatenate([vp_ref[0], vc_ref[0, 0:blk, :]], axis=0)
            prev_bias = jnp.where(i > 0, 0.0, NEG_INF)
        else:
            k = kc_ref[0, (n - 1) * blk:(n + 1) * blk, :]
            v = vc_ref[0, (n - 1) * blk:(n + 1) * blk, :]
            prev_bias = None
        s_all = lax.dot_general(q, k, (((1,), (1,)), ((), ())), preferred_element_type=jnp.float32)
        vstack = jnp.concatenate([jnp.where(vgroup == j, v, jnp.zeros_like(v)) for j in range(per_half)], axis=0)
        halves = []
        for hf in range(B_GROUP // per_half):
            probs = []
            scale = None
            for j in range(per_half):
                g = hf * per_half + j
                sink = sink_ref[0, g * blk:(g + 1) * blk, :]
                s_prev = s_all[g * blk:(g + 1) * blk, :blk]
                if prev_bias is not None:
                    s_prev = s_prev + prev_bias
                s = jnp.where(own, s_all[g * blk:(g + 1) * blk, blk:], s_prev)
                m = jnp.maximum(sink, jnp.max(s, axis=-1, keepdims=True))
                p = jnp.exp2(s - m)
                denom = jnp.exp2(sink - m) + jnp.sum(p, axis=-1, keepdims=True)
                pb = p.astype(jnp.bfloat16)
                probs += [jnp.where(own, zero, pb), jnp.where(own, pb, zero)]
                rg = jnp.tile(1.0 / denom, (1, 2))
                scale = rg if scale is None else jnp.where(ogroup == j, rg, scale)
            acc = jnp.dot(jnp.concatenate(probs, axis=1), vstack, preferred_element_type=jnp.float32)
            halves.append(acc * scale)
        o = jnp.concatenate(halves, axis=1)
        gate = g_ref[n * blk:(n + 1) * blk, :].astype(jnp.float32)
        o_ref[n * blk:(n + 1) * blk, :] = (o * gate).astype(o_ref.dtype)


def _swa_attn(qb, kb, vb_rep, gb, sink_rep, batch, seq):
    tq = SWA_TQ
    nq = seq // tq
    r = tq // WINDOW
    gw = B_GROUP * HEAD_DIM
    cur = lambda b, kv, i: (kv, b * nq + i, 0)
    prev = lambda b, kv, i: (kv, jnp.maximum(b * nq * r + i * r - 1, 0), 0)
    omap = lambda b, kv, i: (b * nq + i, kv)
    return pl.pallas_call(
        _swa_kernel,
        grid=(batch, B_KV_HEADS, nq),
        in_specs=[
            pl.BlockSpec((B_GROUP, tq, HEAD_DIM), cur),
            pl.BlockSpec((1, WINDOW, HEAD_DIM), prev),
            pl.BlockSpec((1, tq, HEAD_DIM), cur),
            pl.BlockSpec((1, WINDOW, 4 * HEAD_DIM), prev),
            pl.BlockSpec((1, tq, 4 * HEAD_DIM), cur),
            pl.BlockSpec((1, B_GROUP * WINDOW, LANES), lambda b, kv, i: (kv, 0, 0)),
            pl.BlockSpec((tq, gw), omap),
        ],
        out_specs=pl.BlockSpec((tq, gw), omap),
        out_shape=jax.ShapeDtypeStruct((batch * seq, B_WIDTH), jnp.bfloat16),
        compiler_params=pltpu.CompilerParams(
            dimension_semantics=("arbitrary", "arbitrary", "arbitrary"), vmem_limit_bytes=V7X_VMEM_LIMIT),
        name="swa_attn",
    )(qb, kb, kb, vb_rep, vb_rep, sink_rep, gb)


def _out_proj_kernel(ya_ref, yb_ref, wa_ref, wb_ref, x_ref, o_ref):
    acc = jnp.dot(ya_ref[...], wa_ref[...], preferred_element_type=jnp.float32)
    acc = acc + jnp.dot(yb_ref[...], wb_ref[...], preferred_element_type=jnp.float32)
    o_ref[...] = x_ref[...] + acc


def _out_proj(ya, yb, w_o, x2):
    m = x2.shape[0]
    bm = OUT_BM
    row = lambda i: (i, 0)
    return pl.pallas_call(
        _out_proj_kernel,
        grid=(m // bm,),
        in_specs=[
            pl.BlockSpec((bm, A_WIDTH), row),
            pl.BlockSpec((bm, B_WIDTH), row),
            pl.BlockSpec((A_WIDTH, D_MODEL), lambda i: (0, 0), pipeline_mode=pl.Buffered(1)),
            pl.BlockSpec((B_WIDTH, D_MODEL), lambda i: (1, 0), pipeline_mode=pl.Buffered(1)),
            pl.BlockSpec((bm, D_MODEL), row),
        ],
        out_specs=pl.BlockSpec((bm, D_MODEL), row),
        out_shape=jax.ShapeDtypeStruct((m, D_MODEL), jnp.float32),
        compiler_params=pltpu.CompilerParams(
            dimension_semantics=("arbitrary",), vmem_limit_bytes=V7X_VMEM_LIMIT),
        name="out_proj",
    )(ya, yb, w_o, w_o, x2)


def kernel(x, positions, norm_w, w_in, q_norm_a, k_norm_a, lambda_q1, lambda_k1, lambda_q2, lambda_k2,
           subln_w, q_norm_b, k_norm_b, sinks, w_out):
    batch, seq, _ = x.shape
    depth = norm_w.shape[0]
    m = batch * seq
    f32 = jnp.float32

    cos_t, sin_t = _rope_tables(positions, m)
    qk_scale = HEAD_DIM ** -0.5 * LOG2E
    tile = lambda w: jnp.tile(w.astype(f32), CHUNK // HEAD_DIM).reshape(1, CHUNK)

    x2 = x.reshape(m, D_MODEL)
    for layer in range(depth):
        lam_init = 0.8 - 0.6 * math.exp(-0.3 * layer)
        w_bf16 = w_in[layer].astype(jnp.bfloat16)
        qa, ka, va_ext, ga, qb, kb, vb_rep, gb = _in_proj(
            x2, cos_t, sin_t, norm_w[layer].reshape(1, D_MODEL), w_bf16,
            tile(q_norm_a[layer]) * qk_scale, tile(k_norm_a[layer]),
            tile(q_norm_b[layer]) * qk_scale, tile(k_norm_b[layer]))
        vec = lambda p: p[layer].astype(f32).reshape(1, HEAD_DIM)
        attn_args = (qa, ka, va_ext, ga, vec(lambda_q1), vec(lambda_k1), vec(lambda_q2), vec(lambda_k2),
                     subln_w[layer].astype(f32).reshape(1, A_VDIM))
        attn = functools.partial(_diff_attn, batch=batch, seq=seq, lam_init=lam_init)
        score_bound = (HEAD_DIM * qk_scale * jnp.max(jnp.abs(q_norm_a[layer].astype(f32)))
                       * jnp.max(jnp.abs(k_norm_a[layer].astype(f32))))
        ya = lax.cond(score_bound <= RAW_EXP_SCORE_BOUND,
                      functools.partial(attn, online_max=False),
                      functools.partial(attn, online_max=True), *attn_args)
        sink_rep = jnp.broadcast_to(
            jnp.repeat(sinks[layer].astype(f32) * LOG2E, WINDOW).reshape(B_KV_HEADS, B_GROUP * WINDOW, 1),
            (B_KV_HEADS, B_GROUP * WINDOW, LANES))
        yb = _swa_attn(qb, kb, vb_rep, gb, sink_rep, batch, seq)
        x2 = _out_proj(ya, yb, w_out[layer].astype(jnp.bfloat16), x2)
    return x2.reshape(batch, seq, D_MODEL)
```

```python
import functools
import math

import jax
import jax.numpy as jnp
from jax import lax
from jax.experimental import pallas as pl
from jax.experimental.pallas import tpu as pltpu

D_MODEL = 2048
HEAD_DIM = 64
ROPE_THETA = 10000.0
EPS = 1e-6
NEG_INF = -1e30
LOG2E = math.log2(math.e)

A_HEADS = 8
A_VDIM = 128
A_WIDTH = 1024
B_HEADS = 16
B_KV_HEADS = 2
B_GROUP = 8
B_WIDTH = 1024
WINDOW = 128
PROJ_WIDTH = 6400

OFF_QA, OFF_KA, OFF_VA, OFF_GA, OFF_QB, OFF_KB, OFF_VB, OFF_GB = 0, 1024, 2048, 3072, 4096, 5120, 5248, 5376

V7X_VMEM_LIMIT = 56 * 1024 * 1024
IN_PROJ_VMEM_LIMIT = 60 * 1024 * 1024
LANES = 128
CHUNK = 256

ROPE_ROWS = 512
PROJ_BM = 512
ATT_TQ = 2048
ATT_QS = 512
ATT_QS_DIAG = 256
OUT_BM = 512
F32_SUBLANES = 8

RAW_EXP_SCORE_BOUND = 60.0


def _rope_kernel(pos_ref, invf_ref, sign_ref, cos_ref, sin_ref):
    half = HEAD_DIM // 2
    groups = LANES // half
    ang = pos_ref[...] * invf_ref[...]
    group = lax.broadcasted_iota(jnp.int32, ang.shape, 1) // half
    for src, dst, sgn in ((jnp.cos(ang), cos_ref, None), (jnp.sin(ang), sin_ref, sign_ref[...])):
        rolled = [src] + [pltpu.roll(src, half * k, 1) for k in range(1, groups)]
        for t in range(groups):
            out = rolled[(0 - t) % groups]
            for g in range(1, groups):
                out = jnp.where(group == g, rolled[(g - t) % groups], out)
            dst[t] = out if sgn is None else out * sgn


def _rope_tables(positions, m):
    half = HEAD_DIM // 2
    f32 = jnp.float32
    groups = LANES // half
    rows = m // groups
    inv_freq = ROPE_THETA ** (-(jnp.arange(0, HEAD_DIM, 2, dtype=f32) / HEAD_DIM))
    pos_rep = jnp.repeat(positions.reshape(groups, rows).astype(f32).T, half, axis=1)
    invf = jnp.tile(inv_freq, groups).reshape(1, LANES)
    sign = jnp.tile(jnp.concatenate([-jnp.ones((half,), f32), jnp.ones((half,), f32)]), LANES // HEAD_DIM)
    row = lambda i: (i, 0)
    const = lambda i: (0, 0)
    slab = lambda i: (0, i, 0)
    cos_t, sin_t = pl.pallas_call(
        _rope_kernel,
        grid=(rows // ROPE_ROWS,),
        in_specs=[pl.BlockSpec((ROPE_ROWS, LANES), row), pl.BlockSpec((1, LANES), const),
                  pl.BlockSpec((1, LANES), const)],
        out_specs=(pl.BlockSpec((groups, ROPE_ROWS, LANES), slab), pl.BlockSpec((groups, ROPE_ROWS, LANES), slab)),
        out_shape=(jax.ShapeDtypeStruct((groups, rows, LANES), f32),) * 2,
        name="rope_table",
    )(pos_rep, invf, sign.reshape(1, LANES))
    return cos_t.reshape(m, LANES), sin_t.reshape(m, LANES)


def _in_proj_kernel(x_ref, cos_ref, sin_ref, nw_ref, w_ref,
                    qna_ref, kna_ref, qnb_ref, knb_ref,
                    qa_ref, ka_ref, va_ref, ga_ref, qb_ref, kb_ref, vb_ref, gb_ref):
    bm = x_ref.shape[0]
    x = x_ref[...]
    ms = jnp.mean(x * x, axis=-1, keepdims=True)
    h = ((x * lax.rsqrt(ms + EPS)) * nw_ref[...]).astype(jnp.bfloat16)

    cos = jnp.tile(cos_ref[...], (1, CHUNK // LANES))
    sin_signed = jnp.tile(sin_ref[...], (1, CHUNK // LANES))
    lane = lax.broadcasted_iota(jnp.int32, (bm, CHUNK), 1)
    first_half = (lane % HEAD_DIM) < (HEAD_DIM // 2)
    head_group = lane // HEAD_DIM

    def proj(c0, width=CHUNK):
        return jnp.dot(h, w_ref[:, c0:c0 + width], preferred_element_type=jnp.float32)

    def norm_rope(t, w):
        t2 = t * t
        msq = None
        for g in range(CHUNK // HEAD_DIM):
            in_g = head_group == g
            part = jnp.sum(jnp.where(in_g, t2, 0.0), axis=-1, keepdims=True) * (1.0 / HEAD_DIM)
            msq = part if msq is None else jnp.where(in_g, part, msq)
        y = t * lax.rsqrt(msq + EPS) * w
        rot = jnp.where(first_half, pltpu.roll(y, CHUNK - HEAD_DIM // 2, 1), pltpu.roll(y, HEAD_DIM // 2, 1))
        return y * cos + rot * sin_signed

    def silu(t):
        return t / (1.0 + jnp.exp(-t))

    def put_qa(t, o):
        y = norm_rope(t, qna_ref[...]).astype(jnp.bfloat16)
        first_comp = (lane % A_VDIM) < HEAD_DIM
        zero = jnp.zeros_like(y)
        qa_ref[0, :, o:o + CHUNK] = jnp.where(first_comp, y, zero)
        qa_ref[1, :, o:o + CHUNK] = jnp.where(first_comp, zero, y)

    def put_ka(t, o):
        ka_ref[:, o:o + CHUNK] = norm_rope(t, kna_ref[...]).astype(jnp.bfloat16)

    def put_va(t, o):
        vat = t.T.astype(jnp.bfloat16)
        for j in range(CHUNK // A_VDIM):
            va_ref[o // A_VDIM + j, 0] = vat[j * A_VDIM:(j + 1) * A_VDIM, :]

    def put_ga(t, o):
        ga_ref[:, o:o + CHUNK] = silu(t).astype(jnp.bfloat16)

    def put_gb(t, o):
        gb_ref[:, o:o + CHUNK] = silu(t).astype(jnp.bfloat16)

    def put_qb(t, o):
        qb = norm_rope(t, qnb_ref[...]).astype(jnp.bfloat16)
        for j in range(CHUNK // HEAD_DIM):
            qb_ref[o // HEAD_DIM + j] = qb[:, j * HEAD_DIM:(j + 1) * HEAD_DIM]

    def put_kvb(kv, o):
        kbn = norm_rope(kv, knb_ref[...]).astype(jnp.bfloat16)
        for j in range(B_KV_HEADS):
            kb_ref[j] = kbn[:, j * HEAD_DIM:(j + 1) * HEAD_DIM]
        v_lo = lane < CHUNK - HEAD_DIM
        v0 = jnp.where(v_lo, kv, pltpu.roll(kv, HEAD_DIM, 1))[:, LANES:]
        v1 = jnp.where(v_lo, pltpu.roll(kv, CHUNK - HEAD_DIM, 1), kv)[:, LANES:]
        vb_ref[0] = jnp.concatenate([v0, v0], axis=1).astype(jnp.bfloat16)
        vb_ref[1] = jnp.concatenate([v1, v1], axis=1).astype(jnp.bfloat16)

    tasks = [(OFF_KB, put_kvb, 0)]
    for c in range(A_WIDTH // CHUNK):
        o = c * CHUNK
        tasks += [(OFF_QA + o, put_qa, o), (OFF_KA + o, put_ka, o), (OFF_QB + o, put_qb, o),
                  (OFF_VA + o, put_va, o), (OFF_GA + o, put_ga, o), (OFF_GB + o, put_gb, o)]
    pending = None
    for col, put, o in tasks:
        t = proj(col)
        if pending is not None:
            pending[0](pending[1], pending[2])
        pending = (put, t, o)
    pending[0](pending[1], pending[2])


def _in_proj(x2, cos_t, sin_t, norm_w, w_bf16, qna, kna, qnb, knb):
    m = x2.shape[0]
    bm = PROJ_BM
    row = lambda i: (i, 0)
    const = lambda i: (0, 0)
    hm = lambda i: (0, i, 0)
    bf = jnp.bfloat16
    out_shape = (
        jax.ShapeDtypeStruct((2, m, A_WIDTH), bf),
        jax.ShapeDtypeStruct((m, A_WIDTH), bf),
        jax.ShapeDtypeStruct((A_HEADS, m // ATT_TQ, A_VDIM, ATT_TQ), bf),
        jax.ShapeDtypeStruct((m, A_WIDTH), bf),
        jax.ShapeDtypeStruct((B_HEADS, m, HEAD_DIM), bf),
        jax.ShapeDtypeStruct((B_KV_HEADS, m, HEAD_DIM), bf),
        jax.ShapeDtypeStruct((B_KV_HEADS, m, 4 * HEAD_DIM), bf),
        jax.ShapeDtypeStruct((m, B_WIDTH), bf),
    )
    out_specs = (
        pl.BlockSpec((2, bm, A_WIDTH), hm), pl.BlockSpec((bm, A_WIDTH), row),
        pl.BlockSpec((A_HEADS, 1, A_VDIM, bm), lambda i: (0, i // (ATT_TQ // bm), 0, i % (ATT_TQ // bm))),
        pl.BlockSpec((bm, A_WIDTH), row),
        pl.BlockSpec((B_HEADS, bm, HEAD_DIM), hm),
        pl.BlockSpec((B_KV_HEADS, bm, HEAD_DIM), hm),
        pl.BlockSpec((B_KV_HEADS, bm, 4 * HEAD_DIM), hm),
        pl.BlockSpec((bm, B_WIDTH), row),
    )
    in_specs = [
        pl.BlockSpec((bm, D_MODEL), row),
        pl.BlockSpec((bm, LANES), row),
        pl.BlockSpec((bm, LANES), row),
        pl.BlockSpec((1, D_MODEL), const),
        pl.BlockSpec((D_MODEL, PROJ_WIDTH), const, pipeline_mode=pl.Buffered(1)),
        pl.BlockSpec((1, CHUNK), const), pl.BlockSpec((1, CHUNK), const),
        pl.BlockSpec((1, CHUNK), const), pl.BlockSpec((1, CHUNK), const),
    ]
    return pl.pallas_call(
        _in_proj_kernel,
        grid=(m // bm,),
        in_specs=in_specs,
        out_specs=out_specs,
        out_shape=out_shape,
        compiler_params=pltpu.CompilerParams(
            dimension_semantics=("arbitrary",), vmem_limit_bytes=IN_PROJ_VMEM_LIMIT),
        name="in_proj",
    )(x2, cos_t, sin_t, norm_w, w_bf16, qna, kna, qnb, knb)


def _diff_attn_kernel(q_ref, k_ref, vt_ref, g_ref, lq1_ref, lk1_ref, lq2_ref, lk2_ref, sw_ref,
                      o_ref, acc_ref, l_ref, m_ref, *, lam_init, online_max):
    tq = q_ref.shape[1]
    qi = pl.program_id(2)

    acc_ref[...] = jnp.zeros(acc_ref.shape, jnp.float32)
    l_ref[...] = jnp.zeros(l_ref.shape, jnp.float32)
    if online_max:
        m_ref[...] = jnp.full(m_ref.shape, NEG_INF, jnp.float32)

    def scores(c, cols, j, nk, masked):
        k = k_ref[pl.ds(pl.multiple_of(j * tq, tq), nk), :]
        s = lax.dot_general(k, q_ref[c, cols, :], (((1,), (1,)), ((), ())), preferred_element_type=jnp.float32)
        if masked:
            key = lax.broadcasted_iota(jnp.int32, s.shape, 0)
            qry = lax.broadcasted_iota(jnp.int32, s.shape, 1)
            s = jnp.where(key <= qry + (nk - s.shape[1]), s, NEG_INF)
        return s

    def accumulate(s, c, cols, j):
        nk, nq_unit = s.shape
        vt = vt_ref[0, j, :, 0:nk]
        if online_max:
            m_old = m_ref[c, :, cols]
            m_new = jnp.maximum(m_old, jnp.max(s, axis=0, keepdims=True))
            alpha = jnp.exp2(m_old - m_new)
            m_ref[c, :, cols] = m_new
            p = jnp.exp2(s - m_new)
            pv = jnp.dot(vt, p.astype(jnp.bfloat16), preferred_element_type=jnp.float32)
            acc_ref[c, :, cols] = acc_ref[c, :, cols] * alpha + pv
            l_ref[c, :, cols] = (l_ref[c, :, cols] * alpha
                                 + p.reshape(nk // F32_SUBLANES, F32_SUBLANES, nq_unit).sum(axis=0))
        else:
            p = jnp.exp2(s)
            pv = jnp.dot(vt, p.astype(jnp.bfloat16), preferred_element_type=jnp.float32)
            acc_ref[c, :, cols] += pv
            l_ref[c, :, cols] += p.reshape(nk // F32_SUBLANES, F32_SUBLANES, nq_unit).sum(axis=0)

    def sweep(j, width, masked):
        units = [(c, slice(r * width, (r + 1) * width)) for r in range(tq // width) for c in range(2)]
        nk_of = (lambda cols: cols.stop) if masked else (lambda cols: tq)
        s_next = scores(*units[0], j, nk_of(units[0][1]), masked)
        for idx, (c, cols) in enumerate(units):
            s = s_next
            if idx + 1 < len(units):
                cn, colsn = units[idx + 1]
                s_next = scores(cn, colsn, j, nk_of(colsn), masked)
            accumulate(s, c, cols, j)

    def body(j, carry):
        sweep(j, ATT_QS, False)
        return carry

    lax.fori_loop(0, qi, body, 0)
    sweep(qi, ATT_QS_DIAG, True)

    lam = (jnp.exp(jnp.sum(lq1_ref[...] * lk1_ref[...], axis=-1, keepdims=True))
           - jnp.exp(jnp.sum(lq2_ref[...] * lk2_ref[...], axis=-1, keepdims=True)) + lam_init)
    l1 = jnp.sum(l_ref[0], axis=0, keepdims=True)
    l2 = jnp.sum(l_ref[1], axis=0, keepdims=True)
    ot = acc_ref[0] / l1 - lam * (acc_ref[1] / l2)
    o = ot.T
    o = o * lax.rsqrt(jnp.mean(o * o, axis=-1, keepdims=True) + EPS) * sw_ref[...] * (1.0 - lam_init)
    o_ref[...] = (o * g_ref[...].astype(jnp.float32)).astype(o_ref.dtype)


def _diff_attn(qa, ka, va_t, ga, lq1, lk1, lq2, lk2, subln_w, *, batch, seq, lam_init, online_max):
    tq = ATT_TQ
    nq = seq // tq
    qmap = lambda b, h, i: (b * nq + i, h)
    const = lambda b, h, i: (0, 0)
    vec = pl.BlockSpec((1, HEAD_DIM), const)
    return pl.pallas_call(
        functools.partial(_diff_attn_kernel, lam_init=lam_init, online_max=online_max),
        grid=(batch, A_HEADS, nq),
        in_specs=[
            pl.BlockSpec((2, tq, A_VDIM), lambda b, h, i: (0, b * nq + i, h)),
            pl.BlockSpec((seq, A_VDIM), lambda b, h, i: (b, h)),
            pl.BlockSpec((1, nq, A_VDIM, tq), lambda b, h, i: (h, b, 0, 0)),
            pl.BlockSpec((tq, A_VDIM), qmap),
            vec, vec, vec, vec,
            pl.BlockSpec((1, A_VDIM), const),
        ],
        out_specs=pl.BlockSpec((tq, A_VDIM), qmap),
        out_shape=jax.ShapeDtypeStruct((batch * seq, A_WIDTH), jnp.bfloat16),
        scratch_shapes=[
            pltpu.VMEM((2, A_VDIM, tq), jnp.float32),
            pltpu.VMEM((2, F32_SUBLANES, tq), jnp.float32),
            pltpu.VMEM((2, 1, tq), jnp.float32),
        ],
        compiler_params=pltpu.CompilerParams(
            dimension_semantics=("arbitrary", "arbitrary", "arbitrary"), vmem_limit_bytes=V7X_VMEM_LIMIT),
        name="diff_attn_online" if online_max else "diff_attn",
    )(qa, ka, va_t, ga, lq1, lk1, lq2, lk2, subln_w)


def _swa_block(q, k, v, sink_ref, kv, prev_bias):
    blk = WINDOW
    per_half = CHUNK // HEAD_DIM
    own = (lax.broadcasted_iota(jnp.int32, (blk, blk), 1) <= lax.broadcasted_iota(jnp.int32, (blk, blk), 0))
    vgroup = lax.broadcasted_iota(jnp.int32, (2 * blk, CHUNK), 1) // HEAD_DIM
    ogroup = lax.broadcasted_iota(jnp.int32, (blk, CHUNK), 1) // HEAD_DIM
    zero = jnp.zeros((blk, blk), jnp.bfloat16)
    s_all = lax.dot_general(q, k, (((1,), (1,)), ((), ())), preferred_element_type=jnp.float32)
    vstack = jnp.concatenate([jnp.where(vgroup == j, v, jnp.zeros_like(v)) for j in range(per_half)], axis=0)
    halves = []
    for hf in range(B_GROUP // per_half):
        probs = []
        scale = None
        for j in range(per_half):
            g = hf * per_half + j
            sink = sink_ref[kv, g * blk:(g + 1) * blk, :]
            s_prev = s_all[g * blk:(g + 1) * blk, :blk]
            if prev_bias is not None:
                s_prev = s_prev + prev_bias
            s = jnp.where(own, s_all[g * blk:(g + 1) * blk, blk:], s_prev)
            m = jnp.maximum(sink, jnp.max(s, axis=-1, keepdims=True))
            p = jnp.exp2(s - m)
            denom = jnp.exp2(sink - m) + jnp.sum(p, axis=-1, keepdims=True)
            pb = p.astype(jnp.bfloat16)
            probs += [jnp.where(own, zero, pb), jnp.where(own, pb, zero)]
            rg = jnp.tile(1.0 / denom, (1, 2))
            scale = rg if scale is None else jnp.where(ogroup == j, rg, scale)
        acc = jnp.dot(jnp.concatenate(probs, axis=1), vstack, preferred_element_type=jnp.float32)
        halves.append(acc * scale)
    return jnp.concatenate(halves, axis=1)


def _out_swa_kernel(ya_ref, q_ref, kp_ref, kc_ref, vp_ref, vc_ref, sink_ref, g_ref, w_ref, x_ref, o_ref, yb_ref,
                    *, steps_per_seq):
    bm = x_ref.shape[0]
    blk = WINDOW
    nblk = bm // blk
    gw = B_GROUP * HEAD_DIM
    has_prev = (pl.program_id(0) % steps_per_seq) > 0
    first_bias = jnp.where(has_prev, 0.0, NEG_INF)

    def swa(kv, n):
        q = q_ref[kv * B_GROUP:(kv + 1) * B_GROUP, n * blk:(n + 1) * blk, :].reshape(B_GROUP * blk, HEAD_DIM)
        if n == 0:
            k = jnp.concatenate([kp_ref[kv], kc_ref[kv, 0:blk, :]], axis=0)
            v = jnp.concatenate([vp_ref[kv], vc_ref[kv, 0:blk, :]], axis=0)
            bias = first_bias
        else:
            k = kc_ref[kv, (n - 1) * blk:(n + 1) * blk, :]
            v = vc_ref[kv, (n - 1) * blk:(n + 1) * blk, :]
            bias = None
        o = _swa_block(q, k, v, sink_ref, kv, bias)
        gate = g_ref[n * blk:(n + 1) * blk, kv * gw:(kv + 1) * gw].astype(jnp.float32)
        yb_ref[n * blk:(n + 1) * blk, kv * gw:(kv + 1) * gw] = (o * gate).astype(yb_ref.dtype)

    width = 2 * CHUNK
    assert nblk == D_MODEL // width

    def project(lhs, rows, first):
        def slab(idx):
            cs = slice(idx * width, (idx + 1) * width)
            part = jnp.dot(lhs, w_ref[rows, cs], preferred_element_type=jnp.float32)
            if first:
                o_ref[:, cs] = x_ref[:, cs] + part
            else:
                o_ref[:, cs] += part
        return slab

    slab_a = project(ya_ref[...], slice(0, A_WIDTH), True)
    for n in range(nblk):
        slab_a(n)
        swa(0, n)
    for kv in range(B_KV_HEADS):
        rows = slice(A_WIDTH + kv * gw, A_WIDTH + (kv + 1) * gw)
        slab_b = project(yb_ref[:, kv * gw:(kv + 1) * gw], rows, False)
        for n in range(nblk):
            slab_b(n)
            if kv + 1 < B_KV_HEADS:
                swa(kv + 1, n)


def _out_swa(ya, qb, kb, vb_rep, gb, sink_rep, w_o, x2, seq):
    m = x2.shape[0]
    bm = OUT_BM
    r = bm // WINDOW
    row = lambda i: (i, 0)
    cur = lambda i: (0, i, 0)
    prev = lambda i: (0, jnp.maximum(i * r - 1, 0), 0)
    return pl.pallas_call(
        functools.partial(_out_swa_kernel, steps_per_seq=seq // bm),
        grid=(m // bm,),
        in_specs=[
            pl.BlockSpec((bm, A_WIDTH), row),
            pl.BlockSpec((B_HEADS, bm, HEAD_DIM), cur),
            pl.BlockSpec((B_KV_HEADS, WINDOW, HEAD_DIM), prev),
            pl.BlockSpec((B_KV_HEADS, bm, HEAD_DIM), cur),
            pl.BlockSpec((B_KV_HEADS, WINDOW, CHUNK), prev),
            pl.BlockSpec((B_KV_HEADS, bm, CHUNK), cur),
            pl.BlockSpec((B_KV_HEADS, B_GROUP * WINDOW, LANES), lambda i: (0, 0, 0)),
            pl.BlockSpec((bm, B_WIDTH), row),
            pl.BlockSpec((A_WIDTH + B_WIDTH, D_MODEL), lambda i: (0, 0), pipeline_mode=pl.Buffered(1)),
            pl.BlockSpec((bm, D_MODEL), row),
        ],
        out_specs=pl.BlockSpec((bm, D_MODEL), row),
        out_shape=jax.ShapeDtypeStruct((m, D_MODEL), jnp.float32),
        scratch_shapes=[pltpu.VMEM((bm, B_WIDTH), jnp.bfloat16)],
        compiler_params=pltpu.CompilerParams(
            dimension_semantics=("arbitrary",), vmem_limit_bytes=V7X_VMEM_LIMIT),
        name="out_swa",
    )(ya, qb, kb, kb, vb_rep, vb_rep, sink_rep, gb, w_o, x2)


def kernel(x, positions, norm_w, w_in, q_norm_a, k_norm_a, lambda_q1, lambda_k1, lambda_q2, lambda_k2,
           subln_w, q_norm_b, k_norm_b, sinks, w_out):
    batch, seq, _ = x.shape
    depth = norm_w.shape[0]
    m = batch * seq
    f32 = jnp.float32

    cos_t, sin_t = _rope_tables(positions, m)
    qk_scale = HEAD_DIM ** -0.5 * LOG2E
    tile = lambda w: jnp.tile(w.astype(f32), CHUNK // HEAD_DIM).reshape(1, CHUNK)

    x2 = x.reshape(m, D_MODEL)
    for layer in range(depth):
        lam_init = 0.8 - 0.6 * math.exp(-0.3 * layer)
        w_bf16 = w_in[layer].astype(jnp.bfloat16)
        qa, ka, va_t, ga, qb, kb, vb_rep, gb = _in_proj(
            x2, cos_t, sin_t, norm_w[layer].reshape(1, D_MODEL), w_bf16,
            tile(q_norm_a[layer]) * qk_scale, tile(k_norm_a[layer]),
            tile(q_norm_b[layer]) * qk_scale, tile(k_norm_b[layer]))
        vec = lambda p: p[layer].astype(f32).reshape(1, HEAD_DIM)
        attn_args = (qa, ka, va_t, ga, vec(lambda_q1), vec(lambda_k1), vec(lambda_q2), vec(lambda_k2),
                     subln_w[layer].astype(f32).reshape(1, A_VDIM))
        attn = functools.partial(_diff_attn, batch=batch, seq=seq, lam_init=lam_init)
        score_bound = (HEAD_DIM * qk_scale * jnp.max(jnp.abs(q_norm_a[layer].astype(f32)))
                       * jnp.max(jnp.abs(k_norm_a[layer].astype(f32))))
        ya = lax.cond(score_bound <= RAW_EXP_SCORE_BOUND,
                      functools.partial(attn, online_max=False),
                      functools.partial(attn, online_max=True), *attn_args)
        sink_rep = jnp.broadcast_to(
            jnp.repeat(sinks[layer].astype(f32) * LOG2E, WINDOW).reshape(B_KV_HEADS, B_GROUP * WINDOW, 1),
            (B_KV_HEADS, B_GROUP * WINDOW, LANES))
        x2 = _out_swa(ya, qb, kb, vb_rep, gb, sink_rep, w_out[layer].astype(jnp.bfloat16), x2, seq)
    return x2.reshape(batch, seq, D_MODEL)
```

```python
import functools
import math

import jax
import jax.numpy as jnp
from jax import lax
from jax.experimental import pallas as pl
from jax.experimental.pallas import tpu as pltpu

D_MODEL = 2048
HEAD_DIM = 64
ROPE_THETA = 10000.0
EPS = 1e-6
NEG_INF = -1e30
LOG2E = math.log2(math.e)

A_HEADS = 8
A_VDIM = 128
A_WIDTH = 1024
B_HEADS = 16
B_KV_HEADS = 2
B_GROUP = 8
B_WIDTH = 1024
WINDOW = 128
PROJ_WIDTH = 6400

OFF_QA, OFF_KA, OFF_VA, OFF_GA, OFF_QB, OFF_KB, OFF_VB, OFF_GB = 0, 1024, 2048, 3072, 4096, 5120, 5248, 5376

V7X_VMEM_LIMIT = 56 * 1024 * 1024
IN_PROJ_VMEM_LIMIT = 60 * 1024 * 1024
LANES = 128
CHUNK = 256

ROPE_ROWS = 512
PROJ_BM = 512
ATT_TQ = 2048
ATT_QS = 512
ATT_QS_DIAG = 256
OUT_BM = 512
F32_SUBLANES = 8

RAW_EXP_SCORE_BOUND = 60.0


def _rope_kernel(pos_ref, invf_ref, sign_ref, cos_ref, sin_ref):
    half = HEAD_DIM // 2
    groups = LANES // half
    ang = pos_ref[...] * invf_ref[...]
    group = lax.broadcasted_iota(jnp.int32, ang.shape, 1) // half
    for src, dst, sgn in ((jnp.cos(ang), cos_ref, None), (jnp.sin(ang), sin_ref, sign_ref[...])):
        rolled = [src] + [pltpu.roll(src, half * k, 1) for k in range(1, groups)]
        for t in range(groups):
            out = rolled[(0 - t) % groups]
            for g in range(1, groups):
                out = jnp.where(group == g, rolled[(g - t) % groups], out)
            dst[t] = out if sgn is None else out * sgn


def _rope_tables(positions, m):
    half = HEAD_DIM // 2
    f32 = jnp.float32
    groups = LANES // half
    rows = m // groups
    inv_freq = ROPE_THETA ** (-(jnp.arange(0, HEAD_DIM, 2, dtype=f32) / HEAD_DIM))
    pos_rep = jnp.repeat(positions.reshape(groups, rows).astype(f32).T, half, axis=1)
    invf = jnp.tile(inv_freq, groups).reshape(1, LANES)
    sign = jnp.tile(jnp.concatenate([-jnp.ones((half,), f32), jnp.ones((half,), f32)]), LANES // HEAD_DIM)
    row = lambda i: (i, 0)
    const = lambda i: (0, 0)
    slab = lambda i: (0, i, 0)
    cos_t, sin_t = pl.pallas_call(
        _rope_kernel,
        grid=(rows // ROPE_ROWS,),
        in_specs=[pl.BlockSpec((ROPE_ROWS, LANES), row), pl.BlockSpec((1, LANES), const),
                  pl.BlockSpec((1, LANES), const)],
        out_specs=(pl.BlockSpec((groups, ROPE_ROWS, LANES), slab), pl.BlockSpec((groups, ROPE_ROWS, LANES), slab)),
        out_shape=(jax.ShapeDtypeStruct((groups, rows, LANES), f32),) * 2,
        name="rope_table",
    )(pos_rep, invf, sign.reshape(1, LANES))
    return cos_t.reshape(m, LANES), sin_t.reshape(m, LANES)


def _in_proj_kernel(x_ref, cos_ref, sin_ref, nw_ref, w_ref,
                    qna_ref, kna_ref, qnb_ref, knb_ref, vscale_ref,
                    qa_ref, ka_ref, va_ref, ga_ref, qb_ref, kb_ref, vb_ref, gb_ref):
    bm = x_ref.shape[0]
    x = x_ref[...]
    ms = jnp.mean(x * x, axis=-1, keepdims=True)
    h = ((x * lax.rsqrt(ms + EPS)) * nw_ref[...]).astype(jnp.bfloat16)

    cos = jnp.tile(cos_ref[...], (1, CHUNK // LANES))
    sin_signed = jnp.tile(sin_ref[...], (1, CHUNK // LANES))
    lane = lax.broadcasted_iota(jnp.int32, (bm, CHUNK), 1)
    first_half = (lane % HEAD_DIM) < (HEAD_DIM // 2)
    head_group = lane // HEAD_DIM

    def proj(c0, width=CHUNK):
        return jnp.dot(h, w_ref[:, c0:c0 + width], preferred_element_type=jnp.float32)

    def norm_rope(t, w):
        t2 = t * t
        msq = None
        for g in range(CHUNK // HEAD_DIM):
            in_g = head_group == g
            part = jnp.sum(jnp.where(in_g, t2, 0.0), axis=-1, keepdims=True) * (1.0 / HEAD_DIM)
            msq = part if msq is None else jnp.where(in_g, part, msq)
        y = t * lax.rsqrt(msq + EPS) * w
        rot = jnp.where(first_half, pltpu.roll(y, CHUNK - HEAD_DIM // 2, 1), pltpu.roll(y, HEAD_DIM // 2, 1))
        return y * cos + rot * sin_signed

    def silu(t):
        return t / (1.0 + jnp.exp(-t))

    def put_qa(t, o):
        y = norm_rope(t, qna_ref[...]).astype(jnp.bfloat16)
        first_comp = (lane % A_VDIM) < HEAD_DIM
        zero = jnp.zeros_like(y)
        qa_ref[0, :, o:o + CHUNK] = jnp.where(first_comp, y, zero)
        qa_ref[1, :, o:o + CHUNK] = jnp.where(first_comp, zero, y)

    def put_ka(t, o):
        ka_ref[:, o:o + CHUNK] = norm_rope(t, kna_ref[...]).astype(jnp.bfloat16)

    def put_va(t, o):
        vat = (t * vscale_ref[...]).T.astype(jnp.bfloat16)
        for j in range(CHUNK // A_VDIM):
            va_ref[o // A_VDIM + j, 0] = vat[j * A_VDIM:(j + 1) * A_VDIM, :]

    def put_ga(t, o):
        ga_ref[:, o:o + CHUNK] = silu(t).astype(jnp.bfloat16)

    def put_gb(t, o):
        gb_ref[:, o:o + CHUNK] = silu(t).astype(jnp.bfloat16)

    def put_qb(t, o):
        qb = norm_rope(t, qnb_ref[...]).astype(jnp.bfloat16)
        for j in range(CHUNK // HEAD_DIM):
            qb_ref[o // HEAD_DIM + j] = qb[:, j * HEAD_DIM:(j + 1) * HEAD_DIM]

    def put_kvb(kv, o):
        kbn = norm_rope(kv, knb_ref[...]).astype(jnp.bfloat16)
        for j in range(B_KV_HEADS):
            kb_ref[j] = kbn[:, j * HEAD_DIM:(j + 1) * HEAD_DIM]
        v_lo = lane < CHUNK - HEAD_DIM
        v0 = jnp.where(v_lo, kv, pltpu.roll(kv, HEAD_DIM, 1))[:, LANES:]
        v1 = jnp.where(v_lo, pltpu.roll(kv, CHUNK - HEAD_DIM, 1), kv)[:, LANES:]
        vb_ref[0] = jnp.concatenate([v0, v0], axis=1).astype(jnp.bfloat16)
        vb_ref[1] = jnp.concatenate([v1, v1], axis=1).astype(jnp.bfloat16)

    tasks = [(OFF_KB, put_kvb, 0)]
    for c in range(A_WIDTH // CHUNK):
        o = c * CHUNK
        tasks += [(OFF_QA + o, put_qa, o), (OFF_KA + o, put_ka, o), (OFF_QB + o, put_qb, o),
                  (OFF_VA + o, put_va, o), (OFF_GA + o, put_ga, o), (OFF_GB + o, put_gb, o)]
    pending = None
    for col, put, o in tasks:
        t = proj(col)
        if pending is not None:
            pending[0](pending[1], pending[2])
        pending = (put, t, o)
    pending[0](pending[1], pending[2])


def _in_proj(x2, cos_t, sin_t, norm_w, w_bf16, qna, kna, qnb, knb, vscale):
    m = x2.shape[0]
    bm = PROJ_BM
    row = lambda i: (i, 0)
    const = lambda i: (0, 0)
    hm = lambda i: (0, i, 0)
    bf = jnp.bfloat16
    out_shape = (
        jax.ShapeDtypeStruct((2, m, A_WIDTH), bf),
        jax.ShapeDtypeStruct((m, A_WIDTH), bf),
        jax.ShapeDtypeStruct((A_HEADS, m // ATT_TQ, A_VDIM, ATT_TQ), bf),
        jax.ShapeDtypeStruct((m, A_WIDTH), bf),
        jax.ShapeDtypeStruct((B_HEADS, m, HEAD_DIM), bf),
        jax.ShapeDtypeStruct((B_KV_HEADS, m, HEAD_DIM), bf),
        jax.ShapeDtypeStruct((B_KV_HEADS, m, 4 * HEAD_DIM), bf),
        jax.ShapeDtypeStruct((m, B_WIDTH), bf),
    )
    out_specs = (
        pl.BlockSpec((2, bm, A_WIDTH), hm), pl.BlockSpec((bm, A_WIDTH), row),
        pl.BlockSpec((A_HEADS, 1, A_VDIM, bm), lambda i: (0, i // (ATT_TQ // bm), 0, i % (ATT_TQ // bm))),
        pl.BlockSpec((bm, A_WIDTH), row),
        pl.BlockSpec((B_HEADS, bm, HEAD_DIM), hm),
        pl.BlockSpec((B_KV_HEADS, bm, HEAD_DIM), hm),
        pl.BlockSpec((B_KV_HEADS, bm, 4 * HEAD_DIM), hm),
        pl.BlockSpec((bm, B_WIDTH), row),
    )
    in_specs = [
        pl.BlockSpec((bm, D_MODEL), row),
        pl.BlockSpec((bm, LANES), row),
        pl.BlockSpec((bm, LANES), row),
        pl.BlockSpec((1, D_MODEL), const),
        pl.BlockSpec((D_MODEL, PROJ_WIDTH), const, pipeline_mode=pl.Buffered(1)),
        pl.BlockSpec((1, CHUNK), const), pl.BlockSpec((1, CHUNK), const),
        pl.BlockSpec((1, CHUNK), const), pl.BlockSpec((1, CHUNK), const),
        pl.BlockSpec((1, 1), const),
    ]
    return pl.pallas_call(
        _in_proj_kernel,
        grid=(m // bm,),
        in_specs=in_specs,
        out_specs=out_specs,
        out_shape=out_shape,
        compiler_params=pltpu.CompilerParams(
            dimension_semantics=("arbitrary",), vmem_limit_bytes=IN_PROJ_VMEM_LIMIT),
        name="in_proj",
    )(x2, cos_t, sin_t, norm_w, w_bf16, qna, kna, qnb, knb, vscale)


def _diff_attn_kernel(q_ref, k_ref, vt_ref, g_ref, lq1_ref, lk1_ref, lq2_ref, lk2_ref, sw_ref, unscale_ref,
                      o_ref, acc_ref, l_ref, m_ref, *, lam_init, online_max):
    tq = q_ref.shape[1]
    qi = pl.program_id(2)

    acc_ref[...] = jnp.zeros(acc_ref.shape, jnp.float32)
    l_ref[...] = jnp.zeros(l_ref.shape, jnp.float32)
    if online_max:
        m_ref[...] = jnp.full(m_ref.shape, NEG_INF, jnp.float32)

    def scores(c, cols, j, nk, masked):
        k = k_ref[pl.ds(pl.multiple_of(j * tq, tq), nk), :]
        s = lax.dot_general(k, q_ref[c, cols, :], (((1,), (1,)), ((), ())), preferred_element_type=jnp.float32)
        if masked:
            key = lax.broadcasted_iota(jnp.int32, s.shape, 0)
            qry = lax.broadcasted_iota(jnp.int32, s.shape, 1)
            s = jnp.where(key <= qry + (nk - s.shape[1]), s, NEG_INF)
        return s

    def accumulate(s, c, cols, j):
        nk, nq_unit = s.shape
        vt = vt_ref[0, j, :, 0:nk]
        if online_max:
            m_old = m_ref[c, :, cols]
            m_new = jnp.maximum(m_old, jnp.max(s, axis=0, keepdims=True))
            alpha = jnp.exp2(m_old - m_new)
            m_ref[c, :, cols] = m_new
            p = jnp.exp2(s - m_new)
            pv = jnp.dot(vt, p.astype(jnp.bfloat16), preferred_element_type=jnp.float32)
            acc_ref[c, :, cols] = acc_ref[c, :, cols] * alpha + pv
            l_ref[c, :, cols] = (l_ref[c, :, cols] * alpha
                                 + p.reshape(nk // F32_SUBLANES, F32_SUBLANES, nq_unit).sum(axis=0))
        else:
            p = jnp.exp2(s)
            pv = jnp.dot(vt, p.astype(jnp.bfloat16), preferred_element_type=jnp.float32)
            acc_ref[c, :, cols] += pv
            l_ref[c, :, cols] += p.reshape(nk // F32_SUBLANES, F32_SUBLANES, nq_unit).sum(axis=0)

    def sweep(j, width, masked):
        units = [(c, slice(r * width, (r + 1) * width)) for r in range(tq // width) for c in range(2)]
        nk_of = (lambda cols: cols.stop) if masked else (lambda cols: tq)
        s_next = scores(*units[0], j, nk_of(units[0][1]), masked)
        for idx, (c, cols) in enumerate(units):
            s = s_next
            if idx + 1 < len(units):
                cn, colsn = units[idx + 1]
                s_next = scores(cn, colsn, j, nk_of(colsn), masked)
            accumulate(s, c, cols, j)

    def body(j, carry):
        sweep(j, ATT_QS, False)
        return carry

    lax.fori_loop(0, qi, body, 0)
    sweep(qi, ATT_QS_DIAG, True)

    lam = (jnp.exp(jnp.sum(lq1_ref[...] * lk1_ref[...], axis=-1, keepdims=True))
           - jnp.exp(jnp.sum(lq2_ref[...] * lk2_ref[...], axis=-1, keepdims=True)) + lam_init)
    l1 = jnp.sum(l_ref[0], axis=0, keepdims=True)
    l2 = jnp.sum(l_ref[1], axis=0, keepdims=True)
    ot = (acc_ref[0] / l1 - lam * (acc_ref[1] / l2)) * unscale_ref[...]
    o = ot.T
    o = o * lax.rsqrt(jnp.mean(o * o, axis=-1, keepdims=True) + EPS) * sw_ref[...] * (1.0 - lam_init)
    o_ref[...] = (o * g_ref[...].astype(jnp.float32)).astype(o_ref.dtype)


def _diff_attn(qa, ka, va_t, ga, lq1, lk1, lq2, lk2, subln_w, unscale, *, batch, seq, lam_init, online_max):
    tq = ATT_TQ
    nq = seq // tq
    qmap = lambda b, h, i: (b * nq + i, h)
    const = lambda b, h, i: (0, 0)
    vec = pl.BlockSpec((1, HEAD_DIM), const)
    return pl.pallas_call(
        functools.partial(_diff_attn_kernel, lam_init=lam_init, online_max=online_max),
        grid=(batch, A_HEADS, nq),
        in_specs=[
            pl.BlockSpec((2, tq, A_VDIM), lambda b, h, i: (0, b * nq + i, h)),
            pl.BlockSpec((seq, A_VDIM), lambda b, h, i: (b, h)),
            pl.BlockSpec((1, nq, A_VDIM, tq), lambda b, h, i: (h, b, 0, 0)),
            pl.BlockSpec((tq, A_VDIM), qmap),
            vec, vec, vec, vec,
            pl.BlockSpec((1, A_VDIM), const),
            pl.BlockSpec((1, 1), const),
        ],
        out_specs=pl.BlockSpec((tq, A_VDIM), qmap),
        out_shape=jax.ShapeDtypeStruct((batch * seq, A_WIDTH), jnp.bfloat16),
        scratch_shapes=[
            pltpu.VMEM((2, A_VDIM, tq), jnp.float32),
            pltpu.VMEM((2, F32_SUBLANES, tq), jnp.float32),
            pltpu.VMEM((2, 1, tq), jnp.float32),
        ],
        compiler_params=pltpu.CompilerParams(
            dimension_semantics=("arbitrary", "arbitrary", "arbitrary"), vmem_limit_bytes=V7X_VMEM_LIMIT),
        name="diff_attn_online" if online_max else "diff_attn",
    )(qa, ka, va_t, ga, lq1, lk1, lq2, lk2, subln_w, unscale)


def _swa_block(q, k, v, sink_ref, kv, prev_bias):
    blk = WINDOW
    per_half = CHUNK // HEAD_DIM
    own = (lax.broadcasted_iota(jnp.int32, (blk, blk), 1) <= lax.broadcasted_iota(jnp.int32, (blk, blk), 0))
    vgroup = lax.broadcasted_iota(jnp.int32, (2 * blk, CHUNK), 1) // HEAD_DIM
    ogroup = lax.broadcasted_iota(jnp.int32, (blk, CHUNK), 1) // HEAD_DIM
    zero = jnp.zeros((blk, blk), jnp.bfloat16)
    s_all = lax.dot_general(q, k, (((1,), (1,)), ((), ())), preferred_element_type=jnp.float32)
    vstack = jnp.concatenate([jnp.where(vgroup == j, v, jnp.zeros_like(v)) for j in range(per_half)], axis=0)
    halves = []
    for hf in range(B_GROUP // per_half):
        probs = []
        scale = None
        for j in range(per_half):
            g = hf * per_half + j
            sink = sink_ref[kv, g * blk:(g + 1) * blk, :]
            s_prev = s_all[g * blk:(g + 1) * blk, :blk]
            if prev_bias is not None:
                s_prev = s_prev + prev_bias
            s = jnp.where(own, s_all[g * blk:(g + 1) * blk, blk:], s_prev)
            m = jnp.maximum(sink, jnp.max(s, axis=-1, keepdims=True))
            p = jnp.exp2(s - m)
            denom = jnp.exp2(sink - m) + jnp.sum(p, axis=-1, keepdims=True)
            pb = p.astype(jnp.bfloat16)
            probs += [jnp.where(own, zero, pb), jnp.where(own, pb, zero)]
            rg = jnp.tile(1.0 / denom, (1, 2))
            scale = rg if scale is None else jnp.where(ogroup == j, rg, scale)
        acc = jnp.dot(jnp.concatenate(probs, axis=1), vstack, preferred_element_type=jnp.float32)
        halves.append(acc * scale)
    return jnp.concatenate(halves, axis=1)


def _out_swa_kernel(ya_ref, q_ref, kp_ref, kc_ref, vp_ref, vc_ref, sink_ref, g_ref, w_ref, x_ref, o_ref, yb_ref,
                    *, steps_per_seq):
    bm = x_ref.shape[0]
    blk = WINDOW
    nblk = bm // blk
    gw = B_GROUP * HEAD_DIM
    has_prev = (pl.program_id(0) % steps_per_seq) > 0
    first_bias = jnp.where(has_prev, 0.0, NEG_INF)

    def swa(kv, n):
        q = q_ref[kv * B_GROUP:(kv + 1) * B_GROUP, n * blk:(n + 1) * blk, :].reshape(B_GROUP * blk, HEAD_DIM)
        if n == 0:
            k = jnp.concatenate([kp_ref[kv], kc_ref[kv, 0:blk, :]], axis=0)
            v = jnp.concatenate([vp_ref[kv], vc_ref[kv, 0:blk, :]], axis=0)
            bias = first_bias
        else:
            k = kc_ref[kv, (n - 1) * blk:(n + 1) * blk, :]
            v = vc_ref[kv, (n - 1) * blk:(n + 1) * blk, :]
            bias = None
        o = _swa_block(q, k, v, sink_ref, kv, bias)
        gate = g_ref[n * blk:(n + 1) * blk, kv * gw:(kv + 1) * gw].astype(jnp.float32)
        yb_ref[n * blk:(n + 1) * blk, kv * gw:(kv + 1) * gw] = (o * gate).astype(yb_ref.dtype)

    width = 2 * CHUNK
    assert nblk == D_MODEL // width

    def project(lhs, rows, first):
        def slab(idx):
            cs = slice(idx * width, (idx + 1) * width)
            part = jnp.dot(lhs, w_ref[rows, cs], preferred_element_type=jnp.float32)
            if first:
                o_ref[:, cs] = x_ref[:, cs] + part
            else:
                o_ref[:, cs] += part
        return slab

    slab_a = project(ya_ref[...], slice(0, A_WIDTH), True)
    for n in range(nblk):
        slab_a(n)
        swa(0, n)
    for kv in range(B_KV_HEADS):
        rows = slice(A_WIDTH + kv * gw, A_WIDTH + (kv + 1) * gw)
        slab_b = project(yb_ref[:, kv * gw:(kv + 1) * gw], rows, False)
        for n in range(nblk):
            slab_b(n)
            if kv + 1 < B_KV_HEADS:
                swa(kv + 1, n)


def _out_swa(ya, qb, kb, vb_rep, gb, sink_rep, w_o, x2, seq):
    m = x2.shape[0]
    bm = OUT_BM
    r = bm // WINDOW
    row = lambda i: (i, 0)
    cur = lambda i: (0, i, 0)
    prev = lambda i: (0, jnp.maximum(i * r - 1, 0), 0)
    return pl.pallas_call(
        functools.partial(_out_swa_kernel, steps_per_seq=seq // bm),
        grid=(m // bm,),
        in_specs=[
            pl.BlockSpec((bm, A_WIDTH), row),
            pl.BlockSpec((B_HEADS, bm, HEAD_DIM), cur),
            pl.BlockSpec((B_KV_HEADS, WINDOW, HEAD_DIM), prev),
            pl.BlockSpec((B_KV_HEADS, bm, HEAD_DIM), cur),
            pl.BlockSpec((B_KV_HEADS, WINDOW, CHUNK), prev),
            pl.BlockSpec((B_KV_HEADS, bm, CHUNK), cur),
            pl.BlockSpec((B_KV_HEADS, B_GROUP * WINDOW, LANES), lambda i: (0, 0, 0)),
            pl.BlockSpec((bm, B_WIDTH), row),
            pl.BlockSpec((A_WIDTH + B_WIDTH, D_MODEL), lambda i: (0, 0), pipeline_mode=pl.Buffered(1)),
            pl.BlockSpec((bm, D_MODEL), row),
        ],
        out_specs=pl.BlockSpec((bm, D_MODEL), row),
        out_shape=jax.ShapeDtypeStruct((m, D_MODEL), jnp.float32),
        scratch_shapes=[pltpu.VMEM((bm, B_WIDTH), jnp.bfloat16)],
        compiler_params=pltpu.CompilerParams(
            dimension_semantics=("arbitrary",), vmem_limit_bytes=V7X_VMEM_LIMIT),
        name="out_swa",
    )(ya, qb, kb, kb, vb_rep, vb_rep, sink_rep, gb, w_o, x2)


def kernel(x, positions, norm_w, w_in, q_norm_a, k_norm_a, lambda_q1, lambda_k1, lambda_q2, lambda_k2,
           subln_w, q_norm_b, k_norm_b, sinks, w_out):
    batch, seq, _ = x.shape
    depth = norm_w.shape[0]
    m = batch * seq
    f32 = jnp.float32

    cos_t, sin_t = _rope_tables(positions, m)
    qk_scale = HEAD_DIM ** -0.5 * LOG2E
    tile = lambda w: jnp.tile(w.astype(f32), CHUNK // HEAD_DIM).reshape(1, CHUNK)

    x2 = x.reshape(m, D_MODEL)
    for layer in range(depth):
        lam_init = 0.8 - 0.6 * math.exp(-0.3 * layer)
        w_bf16 = w_in[layer].astype(jnp.bfloat16)
        score_bound = (HEAD_DIM * qk_scale * jnp.max(jnp.abs(q_norm_a[layer].astype(f32)))
                       * jnp.max(jnp.abs(k_norm_a[layer].astype(f32))))
        raw_exp = score_bound <= RAW_EXP_SCORE_BOUND
        v_shift = jnp.where(raw_exp, jnp.ceil(score_bound), 0.0).astype(jnp.int32).reshape(1, 1)
        one = jnp.ones((1, 1), f32)
        qa, ka, va_t, ga, qb, kb, vb_rep, gb = _in_proj(
            x2, cos_t, sin_t, norm_w[layer].reshape(1, D_MODEL), w_bf16,
            tile(q_norm_a[layer]) * qk_scale, tile(k_norm_a[layer]),
            tile(q_norm_b[layer]) * qk_scale, tile(k_norm_b[layer]), jnp.ldexp(one, -v_shift))
        vec = lambda p: p[layer].astype(f32).reshape(1, HEAD_DIM)
        attn_args = (qa, ka, va_t, ga, vec(lambda_q1), vec(lambda_k1), vec(lambda_q2), vec(lambda_k2),
                     subln_w[layer].astype(f32).reshape(1, A_VDIM), jnp.ldexp(one, v_shift))
        attn = functools.partial(_diff_attn, batch=batch, seq=seq, lam_init=lam_init)
        ya = lax.cond(raw_exp,
                      functools.partial(attn, online_max=False),
                      functools.partial(attn, online_max=True), *attn_args)
        sink_rep = jnp.broadcast_to(
            jnp.repeat(sinks[layer].astype(f32) * LOG2E, WINDOW).reshape(B_KV_HEADS, B_GROUP * WINDOW, 1),
            (B_KV_HEADS, B_GROUP * WINDOW, LANES))
        x2 = _out_swa(ya, qb, kb, vb_rep, gb, sink_rep, w_out[layer].astype(jnp.bfloat16), x2, seq)
    return x2.reshape(batch, seq, D_MODEL)
```

```python
import functools
import math

import jax
import jax.numpy as jnp
from jax import lax
from jax.experimental import pallas as pl
from jax.experimental.pallas import tpu as pltpu

D_MODEL = 2048
HEAD_DIM = 64
ROPE_THETA = 10000.0
EPS = 1e-6
NEG_INF = -1e30
LOG2E = math.log2(math.e)

A_HEADS = 8
A_VDIM = 128
A_WIDTH = 1024
B_HEADS = 16
B_KV_HEADS = 2
B_GROUP = 8
B_WIDTH = 1024
WINDOW = 128
PROJ_WIDTH = 6400

OFF_QA, OFF_KA, OFF_VA, OFF_GA, OFF_QB, OFF_KB, OFF_VB, OFF_GB = 0, 1024, 2048, 3072, 4096, 5120, 5248, 5376

V7X_VMEM_LIMIT = 56 * 1024 * 1024
IN_PROJ_VMEM_LIMIT = 60 * 1024 * 1024
LANES = 128
CHUNK = 256

ROPE_ROWS = 512
PROJ_BM = 512
ATT_TQ = 2048
ATT_QS = 512
ATT_QS_DIAG = 256
OUT_BM = 512
F32_SUBLANES = 8

RAW_EXP_SCORE_BOUND = 60.0


def _rope_kernel(pos_ref, invf_ref, sign_ref, cos_ref, sin_ref):
    half = HEAD_DIM // 2
    groups = LANES // half
    ang = pos_ref[...] * invf_ref[...]
    group = lax.broadcasted_iota(jnp.int32, ang.shape, 1) // half
    for src, dst, sgn in ((jnp.cos(ang), cos_ref, None), (jnp.sin(ang), sin_ref, sign_ref[...])):
        rolled = [src] + [pltpu.roll(src, half * k, 1) for k in range(1, groups)]
        for t in range(groups):
            out = rolled[(0 - t) % groups]
            for g in range(1, groups):
                out = jnp.where(group == g, rolled[(g - t) % groups], out)
            dst[t] = out if sgn is None else out * sgn


def _rope_tables(positions, m):
    half = HEAD_DIM // 2
    f32 = jnp.float32
    groups = LANES // half
    rows = m // groups
    inv_freq = ROPE_THETA ** (-(jnp.arange(0, HEAD_DIM, 2, dtype=f32) / HEAD_DIM))
    pos_rep = jnp.repeat(positions.reshape(groups, rows).astype(f32).T, half, axis=1)
    invf = jnp.tile(inv_freq, groups).reshape(1, LANES)
    sign = jnp.tile(jnp.concatenate([-jnp.ones((half,), f32), jnp.ones((half,), f32)]), LANES // HEAD_DIM)
    row = lambda i: (i, 0)
    const = lambda i: (0, 0)
    slab = lambda i: (0, i, 0)
    cos_t, sin_t = pl.pallas_call(
        _rope_kernel,
        grid=(rows // ROPE_ROWS,),
        in_specs=[pl.BlockSpec((ROPE_ROWS, LANES), row), pl.BlockSpec((1, LANES), const),
                  pl.BlockSpec((1, LANES), const)],
        out_specs=(pl.BlockSpec((groups, ROPE_ROWS, LANES), slab), pl.BlockSpec((groups, ROPE_ROWS, LANES), slab)),
        out_shape=(jax.ShapeDtypeStruct((groups, rows, LANES), f32),) * 2,
        name="rope_table",
    )(pos_rep, invf, sign.reshape(1, LANES))
    return cos_t.reshape(m, LANES), sin_t.reshape(m, LANES)


def _in_proj_kernel(x_ref, cos_ref, sin_ref, nw_ref, w_ref,
                    qna_ref, kna_ref, qnb_ref, knb_ref, vscale_ref,
                    qa_ref, ka_ref, va_ref, ga_ref, qb_ref, kb_ref, vb_ref, gb_ref):
    bm = x_ref.shape[0]
    x = x_ref[...]
    ms = jnp.mean(x * x, axis=-1, keepdims=True)
    h = ((x * lax.rsqrt(ms + EPS)) * nw_ref[...]).astype(jnp.bfloat16)

    cos = jnp.tile(cos_ref[...], (1, CHUNK // LANES))
    sin_signed = jnp.tile(sin_ref[...], (1, CHUNK // LANES))
    lane = lax.broadcasted_iota(jnp.int32, (bm, CHUNK), 1)
    first_half = (lane % HEAD_DIM) < (HEAD_DIM // 2)
    head_group = lane // HEAD_DIM

    def proj(c0, width=CHUNK):
        return jnp.dot(h, w_ref[:, c0:c0 + width], preferred_element_type=jnp.float32)

    def norm_rope(t, w):
        t2 = t * t
        msq = None
        for g in range(CHUNK // HEAD_DIM):
            in_g = head_group == g
            part = jnp.sum(jnp.where(in_g, t2, 0.0), axis=-1, keepdims=True) * (1.0 / HEAD_DIM)
            msq = part if msq is None else jnp.where(in_g, part, msq)
        y = t * lax.rsqrt(msq + EPS) * w
        rot = jnp.where(first_half, pltpu.roll(y, CHUNK - HEAD_DIM // 2, 1), pltpu.roll(y, HEAD_DIM // 2, 1))
        return y * cos + rot * sin_signed

    def silu(t):
        half_t = 0.5 * t
        return half_t + half_t * jnp.tanh(half_t)

    def put_qa(t, o):
        y = norm_rope(t, qna_ref[...]).astype(jnp.bfloat16)
        first_comp = (lane % A_VDIM) < HEAD_DIM
        zero = jnp.zeros_like(y)
        qa_ref[0, :, o:o + CHUNK] = jnp.where(first_comp, y, zero)
        qa_ref[1, :, o:o + CHUNK] = jnp.where(first_comp, zero, y)

    def put_ka(t, o):
        ka_ref[:, o:o + CHUNK] = norm_rope(t, kna_ref[...]).astype(jnp.bfloat16)

    def put_va(t, o):
        vat = (t * vscale_ref[...]).T.astype(jnp.bfloat16)
        for j in range(CHUNK // A_VDIM):
            va_ref[o // A_VDIM + j, 0] = vat[j * A_VDIM:(j + 1) * A_VDIM, :]

    def put_ga(t, o):
        ga_ref[:, o:o + CHUNK] = silu(t).astype(jnp.bfloat16)

    def put_gb(t, o):
        gb_ref[:, o:o + CHUNK] = silu(t).astype(jnp.bfloat16)

    def put_qb(t, o):
        qb = norm_rope(t, qnb_ref[...]).astype(jnp.bfloat16)
        for j in range(CHUNK // HEAD_DIM):
            qb_ref[o // HEAD_DIM + j] = qb[:, j * HEAD_DIM:(j + 1) * HEAD_DIM]

    def put_kvb(kv, o):
        kbn = norm_rope(kv, knb_ref[...]).astype(jnp.bfloat16)
        for j in range(B_KV_HEADS):
            kb_ref[j] = kbn[:, j * HEAD_DIM:(j + 1) * HEAD_DIM]
        v_lo = lane < CHUNK - HEAD_DIM
        v0 = jnp.where(v_lo, kv, pltpu.roll(kv, HEAD_DIM, 1))[:, LANES:]
        v1 = jnp.where(v_lo, pltpu.roll(kv, CHUNK - HEAD_DIM, 1), kv)[:, LANES:]
        vb_ref[0] = jnp.concatenate([v0, v0], axis=1).astype(jnp.bfloat16)
        vb_ref[1] = jnp.concatenate([v1, v1], axis=1).astype(jnp.bfloat16)

    tasks = [(OFF_KB, put_kvb, 0)]
    for c in range(A_WIDTH // CHUNK):
        o = c * CHUNK
        tasks += [(OFF_QA + o, put_qa, o), (OFF_KA + o, put_ka, o), (OFF_QB + o, put_qb, o),
                  (OFF_VA + o, put_va, o), (OFF_GA + o, put_ga, o), (OFF_GB + o, put_gb, o)]
    pending = None
    for col, put, o in tasks:
        t = proj(col)
        if pending is not None:
            pending[0](pending[1], pending[2])
        pending = (put, t, o)
    pending[0](pending[1], pending[2])


def _in_proj(x2, cos_t, sin_t, norm_w, w_bf16, qna, kna, qnb, knb, vscale):
    m = x2.shape[0]
    bm = PROJ_BM
    row = lambda i: (i, 0)
    const = lambda i: (0, 0)
    hm = lambda i: (0, i, 0)
    bf = jnp.bfloat16
    out_shape = (
        jax.ShapeDtypeStruct((2, m, A_WIDTH), bf),
        jax.ShapeDtypeStruct((m, A_WIDTH), bf),
        jax.ShapeDtypeStruct((A_HEADS, m // ATT_TQ, A_VDIM, ATT_TQ), bf),
        jax.ShapeDtypeStruct((m, A_WIDTH), bf),
        jax.ShapeDtypeStruct((B_HEADS, m, HEAD_DIM), bf),
        jax.ShapeDtypeStruct((B_KV_HEADS, m, HEAD_DIM), bf),
        jax.ShapeDtypeStruct((B_KV_HEADS, m, 4 * HEAD_DIM), bf),
        jax.ShapeDtypeStruct((m, B_WIDTH), bf),
    )
    out_specs = (
        pl.BlockSpec((2, bm, A_WIDTH), hm), pl.BlockSpec((bm, A_WIDTH), row),
        pl.BlockSpec((A_HEADS, 1, A_VDIM, bm), lambda i: (0, i // (ATT_TQ // bm), 0, i % (ATT_TQ // bm))),
        pl.BlockSpec((bm, A_WIDTH), row),
        pl.BlockSpec((B_HEADS, bm, HEAD_DIM), hm),
        pl.BlockSpec((B_KV_HEADS, bm, HEAD_DIM), hm),
        pl.BlockSpec((B_KV_HEADS, bm, 4 * HEAD_DIM), hm),
        pl.BlockSpec((bm, B_WIDTH), row),
    )
    in_specs = [
        pl.BlockSpec((bm, D_MODEL), row),
        pl.BlockSpec((bm, LANES), row),
        pl.BlockSpec((bm, LANES), row),
        pl.BlockSpec((1, D_MODEL), const),
        pl.BlockSpec((D_MODEL, PROJ_WIDTH), const, pipeline_mode=pl.Buffered(1)),
        pl.BlockSpec((1, CHUNK), const), pl.BlockSpec((1, CHUNK), const),
        pl.BlockSpec((1, CHUNK), const), pl.BlockSpec((1, CHUNK), const),
        pl.BlockSpec((1, 1), const),
    ]
    return pl.pallas_call(
        _in_proj_kernel,
        grid=(m // bm,),
        in_specs=in_specs,
        out_specs=out_specs,
        out_shape=out_shape,
        compiler_params=pltpu.CompilerParams(
            dimension_semantics=("arbitrary",), vmem_limit_bytes=IN_PROJ_VMEM_LIMIT),
        name="in_proj",
    )(x2, cos_t, sin_t, norm_w, w_bf16, qna, kna, qnb, knb, vscale)


def _diff_attn_kernel(q_ref, k_ref, vt_ref, g_ref, lq1_ref, lk1_ref, lq2_ref, lk2_ref, sw_ref, unscale_ref,
                      o_ref, acc_ref, l_ref, m_ref, *, lam_init, online_max):
    tq = q_ref.shape[1]
    qi = pl.program_id(2)

    acc_ref[...] = jnp.zeros(acc_ref.shape, jnp.float32)
    l_ref[...] = jnp.zeros(l_ref.shape, jnp.float32)
    if online_max:
        m_ref[...] = jnp.full(m_ref.shape, NEG_INF, jnp.float32)

    def scores(c, cols, j, nk, masked):
        k = k_ref[pl.ds(pl.multiple_of(j * tq, tq), nk), :]
        s = lax.dot_general(k, q_ref[c, cols, :], (((1,), (1,)), ((), ())), preferred_element_type=jnp.float32)
        if masked:
            key = lax.broadcasted_iota(jnp.int32, s.shape, 0)
            qry = lax.broadcasted_iota(jnp.int32, s.shape, 1)
            s = jnp.where(key <= qry + (nk - s.shape[1]), s, NEG_INF)
        return s

    def accumulate(s, c, cols, j):
        nk, nq_unit = s.shape
        vt = vt_ref[0, j, :, 0:nk]
        if online_max:
            m_old = m_ref[c, :, cols]
            m_new = jnp.maximum(m_old, jnp.max(s, axis=0, keepdims=True))
            alpha = jnp.exp2(m_old - m_new)
            m_ref[c, :, cols] = m_new
            p = jnp.exp2(s - m_new)
            pv = jnp.dot(vt, p.astype(jnp.bfloat16), preferred_element_type=jnp.float32)
            acc_ref[c, :, cols] = acc_ref[c, :, cols] * alpha + pv
            l_ref[c, :, cols] = (l_ref[c, :, cols] * alpha
                                 + p.reshape(nk // F32_SUBLANES, F32_SUBLANES, nq_unit).sum(axis=0))
        else:
            p = jnp.exp2(s)
            pv = jnp.dot(vt, p.astype(jnp.bfloat16), preferred_element_type=jnp.float32)
            acc_ref[c, :, cols] += pv
            l_ref[c, :, cols] += p.reshape(nk // F32_SUBLANES, F32_SUBLANES, nq_unit).sum(axis=0)

    def sweep(j, width, masked):
        order = reversed(range(tq // width)) if masked else range(tq // width)
        units = [(c, slice(r * width, (r + 1) * width)) for r in order for c in range(2)]
        nk_of = (lambda cols: cols.stop) if masked else (lambda cols: tq)
        s_next = scores(*units[0], j, nk_of(units[0][1]), masked)
        for idx, (c, cols) in enumerate(units):
            s = s_next
            if idx + 1 < len(units):
                cn, colsn = units[idx + 1]
                s_next = scores(cn, colsn, j, nk_of(colsn), masked)
            accumulate(s, c, cols, j)

    def body(j, carry):
        sweep(j, ATT_QS, False)
        return carry

    lax.fori_loop(0, qi, body, 0)
    sweep(qi, ATT_QS_DIAG, True)

    lam = (jnp.exp(jnp.sum(lq1_ref[...] * lk1_ref[...], axis=-1, keepdims=True))
           - jnp.exp(jnp.sum(lq2_ref[...] * lk2_ref[...], axis=-1, keepdims=True)) + lam_init)
    l1 = jnp.sum(l_ref[0], axis=0, keepdims=True)
    l2 = jnp.sum(l_ref[1], axis=0, keepdims=True)
    ot = (acc_ref[0] / l1 - lam * (acc_ref[1] / l2)) * unscale_ref[...]
    o = ot.T
    o = o * lax.rsqrt(jnp.mean(o * o, axis=-1, keepdims=True) + EPS) * sw_ref[...] * (1.0 - lam_init)
    o_ref[...] = (o * g_ref[...].astype(jnp.float32)).astype(o_ref.dtype)


def _diff_attn(qa, ka, va_t, ga, lq1, lk1, lq2, lk2, subln_w, unscale, *, batch, seq, lam_init, online_max):
    tq = ATT_TQ
    nq = seq // tq
    qmap = lambda b, h, i: (b * nq + i, h)
    const = lambda b, h, i: (0, 0)
    vec = pl.BlockSpec((1, HEAD_DIM), const)
    return pl.pallas_call(
        functools.partial(_diff_attn_kernel, lam_init=lam_init, online_max=online_max),
        grid=(batch, A_HEADS, nq),
        in_specs=[
            pl.BlockSpec((2, tq, A_VDIM), lambda b, h, i: (0, b * nq + i, h)),
            pl.BlockSpec((seq, A_VDIM), lambda b, h, i: (b, h)),
            pl.BlockSpec((1, nq, A_VDIM, tq), lambda b, h, i: (h, b, 0, 0)),
            pl.BlockSpec((tq, A_VDIM), qmap),
            vec, vec, vec, vec,
            pl.BlockSpec((1, A_VDIM), const),
            pl.BlockSpec((1, 1), const),
        ],
        out_specs=pl.BlockSpec((tq, A_VDIM), qmap),
        out_shape=jax.ShapeDtypeStruct((batch * seq, A_WIDTH), jnp.bfloat16),
        scratch_shapes=[
            pltpu.VMEM((2, A_VDIM, tq), jnp.float32),
            pltpu.VMEM((2, F32_SUBLANES, tq), jnp.float32),
            pltpu.VMEM((2, 1, tq), jnp.float32),
        ],
        compiler_params=pltpu.CompilerParams(
            dimension_semantics=("arbitrary", "arbitrary", "arbitrary"), vmem_limit_bytes=V7X_VMEM_LIMIT),
        name="diff_attn_online" if online_max else "diff_attn",
    )(qa, ka, va_t, ga, lq1, lk1, lq2, lk2, subln_w, unscale)


def _swa_block(q, k, v, sink_ref, kv, prev_bias):
    blk = WINDOW
    per_half = CHUNK // HEAD_DIM
    own = (lax.broadcasted_iota(jnp.int32, (blk, blk), 1) <= lax.broadcasted_iota(jnp.int32, (blk, blk), 0))
    vgroup = lax.broadcasted_iota(jnp.int32, (2 * blk, CHUNK), 1) // HEAD_DIM
    ogroup = lax.broadcasted_iota(jnp.int32, (blk, CHUNK), 1) // HEAD_DIM
    zero = jnp.zeros((blk, blk), jnp.bfloat16)
    s_all = lax.dot_general(q, k, (((1,), (1,)), ((), ())), preferred_element_type=jnp.float32)
    vstack = jnp.concatenate([jnp.where(vgroup == j, v, jnp.zeros_like(v)) for j in range(per_half)], axis=0)
    halves = []
    for hf in range(B_GROUP // per_half):
        probs = []
        scale = None
        for j in range(per_half):
            g = hf * per_half + j
            sink = sink_ref[kv, g * blk:(g + 1) * blk, :]
            s_prev = s_all[g * blk:(g + 1) * blk, :blk]
            if prev_bias is not None:
                s_prev = s_prev + prev_bias
            s = jnp.where(own, s_all[g * blk:(g + 1) * blk, blk:], s_prev)
            m = jnp.maximum(sink, jnp.max(s, axis=-1, keepdims=True))
            p = jnp.exp2(s - m)
            denom = jnp.exp2(sink - m) + jnp.sum(p, axis=-1, keepdims=True)
            pb = p.astype(jnp.bfloat16)
            probs += [jnp.where(own, zero, pb), jnp.where(own, pb, zero)]
            rg = jnp.tile(1.0 / denom, (1, 2))
            scale = rg if scale is None else jnp.where(ogroup == j, rg, scale)
        acc = jnp.dot(jnp.concatenate(probs, axis=1), vstack, preferred_element_type=jnp.float32)
        halves.append(acc * scale)
    return jnp.concatenate(halves, axis=1)


def _out_swa_kernel(ya_ref, q_ref, kp_ref, kc_ref, vp_ref, vc_ref, sink_ref, g_ref, w_ref, x_ref, o_ref, yb_ref,
                    *, steps_per_seq):
    bm = x_ref.shape[0]
    blk = WINDOW
    nblk = bm // blk
    gw = B_GROUP * HEAD_DIM
    has_prev = (pl.program_id(0) % steps_per_seq) > 0
    first_bias = jnp.where(has_prev, 0.0, NEG_INF)

    def swa(kv, n):
        q = q_ref[kv * B_GROUP:(kv + 1) * B_GROUP, n * blk:(n + 1) * blk, :].reshape(B_GROUP * blk, HEAD_DIM)
        if n == 0:
            k = jnp.concatenate([kp_ref[kv], kc_ref[kv, 0:blk, :]], axis=0)
            v = jnp.concatenate([vp_ref[kv], vc_ref[kv, 0:blk, :]], axis=0)
            bias = first_bias
        else:
            k = kc_ref[kv, (n - 1) * blk:(n + 1) * blk, :]
            v = vc_ref[kv, (n - 1) * blk:(n + 1) * blk, :]
            bias = None
        o = _swa_block(q, k, v, sink_ref, kv, bias)
        gate = g_ref[n * blk:(n + 1) * blk, kv * gw:(kv + 1) * gw].astype(jnp.float32)
        yb_ref[n * blk:(n + 1) * blk, kv * gw:(kv + 1) * gw] = (o * gate).astype(yb_ref.dtype)

    width = 2 * CHUNK
    assert nblk == D_MODEL // width

    def project(lhs, rows, first):
        def slab(idx):
            cs = slice(idx * width, (idx + 1) * width)
            part = jnp.dot(lhs, w_ref[rows, cs], preferred_element_type=jnp.float32)
            if first:
                o_ref[:, cs] = x_ref[:, cs] + part
            else:
                o_ref[:, cs] += part
        return slab

    slab_a = project(ya_ref[...], slice(0, A_WIDTH), True)
    for n in range(nblk):
        slab_a(n)
        swa(0, n)
    for kv in range(B_KV_HEADS):
        rows = slice(A_WIDTH + kv * gw, A_WIDTH + (kv + 1) * gw)
        slab_b = project(yb_ref[:, kv * gw:(kv + 1) * gw], rows, False)
        for n in range(nblk):
            slab_b(n)
            if kv + 1 < B_KV_HEADS:
                swa(kv + 1, n)


def _out_swa(ya, qb, kb, vb_rep, gb, sink_rep, w_o, x2, seq):
    m = x2.shape[0]
    bm = OUT_BM
    r = bm // WINDOW
    row = lambda i: (i, 0)
    cur = lambda i: (0, i, 0)
    prev = lambda i: (0, jnp.maximum(i * r - 1, 0), 0)
    return pl.pallas_call(
        functools.partial(_out_swa_kernel, steps_per_seq=seq // bm),
        grid=(m // bm,),
        in_specs=[
            pl.BlockSpec((bm, A_WIDTH), row),
            pl.BlockSpec((B_HEADS, bm, HEAD_DIM), cur),
            pl.BlockSpec((B_KV_HEADS, WINDOW, HEAD_DIM), prev),
            pl.BlockSpec((B_KV_HEADS, bm, HEAD_DIM), cur),
            pl.BlockSpec((B_KV_HEADS, WINDOW, CHUNK), prev),
            pl.BlockSpec((B_KV_HEADS, bm, CHUNK), cur),
            pl.BlockSpec((B_KV_HEADS, B_GROUP * WINDOW, LANES), lambda i: (0, 0, 0)),
            pl.BlockSpec((bm, B_WIDTH), row),
            pl.BlockSpec((A_WIDTH + B_WIDTH, D_MODEL), lambda i: (0, 0), pipeline_mode=pl.Buffered(1)),
            pl.BlockSpec((bm, D_MODEL), row),
        ],
        out_specs=pl.BlockSpec((bm, D_MODEL), row),
        out_shape=jax.ShapeDtypeStruct((m, D_MODEL), jnp.float32),
        scratch_shapes=[pltpu.VMEM((bm, B_WIDTH), jnp.bfloat16)],
        compiler_params=pltpu.CompilerParams(
            dimension_semantics=("arbitrary",), vmem_limit_bytes=V7X_VMEM_LIMIT),
        name="out_swa",
    )(ya, qb, kb, kb, vb_rep, vb_rep, sink_rep, gb, w_o, x2)


def kernel(x, positions, norm_w, w_in, q_norm_a, k_norm_a, lambda_q1, lambda_k1, lambda_q2, lambda_k2,
           subln_w, q_norm_b, k_norm_b, sinks, w_out):
    batch, seq, _ = x.shape
    depth = norm_w.shape[0]
    m = batch * seq
    f32 = jnp.float32

    cos_t, sin_t = _rope_tables(positions, m)
    qk_scale = HEAD_DIM ** -0.5 * LOG2E
    tile = lambda w: jnp.tile(w.astype(f32), CHUNK // HEAD_DIM).reshape(1, CHUNK)

    x2 = x.reshape(m, D_MODEL)
    for layer in range(depth):
        lam_init = 0.8 - 0.6 * math.exp(-0.3 * layer)
        w_bf16 = w_in[layer].astype(jnp.bfloat16)
        score_bound = (HEAD_DIM * qk_scale * jnp.max(jnp.abs(q_norm_a[layer].astype(f32)))
                       * jnp.max(jnp.abs(k_norm_a[layer].astype(f32))))
        raw_exp = score_bound <= RAW_EXP_SCORE_BOUND
        v_shift = jnp.where(raw_exp, jnp.ceil(score_bound), 0.0).astype(jnp.int32).reshape(1, 1)
        one = jnp.ones((1, 1), f32)
        qa, ka, va_t, ga, qb, kb, vb_rep, gb = _in_proj(
            x2, cos_t, sin_t, norm_w[layer].reshape(1, D_MODEL), w_bf16,
            tile(q_norm_a[layer]) * qk_scale, tile(k_norm_a[layer]),
            tile(q_norm_b[layer]) * qk_scale, tile(k_norm_b[layer]), jnp.ldexp(one, -v_shift))
        vec = lambda p: p[layer].astype(f32).reshape(1, HEAD_DIM)
        attn_args = (qa, ka, va_t, ga, vec(lambda_q1), vec(lambda_k1), vec(lambda_q2), vec(lambda_k2),
                     subln_w[layer].astype(f32).reshape(1, A_VDIM), jnp.ldexp(one, v_shift))
        attn = functools.partial(_diff_attn, batch=batch, seq=seq, lam_init=lam_init)
        ya = lax.cond(raw_exp,
                      functools.partial(attn, online_max=False),
                      functools.partial(attn, online_max=True), *attn_args)
        sink_rep = jnp.broadcast_to(
            jnp.repeat(sinks[layer].astype(f32) * LOG2E, WINDOW).reshape(B_KV_HEADS, B_GROUP * WINDOW, 1),
            (B_KV_HEADS, B_GROUP * WINDOW, LANES))
        x2 = _out_swa(ya, qb, kb, vb_rep, gb, sink_rep, w_out[layer].astype(jnp.bfloat16), x2, seq)
    return x2.reshape(batch, seq, D_MODEL)
```

```python
import functools
import math

import jax
import jax.numpy as jnp
from jax import lax
from jax.experimental import pallas as pl
from jax.experimental.pallas import tpu as pltpu

D_MODEL = 2048
HEAD_DIM = 64
ROPE_THETA = 10000.0
EPS = 1e-6
NEG_INF = -1e30
LOG2E = math.log2(math.e)

A_HEADS = 8
A_VDIM = 128
A_WIDTH = 1024
B_HEADS = 16
B_KV_HEADS = 2
B_GROUP = 8
B_WIDTH = 1024
WINDOW = 128
PROJ_WIDTH = 6400

OFF_QA, OFF_KA, OFF_VA, OFF_GA, OFF_QB, OFF_KB, OFF_VB, OFF_GB = 0, 1024, 2048, 3072, 4096, 5120, 5248, 5376

V7X_VMEM_LIMIT = 56 * 1024 * 1024
IN_PROJ_VMEM_LIMIT = 60 * 1024 * 1024
LANES = 128
CHUNK = 256

ROPE_ROWS = 512
PROJ_BM = 512
ATT_TQ = 2048
ATT_QS = 1024
ATT_QS_DIAG = 256
OUT_BM = 512
F32_SUBLANES = 8

RAW_EXP_SCORE_BOUND = 60.0


def _rope_kernel(pos_ref, invf_ref, sign_ref, cos_ref, sin_ref):
    half = HEAD_DIM // 2
    groups = LANES // half
    ang = pos_ref[...] * invf_ref[...]
    group = lax.broadcasted_iota(jnp.int32, ang.shape, 1) // half
    for src, dst, sgn in ((jnp.cos(ang), cos_ref, None), (jnp.sin(ang), sin_ref, sign_ref[...])):
        rolled = [src] + [pltpu.roll(src, half * k, 1) for k in range(1, groups)]
        for t in range(groups):
            out = rolled[(0 - t) % groups]
            for g in range(1, groups):
                out = jnp.where(group == g, rolled[(g - t) % groups], out)
            dst[t] = out if sgn is None else out * sgn


def _rope_tables(positions, m):
    half = HEAD_DIM // 2
    f32 = jnp.float32
    groups = LANES // half
    rows = m // groups
    inv_freq = ROPE_THETA ** (-(jnp.arange(0, HEAD_DIM, 2, dtype=f32) / HEAD_DIM))
    pos_rep = jnp.repeat(positions.reshape(groups, rows).astype(f32).T, half, axis=1)
    invf = jnp.tile(inv_freq, groups).reshape(1, LANES)
    sign = jnp.tile(jnp.concatenate([-jnp.ones((half,), f32), jnp.ones((half,), f32)]), LANES // HEAD_DIM)
    row = lambda i: (i, 0)
    const = lambda i: (0, 0)
    slab = lambda i: (0, i, 0)
    cos_t, sin_t = pl.pallas_call(
        _rope_kernel,
        grid=(rows // ROPE_ROWS,),
        in_specs=[pl.BlockSpec((ROPE_ROWS, LANES), row), pl.BlockSpec((1, LANES), const),
                  pl.BlockSpec((1, LANES), const)],
        out_specs=(pl.BlockSpec((groups, ROPE_ROWS, LANES), slab), pl.BlockSpec((groups, ROPE_ROWS, LANES), slab)),
        out_shape=(jax.ShapeDtypeStruct((groups, rows, LANES), f32),) * 2,
        name="rope_table",
    )(pos_rep, invf, sign.reshape(1, LANES))
    return cos_t.reshape(m, LANES), sin_t.reshape(m, LANES)


def _in_proj_kernel(x_ref, cos_ref, sin_ref, nw_ref, w_ref,
                    qna_ref, kna_ref, qnb_ref, knb_ref, vscale_ref,
                    qa_ref, ka_ref, va_ref, ga_ref, qb_ref, kb_ref, vb_ref, gb_ref):
    bm = x_ref.shape[0]
    x = x_ref[...]
    ms = jnp.mean(x * x, axis=-1, keepdims=True)
    h = ((x * lax.rsqrt(ms + EPS)) * nw_ref[...]).astype(jnp.bfloat16)

    cos = jnp.tile(cos_ref[...], (1, CHUNK // LANES))
    sin_signed = jnp.tile(sin_ref[...], (1, CHUNK // LANES))
    lane = lax.broadcasted_iota(jnp.int32, (bm, CHUNK), 1)
    first_half = (lane % HEAD_DIM) < (HEAD_DIM // 2)
    head_group = lane // HEAD_DIM

    def proj(c0, width=CHUNK):
        return jnp.dot(h, w_ref[:, c0:c0 + width], preferred_element_type=jnp.float32)

    def norm_rope(t, w):
        t2 = t * t
        msq = None
        for g in range(CHUNK // HEAD_DIM):
            in_g = head_group == g
            part = jnp.sum(jnp.where(in_g, t2, 0.0), axis=-1, keepdims=True) * (1.0 / HEAD_DIM)
            msq = part if msq is None else jnp.where(in_g, part, msq)
        y = t * lax.rsqrt(msq + EPS) * w
        rot = jnp.where(first_half, pltpu.roll(y, CHUNK - HEAD_DIM // 2, 1), pltpu.roll(y, HEAD_DIM // 2, 1))
        return y * cos + rot * sin_signed

    def silu(t):
        half_t = 0.5 * t
        return half_t + half_t * jnp.tanh(half_t)

    def put_qa(t, o):
        y = norm_rope(t, qna_ref[...]).astype(jnp.bfloat16)
        first_comp = (lane % A_VDIM) < HEAD_DIM
        zero = jnp.zeros_like(y)
        qa_ref[0, :, o:o + CHUNK] = jnp.where(first_comp, y, zero)
        qa_ref[1, :, o:o + CHUNK] = jnp.where(first_comp, zero, y)

    def put_ka(t, o):
        ka_ref[:, o:o + CHUNK] = norm_rope(t, kna_ref[...]).astype(jnp.bfloat16)

    def put_va(t, o):
        vat = (t * vscale_ref[...]).T.astype(jnp.bfloat16)
        for j in range(CHUNK // A_VDIM):
            va_ref[o // A_VDIM + j, 0] = vat[j * A_VDIM:(j + 1) * A_VDIM, :]

    def put_ga(t, o):
        ga_ref[:, o:o + CHUNK] = silu(t).astype(jnp.bfloat16)

    def put_gb(t, o):
        gb_ref[:, o:o + CHUNK] = silu(t).astype(jnp.bfloat16)

    def put_qb(t, o):
        qb = norm_rope(t, qnb_ref[...]).astype(jnp.bfloat16)
        for j in range(CHUNK // HEAD_DIM):
            qb_ref[o // HEAD_DIM + j] = qb[:, j * HEAD_DIM:(j + 1) * HEAD_DIM]

    def put_kvb(kv, o):
        kbn = norm_rope(kv, knb_ref[...]).astype(jnp.bfloat16)
        for j in range(B_KV_HEADS):
            kb_ref[j] = kbn[:, j * HEAD_DIM:(j + 1) * HEAD_DIM]
        v_lo = lane < CHUNK - HEAD_DIM
        v0 = jnp.where(v_lo, kv, pltpu.roll(kv, HEAD_DIM, 1))[:, LANES:]
        v1 = jnp.where(v_lo, pltpu.roll(kv, CHUNK - HEAD_DIM, 1), kv)[:, LANES:]
        vb_ref[0] = jnp.concatenate([v0, v0], axis=1).astype(jnp.bfloat16)
        vb_ref[1] = jnp.concatenate([v1, v1], axis=1).astype(jnp.bfloat16)

    tasks = [(OFF_KB, put_kvb, 0)]
    for c in range(A_WIDTH // CHUNK):
        o = c * CHUNK
        tasks += [(OFF_QA + o, put_qa, o), (OFF_KA + o, put_ka, o), (OFF_QB + o, put_qb, o),
                  (OFF_VA + o, put_va, o), (OFF_GA + o, put_ga, o), (OFF_GB + o, put_gb, o)]
    pending = None
    for col, put, o in tasks:
        t = proj(col)
        if pending is not None:
            pending[0](pending[1], pending[2])
        pending = (put, t, o)
    pending[0](pending[1], pending[2])


def _in_proj(x2, cos_t, sin_t, norm_w, w_bf16, qna, kna, qnb, knb, vscale):
    m = x2.shape[0]
    bm = PROJ_BM
    row = lambda i: (i, 0)
    const = lambda i: (0, 0)
    hm = lambda i: (0, i, 0)
    bf = jnp.bfloat16
    out_shape = (
        jax.ShapeDtypeStruct((2, m, A_WIDTH), bf),
        jax.ShapeDtypeStruct((m, A_WIDTH), bf),
        jax.ShapeDtypeStruct((A_HEADS, m // ATT_TQ, A_VDIM, ATT_TQ), bf),
        jax.ShapeDtypeStruct((m, A_WIDTH), bf),
        jax.ShapeDtypeStruct((B_HEADS, m, HEAD_DIM), bf),
        jax.ShapeDtypeStruct((B_KV_HEADS, m, HEAD_DIM), bf),
        jax.ShapeDtypeStruct((B_KV_HEADS, m, 4 * HEAD_DIM), bf),
        jax.ShapeDtypeStruct((m, B_WIDTH), bf),
    )
    out_specs = (
        pl.BlockSpec((2, bm, A_WIDTH), hm), pl.BlockSpec((bm, A_WIDTH), row),
        pl.BlockSpec((A_HEADS, 1, A_VDIM, bm), lambda i: (0, i // (ATT_TQ // bm), 0, i % (ATT_TQ // bm))),
        pl.BlockSpec((bm, A_WIDTH), row),
        pl.BlockSpec((B_HEADS, bm, HEAD_DIM), hm),
        pl.BlockSpec((B_KV_HEADS, bm, HEAD_DIM), hm),
        pl.BlockSpec((B_KV_HEADS, bm, 4 * HEAD_DIM), hm),
        pl.BlockSpec((bm, B_WIDTH), row),
    )
    in_specs = [
        pl.BlockSpec((bm, D_MODEL), row),
        pl.BlockSpec((bm, LANES), row),
        pl.BlockSpec((bm, LANES), row),
        pl.BlockSpec((1, D_MODEL), const),
        pl.BlockSpec((D_MODEL, PROJ_WIDTH), const, pipeline_mode=pl.Buffered(1)),
        pl.BlockSpec((1, CHUNK), const), pl.BlockSpec((1, CHUNK), const),
        pl.BlockSpec((1, CHUNK), const), pl.BlockSpec((1, CHUNK), const),
        pl.BlockSpec((1, 1), const),
    ]
    return pl.pallas_call(
        _in_proj_kernel,
        grid=(m // bm,),
        in_specs=in_specs,
        out_specs=out_specs,
        out_shape=out_shape,
        compiler_params=pltpu.CompilerParams(
            dimension_semantics=("arbitrary",), vmem_limit_bytes=IN_PROJ_VMEM_LIMIT),
        name="in_proj",
    )(x2, cos_t, sin_t, norm_w, w_bf16, qna, kna, qnb, knb, vscale)


def _diff_attn_kernel(q_ref, k_ref, vt_ref, g_ref, lq1_ref, lk1_ref, lq2_ref, lk2_ref, sw_ref, unscale_ref,
                      o_ref, acc_ref, l_ref, m_ref, *, lam_init, online_max):
    tq = q_ref.shape[1]
    qi = pl.program_id(2)

    def reset_accumulators():
        acc_ref[...] = jnp.zeros(acc_ref.shape, jnp.float32)
        l_ref[...] = jnp.zeros(l_ref.shape, jnp.float32)
        if online_max:
            m_ref[...] = jnp.full(m_ref.shape, NEG_INF, jnp.float32)

    @pl.when((pl.program_id(0) == 0) & (pl.program_id(1) == 0) & (qi == 0))
    def _():
        reset_accumulators()

    def scores(c, cols, j, nk, masked):
        k = k_ref[pl.ds(pl.multiple_of(j * tq, tq), nk), :]
        s = lax.dot_general(k, q_ref[c, cols, :], (((1,), (1,)), ((), ())), preferred_element_type=jnp.float32)
        if masked:
            key = lax.broadcasted_iota(jnp.int32, s.shape, 0)
            qry = lax.broadcasted_iota(jnp.int32, s.shape, 1)
            s = jnp.where(key <= qry + (nk - s.shape[1]), s, NEG_INF)
        return s

    def accumulate(s, c, cols, j):
        nk, nq_unit = s.shape
        vt = vt_ref[0, j, :, 0:nk]
        if online_max:
            m_old = m_ref[c, :, cols]
            m_new = jnp.maximum(m_old, jnp.max(s, axis=0, keepdims=True))
            alpha = jnp.exp2(m_old - m_new)
            m_ref[c, :, cols] = m_new
            p = jnp.exp2(s - m_new)
            pv = jnp.dot(vt, p.astype(jnp.bfloat16), preferred_element_type=jnp.float32)
            acc_ref[c, :, cols] = acc_ref[c, :, cols] * alpha + pv
            l_ref[c, :, cols] = (l_ref[c, :, cols] * alpha
                                 + p.reshape(nk // F32_SUBLANES, F32_SUBLANES, nq_unit).sum(axis=0))
        else:
            p = jnp.exp2(s)
            pv = jnp.dot(vt, p.astype(jnp.bfloat16), preferred_element_type=jnp.float32)
            acc_ref[c, :, cols] += pv
            l_ref[c, :, cols] += p.reshape(nk // F32_SUBLANES, F32_SUBLANES, nq_unit).sum(axis=0)

    def sweep(j, width, masked):
        order = reversed(range(tq // width)) if masked else range(tq // width)
        units = [(c, slice(r * width, (r + 1) * width)) for r in order for c in range(2)]
        nk_of = (lambda cols: cols.stop) if masked else (lambda cols: tq)
        s_next = scores(*units[0], j, nk_of(units[0][1]), masked)
        for idx, (c, cols) in enumerate(units):
            s = s_next
            if idx + 1 < len(units):
                cn, colsn = units[idx + 1]
                s_next = scores(cn, colsn, j, nk_of(colsn), masked)
            accumulate(s, c, cols, j)

    def body(j, carry):
        sweep(j, ATT_QS, False)
        return carry

    lax.fori_loop(0, qi, body, 0)
    sweep(qi, ATT_QS_DIAG, True)

    lam = (jnp.exp(jnp.sum(lq1_ref[...] * lk1_ref[...], axis=-1, keepdims=True))
           - jnp.exp(jnp.sum(lq2_ref[...] * lk2_ref[...], axis=-1, keepdims=True)) + lam_init)
    l1 = jnp.sum(l_ref[0], axis=0, keepdims=True)
    l2 = jnp.sum(l_ref[1], axis=0, keepdims=True)
    ot = (acc_ref[0] / l1 - lam * (acc_ref[1] / l2)) * unscale_ref[...]
    o = ot.T
    o = o * lax.rsqrt(jnp.mean(o * o, axis=-1, keepdims=True) + EPS) * sw_ref[...] * (1.0 - lam_init)
    o_ref[...] = (o * g_ref[...].astype(jnp.float32)).astype(o_ref.dtype)
    reset_accumulators()


def _diff_attn(qa, ka, va_t, ga, lq1, lk1, lq2, lk2, subln_w, unscale, *, batch, seq, lam_init, online_max):
    tq = ATT_TQ
    nq = seq // tq
    qmap = lambda b, h, i: (b * nq + i, h)
    const = lambda b, h, i: (0, 0)
    vec = pl.BlockSpec((1, HEAD_DIM), const)
    return pl.pallas_call(
        functools.partial(_diff_attn_kernel, lam_init=lam_init, online_max=online_max),
        grid=(batch, A_HEADS, nq),
        in_specs=[
            pl.BlockSpec((2, tq, A_VDIM), lambda b, h, i: (0, b * nq + i, h)),
            pl.BlockSpec((seq, A_VDIM), lambda b, h, i: (b, h)),
            pl.BlockSpec((1, nq, A_VDIM, tq), lambda b, h, i: (h, b, 0, 0)),
            pl.BlockSpec((tq, A_VDIM), qmap),
            vec, vec, vec, vec,
            pl.BlockSpec((1, A_VDIM), const),
            pl.BlockSpec((1, 1), const),
        ],
        out_specs=pl.BlockSpec((tq, A_VDIM), qmap),
        out_shape=jax.ShapeDtypeStruct((batch * seq, A_WIDTH), jnp.bfloat16),
        scratch_shapes=[
            pltpu.VMEM((2, A_VDIM, tq), jnp.float32),
            pltpu.VMEM((2, F32_SUBLANES, tq), jnp.float32),
            pltpu.VMEM((2, 1, tq), jnp.float32),
        ],
        compiler_params=pltpu.CompilerParams(
            dimension_semantics=("arbitrary", "arbitrary", "arbitrary"), vmem_limit_bytes=V7X_VMEM_LIMIT),
        name="diff_attn_online" if online_max else "diff_attn",
    )(qa, ka, va_t, ga, lq1, lk1, lq2, lk2, subln_w, unscale)


def _swa_block(q, k, v, sink_ref, kv, prev_bias):
    blk = WINDOW
    per_half = CHUNK // HEAD_DIM
    own = (lax.broadcasted_iota(jnp.int32, (blk, blk), 1) <= lax.broadcasted_iota(jnp.int32, (blk, blk), 0))
    vgroup = lax.broadcasted_iota(jnp.int32, (2 * blk, CHUNK), 1) // HEAD_DIM
    ogroup = lax.broadcasted_iota(jnp.int32, (blk, CHUNK), 1) // HEAD_DIM
    zero = jnp.zeros((blk, blk), jnp.bfloat16)
    s_all = lax.dot_general(q, k, (((1,), (1,)), ((), ())), preferred_element_type=jnp.float32)
    vstack = jnp.concatenate([jnp.where(vgroup == j, v, jnp.zeros_like(v)) for j in range(per_half)], axis=0)
    halves = []
    for hf in range(B_GROUP // per_half):
        probs = []
        scale = None
        for j in range(per_half):
            g = hf * per_half + j
            sink = sink_ref[kv, g * blk:(g + 1) * blk, :]
            s_prev = s_all[g * blk:(g + 1) * blk, :blk]
            if prev_bias is not None:
                s_prev = s_prev + prev_bias
            s = jnp.where(own, s_all[g * blk:(g + 1) * blk, blk:], s_prev)
            m = jnp.maximum(sink, jnp.max(s, axis=-1, keepdims=True))
            p = jnp.exp2(s - m)
            denom = jnp.exp2(sink - m) + jnp.sum(p, axis=-1, keepdims=True)
            pb = p.astype(jnp.bfloat16)
            probs += [jnp.where(own, zero, pb), jnp.where(own, pb, zero)]
            rg = jnp.tile(1.0 / denom, (1, 2))
            scale = rg if scale is None else jnp.where(ogroup == j, rg, scale)
        acc = jnp.dot(jnp.concatenate(probs, axis=1), vstack, preferred_element_type=jnp.float32)
        halves.append(acc * scale)
    return jnp.concatenate(halves, axis=1)


def _out_swa_kernel(ya_ref, q_ref, kp_ref, kc_ref, vp_ref, vc_ref, sink_ref, g_ref, w_ref, x_ref, o_ref, yb_ref,
                    *, steps_per_seq):
    bm = x_ref.shape[0]
    blk = WINDOW
    nblk = bm // blk
    gw = B_GROUP * HEAD_DIM
    has_prev = (pl.program_id(0) % steps_per_seq) > 0
    first_bias = jnp.where(has_prev, 0.0, NEG_INF)

    def swa(kv, n):
        q = q_ref[kv * B_GROUP:(kv + 1) * B_GROUP, n * blk:(n + 1) * blk, :].reshape(B_GROUP * blk, HEAD_DIM)
        if n == 0:
            k = jnp.concatenate([kp_ref[kv], kc_ref[kv, 0:blk, :]], axis=0)
            v = jnp.concatenate([vp_ref[kv], vc_ref[kv, 0:blk, :]], axis=0)
            bias = first_bias
        else:
            k = kc_ref[kv, (n - 1) * blk:(n + 1) * blk, :]
            v = vc_ref[kv, (n - 1) * blk:(n + 1) * blk, :]
            bias = None
        o = _swa_block(q, k, v, sink_ref, kv, bias)
        gate = g_ref[n * blk:(n + 1) * blk, kv * gw:(kv + 1) * gw].astype(jnp.float32)
        yb_ref[n * blk:(n + 1) * blk, kv * gw:(kv + 1) * gw] = (o * gate).astype(yb_ref.dtype)

    width = 2 * CHUNK
    assert nblk == D_MODEL // width

    def project(lhs, rows, first):
        def slab(idx):
            cs = slice(idx * width, (idx + 1) * width)
            part = jnp.dot(lhs, w_ref[rows, cs], preferred_element_type=jnp.float32)
            if first:
                o_ref[:, cs] = x_ref[:, cs] + part
            else:
                o_ref[:, cs] += part
        return slab

    slab_a = project(ya_ref[...], slice(0, A_WIDTH), True)
    for n in range(nblk):
        slab_a(n)
        swa(0, n)
    for kv in range(B_KV_HEADS):
        rows = slice(A_WIDTH + kv * gw, A_WIDTH + (kv + 1) * gw)
        slab_b = project(yb_ref[:, kv * gw:(kv + 1) * gw], rows, False)
        for n in range(nblk):
            slab_b(n)
            if kv + 1 < B_KV_HEADS:
                swa(kv + 1, n)


def _out_swa(ya, qb, kb, vb_rep, gb, sink_rep, w_o, x2, seq):
    m = x2.shape[0]
    bm = OUT_BM
    r = bm // WINDOW
    row = lambda i: (i, 0)
    cur = lambda i: (0, i, 0)
    prev = lambda i: (0, jnp.maximum(i * r - 1, 0), 0)
    return pl.pallas_call(
        functools.partial(_out_swa_kernel, steps_per_seq=seq // bm),
        grid=(m // bm,),
        in_specs=[
            pl.BlockSpec((bm, A_WIDTH), row),
            pl.BlockSpec((B_HEADS, bm, HEAD_DIM), cur),
            pl.BlockSpec((B_KV_HEADS, WINDOW, HEAD_DIM), prev),
            pl.BlockSpec((B_KV_HEADS, bm, HEAD_DIM), cur),
            pl.BlockSpec((B_KV_HEADS, WINDOW, CHUNK), prev),
            pl.BlockSpec((B_KV_HEADS, bm, CHUNK), cur),
            pl.BlockSpec((B_KV_HEADS, B_GROUP * WINDOW, LANES), lambda i: (0, 0, 0)),
            pl.BlockSpec((bm, B_WIDTH), row),
            pl.BlockSpec((A_WIDTH + B_WIDTH, D_MODEL), lambda i: (0, 0), pipeline_mode=pl.Buffered(1)),
            pl.BlockSpec((bm, D_MODEL), row),
        ],
        out_specs=pl.BlockSpec((bm, D_MODEL), row),
        out_shape=jax.ShapeDtypeStruct((m, D_MODEL), jnp.float32),
        scratch_shapes=[pltpu.VMEM((bm, B_WIDTH), jnp.bfloat16)],
        compiler_params=pltpu.CompilerParams(
            dimension_semantics=("arbitrary",), vmem_limit_bytes=V7X_VMEM_LIMIT),
        name="out_swa",
    )(ya, qb, kb, kb, vb_rep, vb_rep, sink_rep, gb, w_o, x2)


def kernel(x, positions, norm_w, w_in, q_norm_a, k_norm_a, lambda_q1, lambda_k1, lambda_q2, lambda_k2,
           subln_w, q_norm_b, k_norm_b, sinks, w_out):
    batch, seq, _ = x.shape
    depth = norm_w.shape[0]
    m = batch * seq
    f32 = jnp.float32

    cos_t, sin_t = _rope_tables(positions, m)
    qk_scale = HEAD_DIM ** -0.5 * LOG2E
    tile = lambda w: jnp.tile(w.astype(f32), CHUNK // HEAD_DIM).reshape(1, CHUNK)

    x2 = x.reshape(m, D_MODEL)
    for layer in range(depth):
        lam_init = 0.8 - 0.6 * math.exp(-0.3 * layer)
        w_bf16 = w_in[layer].astype(jnp.bfloat16)
        score_bound = (HEAD_DIM * qk_scale * jnp.max(jnp.abs(q_norm_a[layer].astype(f32)))
                       * jnp.max(jnp.abs(k_norm_a[layer].astype(f32))))
        raw_exp = score_bound <= RAW_EXP_SCORE_BOUND
        v_shift = jnp.where(raw_exp, jnp.ceil(score_bound), 0.0).astype(jnp.int32).reshape(1, 1)
        one = jnp.ones((1, 1), f32)
        qa, ka, va_t, ga, qb, kb, vb_rep, gb = _in_proj(
            x2, cos_t, sin_t, norm_w[layer].reshape(1, D_MODEL), w_bf16,
            tile(q_norm_a[layer]) * qk_scale, tile(k_norm_a[layer]),
            tile(q_norm_b[layer]) * qk_scale, tile(k_norm_b[layer]), jnp.ldexp(one, -v_shift))
        vec = lambda p: p[layer].astype(f32).reshape(1, HEAD_DIM)
        attn_args = (qa, ka, va_t, ga, vec(lambda_q1), vec(lambda_k1), vec(lambda_q2), vec(lambda_k2),
                     subln_w[layer].astype(f32).reshape(1, A_VDIM), jnp.ldexp(one, v_shift))
        attn = functools.partial(_diff_attn, batch=batch, seq=seq, lam_init=lam_init)
        ya = lax.cond(raw_exp,
                      functools.partial(attn, online_max=False),
                      functools.partial(attn, online_max=True), *attn_args)
        sink_rep = jnp.broadcast_to(
            jnp.repeat(sinks[layer].astype(f32) * LOG2E, WINDOW).reshape(B_KV_HEADS, B_GROUP * WINDOW, 1),
            (B_KV_HEADS, B_GROUP * WINDOW, LANES))
        x2 = _out_swa(ya, qb, kb, vb_rep, gb, sink_rep, w_out[layer].astype(jnp.bfloat16), x2, seq)
    return x2.reshape(batch, seq, D_MODEL)
```

```python
import functools
import math

import jax
import jax.numpy as jnp
from jax import lax
from jax.experimental import pallas as pl
from jax.experimental.pallas import tpu as pltpu

D_MODEL = 2048
HEAD_DIM = 64
ROPE_THETA = 10000.0
EPS = 1e-6
NEG_INF = -1e30
LOG2E = math.log2(math.e)

A_HEADS = 8
A_VDIM = 128
A_WIDTH = 1024
B_HEADS = 16
B_KV_HEADS = 2
B_GROUP = 8
B_WIDTH = 1024
WINDOW = 128
PROJ_WIDTH = 6400

OFF_QA, OFF_KA, OFF_VA, OFF_GA, OFF_QB, OFF_KB, OFF_VB, OFF_GB = 0, 1024, 2048, 3072, 4096, 5120, 5248, 5376

V7X_VMEM_LIMIT = 56 * 1024 * 1024
IN_PROJ_VMEM_LIMIT = 60 * 1024 * 1024
LANES = 128
CHUNK = 256

PROJ_BM = 512
ATT_TQ = 2048
ATT_QS = 1024
ATT_QS_DIAG = 256
OUT_BM = 512
F32_SUBLANES = 8

RAW_EXP_SCORE_BOUND = 60.0


def _rope_table_block(pos, invf, sign):
    half = HEAD_DIM // 2
    groups = LANES // half
    ang = pos * invf
    group = lax.broadcasted_iota(jnp.int32, ang.shape, 1) // half
    tables = []
    for src, sgn in ((jnp.cos(ang), None), (jnp.sin(ang), sign)):
        rolled = [src] + [pltpu.roll(src, half * k, 1) for k in range(1, groups)]
        slabs = []
        for t in range(groups):
            out = rolled[(0 - t) % groups]
            for g in range(1, groups):
                out = jnp.where(group == g, rolled[(g - t) % groups], out)
            slabs.append(out if sgn is None else out * sgn)
        tables.append(jnp.concatenate(slabs, axis=0))
    return tables


def _rope_positions(positions, m):
    half = HEAD_DIM // 2
    groups = LANES // half
    rows = PROJ_BM // groups
    pos = positions.reshape(m // PROJ_BM, groups, rows).astype(jnp.float32)
    return jnp.repeat(jnp.swapaxes(pos, 1, 2), half, axis=2).reshape(m // groups, LANES)


def _in_proj_kernel(x_ref, pos_ref, invf_ref, sign_ref, nw_ref, w_ref,
                    qna_ref, kna_ref, qnb_ref, knb_ref, vscale_ref,
                    qa_ref, ka_ref, va_ref, ga_ref, qb_ref, kb_ref, vb_ref, gb_ref):
    bm = x_ref.shape[0]
    x = x_ref[...]
    ms = jnp.mean(x * x, axis=-1, keepdims=True)
    h = ((x * lax.rsqrt(ms + EPS)) * nw_ref[...]).astype(jnp.bfloat16)

    cos_blk, sin_blk = _rope_table_block(pos_ref[...], invf_ref[...], sign_ref[...])
    cos = jnp.tile(cos_blk, (1, CHUNK // LANES))
    sin_signed = jnp.tile(sin_blk, (1, CHUNK // LANES))
    lane = lax.broadcasted_iota(jnp.int32, (bm, CHUNK), 1)
    first_half = (lane % HEAD_DIM) < (HEAD_DIM // 2)
    head_group = lane // HEAD_DIM

    def proj(c0, width=CHUNK):
        return jnp.dot(h, w_ref[:, c0:c0 + width], preferred_element_type=jnp.float32)

    def norm_rope(t, w):
        t2 = t * t
        msq = None
        for g in range(CHUNK // HEAD_DIM):
            in_g = head_group == g
            part = jnp.sum(jnp.where(in_g, t2, 0.0), axis=-1, keepdims=True) * (1.0 / HEAD_DIM)
            msq = part if msq is None else jnp.where(in_g, part, msq)
        y = t * lax.rsqrt(msq + EPS) * w
        rot = jnp.where(first_half, pltpu.roll(y, CHUNK - HEAD_DIM // 2, 1), pltpu.roll(y, HEAD_DIM // 2, 1))
        return y * cos + rot * sin_signed

    def silu(t):
        half_t = 0.5 * t
        return half_t + half_t * jnp.tanh(half_t)

    def put_qa(t, o):
        y = norm_rope(t, qna_ref[...]).astype(jnp.bfloat16)
        first_comp = (lane % A_VDIM) < HEAD_DIM
        zero = jnp.zeros_like(y)
        qa_ref[0, :, o:o + CHUNK] = jnp.where(first_comp, y, zero)
        qa_ref[1, :, o:o + CHUNK] = jnp.where(first_comp, zero, y)

    def put_ka(t, o):
        ka_ref[:, o:o + CHUNK] = norm_rope(t, kna_ref[...]).astype(jnp.bfloat16)

    def put_va(t, o):
        vat = (t * vscale_ref[...]).T.astype(jnp.bfloat16)
        for j in range(CHUNK // A_VDIM):
            va_ref[o // A_VDIM + j, 0] = vat[j * A_VDIM:(j + 1) * A_VDIM, :]

    def put_ga(t, o):
        ga_ref[:, o:o + CHUNK] = silu(t).astype(jnp.bfloat16)

    def put_gb(t, o):
        gb_ref[:, o:o + CHUNK] = silu(t).astype(jnp.bfloat16)

    def put_qb(t, o):
        qb = norm_rope(t, qnb_ref[...]).astype(jnp.bfloat16)
        for j in range(CHUNK // HEAD_DIM):
            qb_ref[o // HEAD_DIM + j] = qb[:, j * HEAD_DIM:(j + 1) * HEAD_DIM]

    def put_kvb(kv, o):
        kbn = norm_rope(kv, knb_ref[...]).astype(jnp.bfloat16)
        for j in range(B_KV_HEADS):
            kb_ref[j] = kbn[:, j * HEAD_DIM:(j + 1) * HEAD_DIM]
        v_lo = lane < CHUNK - HEAD_DIM
        v0 = jnp.where(v_lo, kv, pltpu.roll(kv, HEAD_DIM, 1))[:, LANES:]
        v1 = jnp.where(v_lo, pltpu.roll(kv, CHUNK - HEAD_DIM, 1), kv)[:, LANES:]
        vb_ref[0] = jnp.concatenate([v0, v0], axis=1).astype(jnp.bfloat16)
        vb_ref[1] = jnp.concatenate([v1, v1], axis=1).astype(jnp.bfloat16)

    tasks = [(OFF_VA, put_va, 0), (OFF_VA + CHUNK, put_va, CHUNK), (OFF_KB, put_kvb, 0)]
    for c in range(A_WIDTH // CHUNK):
        o = c * CHUNK
        tasks += [(OFF_QA + o, put_qa, o), (OFF_KA + o, put_ka, o), (OFF_QB + o, put_qb, o)]
        if c >= 2:
            tasks.append((OFF_VA + o, put_va, o))
        tasks += [(OFF_GA + o, put_ga, o), (OFF_GB + o, put_gb, o)]
    pending = None
    for col, put, o in tasks:
        t = proj(col)
        if pending is not None:
            pending[0](pending[1], pending[2])
        pending = (put, t, o)
    pending[0](pending[1], pending[2])


def _in_proj(x2, pos_rep, invf, sign, norm_w, w_bf16, qna, kna, qnb, knb, vscale):
    m = x2.shape[0]
    bm = PROJ_BM
    row = lambda i: (i, 0)
    const = lambda i: (0, 0)
    hm = lambda i: (0, i, 0)
    bf = jnp.bfloat16
    out_shape = (
        jax.ShapeDtypeStruct((2, m, A_WIDTH), bf),
        jax.ShapeDtypeStruct((m, A_WIDTH), bf),
        jax.ShapeDtypeStruct((A_HEADS, m // ATT_TQ, A_VDIM, ATT_TQ), bf),
        jax.ShapeDtypeStruct((m, A_WIDTH), bf),
        jax.ShapeDtypeStruct((B_HEADS, m, HEAD_DIM), bf),
        jax.ShapeDtypeStruct((B_KV_HEADS, m, HEAD_DIM), bf),
        jax.ShapeDtypeStruct((B_KV_HEADS, m, 4 * HEAD_DIM), bf),
        jax.ShapeDtypeStruct((m, B_WIDTH), bf),
    )
    out_specs = (
        pl.BlockSpec((2, bm, A_WIDTH), hm), pl.BlockSpec((bm, A_WIDTH), row),
        pl.BlockSpec((A_HEADS, 1, A_VDIM, bm), lambda i: (0, i // (ATT_TQ // bm), 0, i % (ATT_TQ // bm))),
        pl.BlockSpec((bm, A_WIDTH), row),
        pl.BlockSpec((B_HEADS, bm, HEAD_DIM), hm),
        pl.BlockSpec((B_KV_HEADS, bm, HEAD_DIM), hm),
        pl.BlockSpec((B_KV_HEADS, bm, 4 * HEAD_DIM), hm),
        pl.BlockSpec((bm, B_WIDTH), row),
    )
    in_specs = [
        pl.BlockSpec((bm, D_MODEL), row),
        pl.BlockSpec((bm // (LANES // (HEAD_DIM // 2)), LANES), row),
        pl.BlockSpec((1, LANES), const),
        pl.BlockSpec((1, LANES), const),
        pl.BlockSpec((1, D_MODEL), const),
        pl.BlockSpec((D_MODEL, PROJ_WIDTH), const, pipeline_mode=pl.Buffered(1)),
        pl.BlockSpec((1, CHUNK), const), pl.BlockSpec((1, CHUNK), const),
        pl.BlockSpec((1, CHUNK), const), pl.BlockSpec((1, CHUNK), const),
        pl.BlockSpec((1, 1), const),
    ]
    return pl.pallas_call(
        _in_proj_kernel,
        grid=(m // bm,),
        in_specs=in_specs,
        out_specs=out_specs,
        out_shape=out_shape,
        compiler_params=pltpu.CompilerParams(
            dimension_semantics=("arbitrary",), vmem_limit_bytes=IN_PROJ_VMEM_LIMIT),
        name="in_proj",
    )(x2, pos_rep, invf, sign, norm_w, w_bf16, qna, kna, qnb, knb, vscale)


def _diff_attn_kernel(q_ref, k_ref, vt_ref, g_ref, lq1_ref, lk1_ref, lq2_ref, lk2_ref, sw_ref, unscale_ref,
                      o_ref, acc_ref, l_ref, m_ref, *, lam_init, online_max):
    tq = q_ref.shape[1]
    qi = pl.program_id(2)

    def reset_accumulators():
        acc_ref[...] = jnp.zeros(acc_ref.shape, jnp.float32)
        l_ref[...] = jnp.zeros(l_ref.shape, jnp.float32)
        if online_max:
            m_ref[...] = jnp.full(m_ref.shape, NEG_INF, jnp.float32)

    @pl.when((pl.program_id(0) == 0) & (pl.program_id(1) == 0) & (qi == 0))
    def _():
        reset_accumulators()

    def scores(c, cols, j, nk, masked):
        k = k_ref[pl.ds(pl.multiple_of(j * tq, tq), nk), :]
        s = lax.dot_general(k, q_ref[c, cols, :], (((1,), (1,)), ((), ())), preferred_element_type=jnp.float32)
        if masked:
            key = lax.broadcasted_iota(jnp.int32, s.shape, 0)
            qry = lax.broadcasted_iota(jnp.int32, s.shape, 1)
            s = jnp.where(key <= qry + (nk - s.shape[1]), s, NEG_INF)
        return s

    def accumulate(s, c, cols, j):
        nk, nq_unit = s.shape
        vt = vt_ref[0, j, :, 0:nk]
        if online_max:
            m_old = m_ref[c, :, cols]
            m_new = jnp.maximum(m_old, jnp.max(s, axis=0, keepdims=True))
            alpha = jnp.exp2(m_old - m_new)
            m_ref[c, :, cols] = m_new
            p = jnp.exp2(s - m_new)
            pv = jnp.dot(vt, p.astype(jnp.bfloat16), preferred_element_type=jnp.float32)
            acc_ref[c, :, cols] = acc_ref[c, :, cols] * alpha + pv
            l_ref[c, :, cols] = (l_ref[c, :, cols] * alpha
                                 + p.reshape(nk // F32_SUBLANES, F32_SUBLANES, nq_unit).sum(axis=0))
        else:
            p = jnp.exp2(s)
            pv = jnp.dot(vt, p.astype(jnp.bfloat16), preferred_element_type=jnp.float32)
            acc_ref[c, :, cols] += pv
            l_ref[c, :, cols] += p.reshape(nk // F32_SUBLANES, F32_SUBLANES, nq_unit).sum(axis=0)

    def sweep(j, width, masked):
        order = reversed(range(tq // width)) if masked else range(tq // width)
        units = [(c, slice(r * width, (r + 1) * width)) for r in order for c in range(2)]
        nk_of = (lambda cols: cols.stop) if masked else (lambda cols: tq)
        s_next = scores(*units[0], j, nk_of(units[0][1]), masked)
        for idx, (c, cols) in enumerate(units):
            s = s_next
            if idx + 1 < len(units):
                cn, colsn = units[idx + 1]
                s_next = scores(cn, colsn, j, nk_of(colsn), masked)
            accumulate(s, c, cols, j)

    def body(j, carry):
        sweep(j, ATT_QS, False)
        return carry

    lax.fori_loop(0, qi, body, 0)
    sweep(qi, ATT_QS_DIAG, True)

    lam = (jnp.exp(jnp.sum(lq1_ref[...] * lk1_ref[...], axis=-1, keepdims=True))
           - jnp.exp(jnp.sum(lq2_ref[...] * lk2_ref[...], axis=-1, keepdims=True)) + lam_init)
    l1 = jnp.sum(l_ref[0], axis=0, keepdims=True)
    l2 = jnp.sum(l_ref[1], axis=0, keepdims=True)
    ot = (acc_ref[0] / l1 - lam * (acc_ref[1] / l2)) * unscale_ref[...]
    o = ot.T
    o = o * lax.rsqrt(jnp.mean(o * o, axis=-1, keepdims=True) + EPS) * sw_ref[...] * (1.0 - lam_init)
    o_ref[...] = (o * g_ref[...].astype(jnp.float32)).astype(o_ref.dtype)
    reset_accumulators()


def _diff_attn(qa, ka, va_t, ga, lq1, lk1, lq2, lk2, subln_w, unscale, *, batch, seq, lam_init, online_max):
    tq = ATT_TQ
    nq = seq // tq
    qmap = lambda b, h, i: (b * nq + i, h)
    const = lambda b, h, i: (0, 0)
    vec = pl.BlockSpec((1, HEAD_DIM), const)
    return pl.pallas_call(
        functools.partial(_diff_attn_kernel, lam_init=lam_init, online_max=online_max),
        grid=(batch, A_HEADS, nq),
        in_specs=[
            pl.BlockSpec((2, tq, A_VDIM), lambda b, h, i: (0, b * nq + i, h)),
            pl.BlockSpec((seq, A_VDIM), lambda b, h, i: (b, h)),
            pl.BlockSpec((1, nq, A_VDIM, tq), lambda b, h, i: (h, b, 0, 0)),
            pl.BlockSpec((tq, A_VDIM), qmap),
            vec, vec, vec, vec,
            pl.BlockSpec((1, A_VDIM), const),
            pl.BlockSpec((1, 1), const),
        ],
        out_specs=pl.BlockSpec((tq, A_VDIM), qmap),
        out_shape=jax.ShapeDtypeStruct((batch * seq, A_WIDTH), jnp.bfloat16),
        scratch_shapes=[
            pltpu.VMEM((2, A_VDIM, tq), jnp.float32),
            pltpu.VMEM((2, F32_SUBLANES, tq), jnp.float32),
            pltpu.VMEM((2, 1, tq), jnp.float32),
        ],
        compiler_params=pltpu.CompilerParams(
            dimension_semantics=("arbitrary", "arbitrary", "arbitrary"), vmem_limit_bytes=V7X_VMEM_LIMIT),
        name="diff_attn_online" if online_max else "diff_attn",
    )(qa, ka, va_t, ga, lq1, lk1, lq2, lk2, subln_w, unscale)


def _swa_block(q, k, v, sink_ref, kv, prev_bias):
    blk = WINDOW
    per_half = CHUNK // HEAD_DIM
    own = (lax.broadcasted_iota(jnp.int32, (blk, blk), 1) <= lax.broadcasted_iota(jnp.int32, (blk, blk), 0))
    vgroup = lax.broadcasted_iota(jnp.int32, (2 * blk, CHUNK), 1) // HEAD_DIM
    ogroup = lax.broadcasted_iota(jnp.int32, (blk, CHUNK), 1) // HEAD_DIM
    zero = jnp.zeros((blk, blk), jnp.bfloat16)
    s_all = lax.dot_general(q, k, (((1,), (1,)), ((), ())), preferred_element_type=jnp.float32)
    vstack = jnp.concatenate([jnp.where(vgroup == j, v, jnp.zeros_like(v)) for j in range(per_half)], axis=0)
    halves = []
    for hf in range(B_GROUP // per_half):
        probs = []
        scale = None
        for j in range(per_half):
            g = hf * per_half + j
            sink = sink_ref[kv, g * blk:(g + 1) * blk, :]
            s_prev = s_all[g * blk:(g + 1) * blk, :blk]
            if prev_bias is not None:
                s_prev = s_prev + prev_bias
            s = jnp.where(own, s_all[g * blk:(g + 1) * blk, blk:], s_prev)
            m = jnp.maximum(sink, jnp.max(s, axis=-1, keepdims=True))
            p = jnp.exp2(s - m)
            denom = jnp.exp2(sink - m) + jnp.sum(p, axis=-1, keepdims=True)
            pb = p.astype(jnp.bfloat16)
            probs += [jnp.where(own, zero, pb), jnp.where(own, pb, zero)]
            rg = jnp.tile(1.0 / denom, (1, 2))
            scale = rg if scale is None else jnp.where(ogroup == j, rg, scale)
        acc = jnp.dot(jnp.concatenate(probs, axis=1), vstack, preferred_element_type=jnp.float32)
        halves.append(acc * scale)
    return jnp.concatenate(halves, axis=1)


def _out_swa_kernel(ya_ref, q_ref, kp_ref, kc_ref, vp_ref, vc_ref, sink_ref, g_ref, w_ref, x_ref, o_ref, yb_ref,
                    *, steps_per_seq):
    bm = x_ref.shape[0]
    blk = WINDOW
    nblk = bm // blk
    gw = B_GROUP * HEAD_DIM
    has_prev = (pl.program_id(0) % steps_per_seq) > 0
    first_bias = jnp.where(has_prev, 0.0, NEG_INF)

    def swa(kv, n):
        q = q_ref[kv * B_GROUP:(kv + 1) * B_GROUP, n * blk:(n + 1) * blk, :].reshape(B_GROUP * blk, HEAD_DIM)
        if n == 0:
            k = jnp.concatenate([kp_ref[kv], kc_ref[kv, 0:blk, :]], axis=0)
            v = jnp.concatenate([vp_ref[kv], vc_ref[kv, 0:blk, :]], axis=0)
            bias = first_bias
        else:
            k = kc_ref[kv, (n - 1) * blk:(n + 1) * blk, :]
            v = vc_ref[kv, (n - 1) * blk:(n + 1) * blk, :]
            bias = None
        o = _swa_block(q, k, v, sink_ref, kv, bias)
        gate = g_ref[n * blk:(n + 1) * blk, kv * gw:(kv + 1) * gw].astype(jnp.float32)
        yb_ref[n * blk:(n + 1) * blk, kv * gw:(kv + 1) * gw] = (o * gate).astype(yb_ref.dtype)

    width = 2 * CHUNK
    assert nblk == D_MODEL // width

    def project(lhs, rows, first):
        def slab(idx):
            cs = slice(idx * width, (idx + 1) * width)
            part = jnp.dot(lhs, w_ref[rows, cs], preferred_element_type=jnp.float32)
            if first:
                o_ref[:, cs] = x_ref[:, cs] + part
            else:
                o_ref[:, cs] += part
        return slab

    slab_a = project(ya_ref[...], slice(0, A_WIDTH), True)
    for n in range(nblk):
        slab_a(n)
        swa(0, n)
    for kv in range(B_KV_HEADS):
        rows = slice(A_WIDTH + kv * gw, A_WIDTH + (kv + 1) * gw)
        slab_b = project(yb_ref[:, kv * gw:(kv + 1) * gw], rows, False)
        for n in range(nblk):
            slab_b(n)
            if kv + 1 < B_KV_HEADS:
                swa(kv + 1, n)


def _out_swa(ya, qb, kb, vb_rep, gb, sink_rep, w_o, x2, seq):
    m = x2.shape[0]
    bm = OUT_BM
    r = bm // WINDOW
    row = lambda i: (i, 0)
    cur = lambda i: (0, i, 0)
    prev = lambda i: (0, jnp.maximum(i * r - 1, 0), 0)
    return pl.pallas_call(
        functools.partial(_out_swa_kernel, steps_per_seq=seq // bm),
        grid=(m // bm,),
        in_specs=[
            pl.BlockSpec((bm, A_WIDTH), row),
            pl.BlockSpec((B_HEADS, bm, HEAD_DIM), cur),
            pl.BlockSpec((B_KV_HEADS, WINDOW, HEAD_DIM), prev),
            pl.BlockSpec((B_KV_HEADS, bm, HEAD_DIM), cur),
            pl.BlockSpec((B_KV_HEADS, WINDOW, CHUNK), prev),
            pl.BlockSpec((B_KV_HEADS, bm, CHUNK), cur),
            pl.BlockSpec((B_KV_HEADS, B_GROUP * WINDOW, LANES), lambda i: (0, 0, 0)),
            pl.BlockSpec((bm, B_WIDTH), row),
            pl.BlockSpec((A_WIDTH + B_WIDTH, D_MODEL), lambda i: (0, 0), pipeline_mode=pl.Buffered(1)),
            pl.BlockSpec((bm, D_MODEL), row),
        ],
        out_specs=pl.BlockSpec((bm, D_MODEL), row),
        out_shape=jax.ShapeDtypeStruct((m, D_MODEL), jnp.float32),
        scratch_shapes=[pltpu.VMEM((bm, B_WIDTH), jnp.bfloat16)],
        compiler_params=pltpu.CompilerParams(
            dimension_semantics=("arbitrary",), vmem_limit_bytes=V7X_VMEM_LIMIT),
        name="out_swa",
    )(ya, qb, kb, kb, vb_rep, vb_rep, sink_rep, gb, w_o, x2)


def kernel(x, positions, norm_w, w_in, q_norm_a, k_norm_a, lambda_q1, lambda_k1, lambda_q2, lambda_k2,
           subln_w, q_norm_b, k_norm_b, sinks, w_out):
    batch, seq, _ = x.shape
    depth = norm_w.shape[0]
    m = batch * seq
    f32 = jnp.float32

    half = HEAD_DIM // 2
    inv_freq = ROPE_THETA ** (-(jnp.arange(0, HEAD_DIM, 2, dtype=f32) / HEAD_DIM))
    invf = jnp.tile(inv_freq, LANES // half).reshape(1, LANES)
    sign = jnp.tile(jnp.concatenate([-jnp.ones((half,), f32), jnp.ones((half,), f32)]),
                    LANES // HEAD_DIM).reshape(1, LANES)
    pos_rep = _rope_positions(positions, m)
    qk_scale = HEAD_DIM ** -0.5 * LOG2E
    tile = lambda w: jnp.tile(w.astype(f32), CHUNK // HEAD_DIM).reshape(1, CHUNK)

    x2 = x.reshape(m, D_MODEL)
    for layer in range(depth):
        lam_init = 0.8 - 0.6 * math.exp(-0.3 * layer)
        w_bf16 = w_in[layer].astype(jnp.bfloat16)
        score_bound = (HEAD_DIM * qk_scale * jnp.max(jnp.abs(q_norm_a[layer].astype(f32)))
                       * jnp.max(jnp.abs(k_norm_a[layer].astype(f32))))
        raw_exp = score_bound <= RAW_EXP_SCORE_BOUND
        v_shift = jnp.where(raw_exp, jnp.ceil(score_bound), 0.0).astype(jnp.int32).reshape(1, 1)
        one = jnp.ones((1, 1), f32)
        qa, ka, va_t, ga, qb, kb, vb_rep, gb = _in_proj(
            x2, pos_rep, invf, sign, norm_w[layer].reshape(1, D_MODEL), w_bf16,
            tile(q_norm_a[layer]) * qk_scale, tile(k_norm_a[layer]),
            tile(q_norm_b[layer]) * qk_scale, tile(k_norm_b[layer]), jnp.ldexp(one, -v_shift))
        vec = lambda p: p[layer].astype(f32).reshape(1, HEAD_DIM)
        attn_args = (qa, ka, va_t, ga, vec(lambda_q1), vec(lambda_k1), vec(lambda_q2), vec(lambda_k2),
                     subln_w[layer].astype(f32).reshape(1, A_VDIM), jnp.ldexp(one, v_shift))
        attn = functools.partial(_diff_attn, batch=batch, seq=seq, lam_init=lam_init)
        ya = lax.cond(raw_exp,
                      functools.partial(attn, online_max=False),
                      functools.partial(attn, online_max=True), *attn_args)
        sink_rep = jnp.broadcast_to(
            jnp.repeat(sinks[layer].astype(f32) * LOG2E, WINDOW).reshape(B_KV_HEADS, B_GROUP * WINDOW, 1),
            (B_KV_HEADS, B_GROUP * WINDOW, LANES))
        x2 = _out_swa(ya, qb, kb, vb_rep, gb, sink_rep, w_out[layer].astype(jnp.bfloat16), x2, seq)
    return x2.reshape(batch, seq, D_MODEL)
```

```python
import functools
import math

import jax
import jax.numpy as jnp
from jax import lax
from jax.experimental import pallas as pl
from jax.experimental.pallas import tpu as pltpu

D_MODEL = 2048
HEAD_DIM = 64
ROPE_THETA = 10000.0
EPS = 1e-6
NEG_INF = -1e30
LOG2E = math.log2(math.e)

A_HEADS = 8
A_VDIM = 128
A_WIDTH = 1024
B_HEADS = 16
B_KV_HEADS = 2
B_GROUP = 8
B_WIDTH = 1024
WINDOW = 128
PROJ_WIDTH = 6400

OFF_QA, OFF_KA, OFF_VA, OFF_GA, OFF_QB, OFF_KB, OFF_VB, OFF_GB = 0, 1024, 2048, 3072, 4096, 5120, 5248, 5376

V7X_VMEM_LIMIT = 56 * 1024 * 1024
IN_PROJ_VMEM_LIMIT = 60 * 1024 * 1024
LANES = 128
CHUNK = 256

PROJ_BM = 512
ATT_TQ = 2048
ATT_QS = 1024
ATT_QS_DIAG = 256
OUT_BM = 512
F32_SUBLANES = 8

RAW_EXP_SCORE_BOUND = 60.0


def _rope_table_block(pos, invf, sign):
    half = HEAD_DIM // 2
    groups = LANES // half
    ang = pos * invf
    group = lax.broadcasted_iota(jnp.int32, ang.shape, 1) // half
    tables = []
    for src, sgn in ((jnp.cos(ang), None), (jnp.sin(ang), sign)):
        rolled = [src] + [pltpu.roll(src, half * k, 1) for k in range(1, groups)]
        slabs = []
        for t in range(groups):
            out = rolled[(0 - t) % groups]
            for g in range(1, groups):
                out = jnp.where(group == g, rolled[(g - t) % groups], out)
            slabs.append(out if sgn is None else out * sgn)
        tables.append(jnp.concatenate(slabs, axis=0))
    return tables


def _rope_positions(positions, m):
    half = HEAD_DIM // 2
    groups = LANES // half
    rows = PROJ_BM // groups
    pos = positions.reshape(m // PROJ_BM, groups, rows).astype(jnp.float32)
    return jnp.repeat(jnp.swapaxes(pos, 1, 2), half, axis=2).reshape(m // groups, LANES)


def _in_proj_kernel(x_ref, pos_ref, invf_ref, sign_ref, nw_ref, w_ref,
                    qna_ref, kna_ref, qnb_ref, knb_ref, vscale_ref,
                    qa_ref, ka_ref, va_ref, ga_ref, qb_ref, kb_ref, vb_ref, gb_ref):
    bm = x_ref.shape[0]
    x = x_ref[...]
    ms = jnp.mean(x * x, axis=-1, keepdims=True)
    h = ((x * lax.rsqrt(ms + EPS)) * nw_ref[...]).astype(jnp.bfloat16)

    cos_blk, sin_blk = _rope_table_block(pos_ref[...], invf_ref[...], sign_ref[...])
    cos = jnp.tile(cos_blk, (1, CHUNK // LANES))
    sin_signed = jnp.tile(sin_blk, (1, CHUNK // LANES))
    lane = lax.broadcasted_iota(jnp.int32, (bm, CHUNK), 1)
    first_half = (lane % HEAD_DIM) < (HEAD_DIM // 2)
    head_group = lane // HEAD_DIM

    def proj(c0, width=CHUNK):
        return jnp.dot(h, w_ref[:, c0:c0 + width], preferred_element_type=jnp.float32)

    def norm_rope(t, w):
        t2 = t * t
        msq = None
        for g in range(CHUNK // HEAD_DIM):
            in_g = head_group == g
            part = jnp.sum(jnp.where(in_g, t2, 0.0), axis=-1, keepdims=True) * (1.0 / HEAD_DIM)
            msq = part if msq is None else jnp.where(in_g, part, msq)
        y = t * lax.rsqrt(msq + EPS) * w
        rot = jnp.where(first_half, pltpu.roll(y, CHUNK - HEAD_DIM // 2, 1), pltpu.roll(y, HEAD_DIM // 2, 1))
        return y * cos + rot * sin_signed

    def silu(t):
        half_t = 0.5 * t
        return half_t + half_t * jnp.tanh(half_t)

    def put_qa(t, o):
        y = norm_rope(t, qna_ref[...]).astype(jnp.bfloat16)
        first_comp = (lane % A_VDIM) < HEAD_DIM
        zero = jnp.zeros_like(y)
        qa_ref[0, :, o:o + CHUNK] = jnp.where(first_comp, y, zero)
        qa_ref[1, :, o:o + CHUNK] = jnp.where(first_comp, zero, y)

    def put_ka(t, o):
        ka_ref[:, o:o + CHUNK] = norm_rope(t, kna_ref[...]).astype(jnp.bfloat16)

    def put_va(t, o):
        vat = (t * vscale_ref[...]).T.astype(jnp.bfloat16)
        for j in range(CHUNK // A_VDIM):
            va_ref[o // A_VDIM + j, 0] = vat[j * A_VDIM:(j + 1) * A_VDIM, :]

    def put_ga(t, o):
        ga_ref[:, o:o + CHUNK] = silu(t).astype(jnp.bfloat16)

    def put_gb(t, o):
        gb_ref[:, o:o + CHUNK] = silu(t).astype(jnp.bfloat16)

    def put_qb(t, o):
        qb = norm_rope(t, qnb_ref[...]).astype(jnp.bfloat16)
        for j in range(CHUNK // HEAD_DIM):
            qb_ref[o // HEAD_DIM + j] = qb[:, j * HEAD_DIM:(j + 1) * HEAD_DIM]

    def put_kvb(kv, o):
        kbn = norm_rope(kv, knb_ref[...]).astype(jnp.bfloat16)
        for j in range(B_KV_HEADS):
            kb_ref[j] = kbn[:, j * HEAD_DIM:(j + 1) * HEAD_DIM]
        v_lo = lane < CHUNK - HEAD_DIM
        v0 = jnp.where(v_lo, kv, pltpu.roll(kv, HEAD_DIM, 1))[:, LANES:]
        v1 = jnp.where(v_lo, pltpu.roll(kv, CHUNK - HEAD_DIM, 1), kv)[:, LANES:]
        vb_ref[0] = jnp.concatenate([v0, v0], axis=1).astype(jnp.bfloat16)
        vb_ref[1] = jnp.concatenate([v1, v1], axis=1).astype(jnp.bfloat16)

    tasks = [(OFF_VA, put_va, 0), (OFF_VA + CHUNK, put_va, CHUNK), (OFF_KB, put_kvb, 0)]
    for c in range(A_WIDTH // CHUNK):
        o = c * CHUNK
        tasks += [(OFF_QA + o, put_qa, o), (OFF_KA + o, put_ka, o), (OFF_QB + o, put_qb, o)]
        if c >= 2:
            tasks.append((OFF_VA + o, put_va, o))
        tasks += [(OFF_GA + o, put_ga, o), (OFF_GB + o, put_gb, o)]
    pending = None
    for col, put, o in tasks:
        t = proj(col)
        if pending is not None:
            pending[0](pending[1], pending[2])
        pending = (put, t, o)
    pending[0](pending[1], pending[2])


def _in_proj(x2, pos_rep, invf, sign, norm_w, w_bf16, qna, kna, qnb, knb, vscale):
    m = x2.shape[0]
    bm = PROJ_BM
    row = lambda i: (i, 0)
    const = lambda i: (0, 0)
    hm = lambda i: (0, i, 0)
    bf = jnp.bfloat16
    out_shape = (
        jax.ShapeDtypeStruct((2, m, A_WIDTH), bf),
        jax.ShapeDtypeStruct((m, A_WIDTH), bf),
        jax.ShapeDtypeStruct((A_HEADS, m // ATT_TQ, A_VDIM, ATT_TQ), bf),
        jax.ShapeDtypeStruct((m, A_WIDTH), bf),
        jax.ShapeDtypeStruct((B_HEADS, m, HEAD_DIM), bf),
        jax.ShapeDtypeStruct((B_KV_HEADS, m, HEAD_DIM), bf),
        jax.ShapeDtypeStruct((B_KV_HEADS, m, 4 * HEAD_DIM), bf),
        jax.ShapeDtypeStruct((m, B_WIDTH), bf),
    )
    out_specs = (
        pl.BlockSpec((2, bm, A_WIDTH), hm), pl.BlockSpec((bm, A_WIDTH), row),
        pl.BlockSpec((A_HEADS, 1, A_VDIM, bm), lambda i: (0, i // (ATT_TQ // bm), 0, i % (ATT_TQ // bm))),
        pl.BlockSpec((bm, A_WIDTH), row),
        pl.BlockSpec((B_HEADS, bm, HEAD_DIM), hm),
        pl.BlockSpec((B_KV_HEADS, bm, HEAD_DIM), hm),
        pl.BlockSpec((B_KV_HEADS, bm, 4 * HEAD_DIM), hm),
        pl.BlockSpec((bm, B_WIDTH), row),
    )
    in_specs = [
        pl.BlockSpec((bm, D_MODEL), row),
        pl.BlockSpec((bm // (LANES // (HEAD_DIM // 2)), LANES), row),
        pl.BlockSpec((1, LANES), const),
        pl.BlockSpec((1, LANES), const),
        pl.BlockSpec((1, D_MODEL), const),
        pl.BlockSpec((D_MODEL, PROJ_WIDTH), const, pipeline_mode=pl.Buffered(1)),
        pl.BlockSpec((1, CHUNK), const), pl.BlockSpec((1, CHUNK), const),
        pl.BlockSpec((1, CHUNK), const), pl.BlockSpec((1, CHUNK), const),
        pl.BlockSpec((1, 1), const),
    ]
    return pl.pallas_call(
        _in_proj_kernel,
        grid=(m // bm,),
        in_specs=in_specs,
        out_specs=out_specs,
        out_shape=out_shape,
        compiler_params=pltpu.CompilerParams(
            dimension_semantics=("arbitrary",), vmem_limit_bytes=IN_PROJ_VMEM_LIMIT),
        name="in_proj",
    )(x2, pos_rep, invf, sign, norm_w, w_bf16, qna, kna, qnb, knb, vscale)


def _diff_attn_kernel(raw_exp_ref, *refs, lam_init):
    @pl.when(raw_exp_ref[0] == 1)
    def _():
        _diff_attn_body(*refs, lam_init=lam_init, online_max=False)

    @pl.when(raw_exp_ref[0] != 1)
    def _():
        _diff_attn_body(*refs, lam_init=lam_init, online_max=True)


def _diff_attn_body(q_ref, k_ref, vt_ref, g_ref, lq1_ref, lk1_ref, lq2_ref, lk2_ref, sw_ref, unscale_ref,
                    o_ref, acc_ref, l_ref, m_ref, *, lam_init, online_max):
    tq = q_ref.shape[1]
    qi = pl.program_id(2)

    def reset_accumulators():
        acc_ref[...] = jnp.zeros(acc_ref.shape, jnp.float32)
        l_ref[...] = jnp.zeros(l_ref.shape, jnp.float32)
        if online_max:
            m_ref[...] = jnp.full(m_ref.shape, NEG_INF, jnp.float32)

    @pl.when((pl.program_id(0) == 0) & (pl.program_id(1) == 0) & (qi == 0))
    def _():
        reset_accumulators()

    def scores(c, cols, j, nk, masked):
        k = k_ref[pl.ds(pl.multiple_of(j * tq, tq), nk), :]
        s = lax.dot_general(k, q_ref[c, cols, :], (((1,), (1,)), ((), ())), preferred_element_type=jnp.float32)
        if masked:
            key = lax.broadcasted_iota(jnp.int32, s.shape, 0)
            qry = lax.broadcasted_iota(jnp.int32, s.shape, 1)
            s = jnp.where(key <= qry + (nk - s.shape[1]), s, NEG_INF)
        return s

    def accumulate(s, c, cols, j):
        nk, nq_unit = s.shape
        vt = vt_ref[0, j, :, 0:nk]
        if online_max:
            m_old = m_ref[c, :, cols]
            m_new = jnp.maximum(m_old, jnp.max(s, axis=0, keepdims=True))
            alpha = jnp.exp2(m_old - m_new)
            m_ref[c, :, cols] = m_new
            p = jnp.exp2(s - m_new)
            pv = jnp.dot(vt, p.astype(jnp.bfloat16), preferred_element_type=jnp.float32)
            acc_ref[c, :, cols] = acc_ref[c, :, cols] * alpha + pv
            l_ref[c, :, cols] = (l_ref[c, :, cols] * alpha
                                 + p.reshape(nk // F32_SUBLANES, F32_SUBLANES, nq_unit).sum(axis=0))
        else:
            p = jnp.exp2(s)
            pv = jnp.dot(vt, p.astype(jnp.bfloat16), preferred_element_type=jnp.float32)
            acc_ref[c, :, cols] += pv
            l_ref[c, :, cols] += p.reshape(nk // F32_SUBLANES, F32_SUBLANES, nq_unit).sum(axis=0)

    def sweep(j, width, masked):
        order = reversed(range(tq // width)) if masked else range(tq // width)
        units = [(c, slice(r * width, (r + 1) * width)) for r in order for c in range(2)]
        nk_of = (lambda cols: cols.stop) if masked else (lambda cols: tq)
        s_next = scores(*units[0], j, nk_of(units[0][1]), masked)
        for idx, (c, cols) in enumerate(units):
            s = s_next
            if idx + 1 < len(units):
                cn, colsn = units[idx + 1]
                s_next = scores(cn, colsn, j, nk_of(colsn), masked)
            accumulate(s, c, cols, j)

    def body(j, carry):
        sweep(j, ATT_QS, False)
        return carry

    lax.fori_loop(0, qi, body, 0)
    sweep(qi, ATT_QS_DIAG, True)

    lam = (jnp.exp(jnp.sum(lq1_ref[...] * lk1_ref[...], axis=-1, keepdims=True))
           - jnp.exp(jnp.sum(lq2_ref[...] * lk2_ref[...], axis=-1, keepdims=True)) + lam_init)
    l1 = jnp.sum(l_ref[0], axis=0, keepdims=True)
    l2 = jnp.sum(l_ref[1], axis=0, keepdims=True)
    ot = (acc_ref[0] / l1 - lam * (acc_ref[1] / l2)) * unscale_ref[...]
    o = ot.T
    o = o * lax.rsqrt(jnp.mean(o * o, axis=-1, keepdims=True) + EPS) * sw_ref[...] * (1.0 - lam_init)
    o_ref[...] = (o * g_ref[...].astype(jnp.float32)).astype(o_ref.dtype)
    reset_accumulators()


def _diff_attn(raw_exp, qa, ka, va_t, ga, lq1, lk1, lq2, lk2, subln_w, unscale, *, batch, seq, lam_init):
    tq = ATT_TQ
    nq = seq // tq
    qmap = lambda b, h, i, flag: (b * nq + i, h)
    const = lambda b, h, i, flag: (0, 0)
    vec = pl.BlockSpec((1, HEAD_DIM), const)
    grid_spec = pltpu.PrefetchScalarGridSpec(
        num_scalar_prefetch=1,
        grid=(batch, A_HEADS, nq),
        in_specs=[
            pl.BlockSpec((2, tq, A_VDIM), lambda b, h, i, flag: (0, b * nq + i, h)),
            pl.BlockSpec((seq, A_VDIM), lambda b, h, i, flag: (b, h)),
            pl.BlockSpec((1, nq, A_VDIM, tq), lambda b, h, i, flag: (h, b, 0, 0)),
            pl.BlockSpec((tq, A_VDIM), qmap),
            vec, vec, vec, vec,
            pl.BlockSpec((1, A_VDIM), const),
            pl.BlockSpec((1, 1), const),
        ],
        out_specs=pl.BlockSpec((tq, A_VDIM), qmap),
        scratch_shapes=[
            pltpu.VMEM((2, A_VDIM, tq), jnp.float32),
            pltpu.VMEM((2, F32_SUBLANES, tq), jnp.float32),
            pltpu.VMEM((2, 1, tq), jnp.float32),
        ],
    )
    return pl.pallas_call(
        functools.partial(_diff_attn_kernel, lam_init=lam_init),
        grid_spec=grid_spec,
        out_shape=jax.ShapeDtypeStruct((batch * seq, A_WIDTH), jnp.bfloat16),
        compiler_params=pltpu.CompilerParams(
            dimension_semantics=("arbitrary", "arbitrary", "arbitrary"), vmem_limit_bytes=V7X_VMEM_LIMIT),
        name="diff_attn",
    )(raw_exp, qa, ka, va_t, ga, lq1, lk1, lq2, lk2, subln_w, unscale)


def _swa_block(q, k, v, sink_ref, kv, prev_bias):
    blk = WINDOW
    per_half = CHUNK // HEAD_DIM
    own = (lax.broadcasted_iota(jnp.int32, (blk, blk), 1) <= lax.broadcasted_iota(jnp.int32, (blk, blk), 0))
    vgroup = lax.broadcasted_iota(jnp.int32, (2 * blk, CHUNK), 1) // HEAD_DIM
    ogroup = lax.broadcasted_iota(jnp.int32, (blk, CHUNK), 1) // HEAD_DIM
    zero = jnp.zeros((blk, blk), jnp.bfloat16)
    s_all = lax.dot_general(q, k, (((1,), (1,)), ((), ())), preferred_element_type=jnp.float32)
    vstack = jnp.concatenate([jnp.where(vgroup == j, v, jnp.zeros_like(v)) for j in range(per_half)], axis=0)
    halves = []
    for hf in range(B_GROUP // per_half):
        probs = []
        scale = None
        for j in range(per_half):
            g = hf * per_half + j
            sink = sink_ref[kv, g * blk:(g + 1) * blk, :]
            s_prev = s_all[g * blk:(g + 1) * blk, :blk]
            if prev_bias is not None:
                s_prev = s_prev + prev_bias
            s = jnp.where(own, s_all[g * blk:(g + 1) * blk, blk:], s_prev)
            m = jnp.maximum(sink, jnp.max(s, axis=-1, keepdims=True))
            p = jnp.exp2(s - m)
            denom = jnp.exp2(sink - m) + jnp.sum(p, axis=-1, keepdims=True)
            pb = p.astype(jnp.bfloat16)
            probs += [jnp.where(own, zero, pb), jnp.where(own, pb, zero)]
            rg = jnp.tile(1.0 / denom, (1, 2))
            scale = rg if scale is None else jnp.where(ogroup == j, rg, scale)
        acc = jnp.dot(jnp.concatenate(probs, axis=1), vstack, preferred_element_type=jnp.float32)
        halves.append(acc * scale)
    return jnp.concatenate(halves, axis=1)


def _out_swa_kernel(ya_ref, q_ref, kp_ref, kc_ref, vp_ref, vc_ref, sink_ref, g_ref, w_ref, x_ref, o_ref, yb_ref,
                    *, steps_per_seq):
    bm = x_ref.shape[0]
    blk = WINDOW
    nblk = bm // blk
    gw = B_GROUP * HEAD_DIM
    has_prev = (pl.program_id(0) % steps_per_seq) > 0
    first_bias = jnp.where(has_prev, 0.0, NEG_INF)

    def swa(kv, n):
        q = q_ref[kv * B_GROUP:(kv + 1) * B_GROUP, n * blk:(n + 1) * blk, :].reshape(B_GROUP * blk, HEAD_DIM)
        if n == 0:
            k = jnp.concatenate([kp_ref[kv], kc_ref[kv, 0:blk, :]], axis=0)
            v = jnp.concatenate([vp_ref[kv], vc_ref[kv, 0:blk, :]], axis=0)
            bias = first_bias
        else:
            k = kc_ref[kv, (n - 1) * blk:(n + 1) * blk, :]
            v = vc_ref[kv, (n - 1) * blk:(n + 1) * blk, :]
            bias = None
        o = _swa_block(q, k, v, sink_ref, kv, bias)
        gate = g_ref[n * blk:(n + 1) * blk, kv * gw:(kv + 1) * gw].astype(jnp.float32)
        yb_ref[n * blk:(n + 1) * blk, kv * gw:(kv + 1) * gw] = (o * gate).astype(yb_ref.dtype)

    width = 2 * CHUNK
    assert nblk == D_MODEL // width

    def project(lhs, rows, first):
        def slab(idx):
            cs = slice(idx * width, (idx + 1) * width)
            part = jnp.dot(lhs, w_ref[rows, cs], preferred_element_type=jnp.float32)
            if first:
                o_ref[:, cs] = x_ref[:, cs] + part
            else:
                o_ref[:, cs] += part
        return slab

    slab_a = project(ya_ref[...], slice(0, A_WIDTH), True)
    for n in range(nblk):
        slab_a(n)
        swa(0, n)
    for kv in range(B_KV_HEADS):
        rows = slice(A_WIDTH + kv * gw, A_WIDTH + (kv + 1) * gw)
        slab_b = project(yb_ref[:, kv * gw:(kv + 1) * gw], rows, False)
        for n in range(nblk):
            slab_b(n)
            if kv + 1 < B_KV_HEADS:
                swa(kv + 1, n)


def _out_swa(ya, qb, kb, vb_rep, gb, sink_rep, w_o, x2, seq):
    m = x2.shape[0]
    bm = OUT_BM
    r = bm // WINDOW
    row = lambda i: (i, 0)
    cur = lambda i: (0, i, 0)
    prev = lambda i: (0, jnp.maximum(i * r - 1, 0), 0)
    return pl.pallas_call(
        functools.partial(_out_swa_kernel, steps_per_seq=seq // bm),
        grid=(m // bm,),
        in_specs=[
            pl.BlockSpec((bm, A_WIDTH), row),
            pl.BlockSpec((B_HEADS, bm, HEAD_DIM), cur),
            pl.BlockSpec((B_KV_HEADS, WINDOW, HEAD_DIM), prev),
            pl.BlockSpec((B_KV_HEADS, bm, HEAD_DIM), cur),
            pl.BlockSpec((B_KV_HEADS, WINDOW, CHUNK), prev),
            pl.BlockSpec((B_KV_HEADS, bm, CHUNK), cur),
            pl.BlockSpec((B_KV_HEADS, B_GROUP * WINDOW, LANES), lambda i: (0, 0, 0)),
            pl.BlockSpec((bm, B_WIDTH), row),
            pl.BlockSpec((A_WIDTH + B_WIDTH, D_MODEL), lambda i: (0, 0), pipeline_mode=pl.Buffered(1)),
            pl.BlockSpec((bm, D_MODEL), row),
        ],
        out_specs=pl.BlockSpec((bm, D_MODEL), row),
        out_shape=jax.ShapeDtypeStruct((m, D_MODEL), jnp.float32),
        scratch_shapes=[pltpu.VMEM((bm, B_WIDTH), jnp.bfloat16)],
        compiler_params=pltpu.CompilerParams(
            dimension_semantics=("arbitrary",), vmem_limit_bytes=V7X_VMEM_LIMIT),
        name="out_swa",
    )(ya, qb, kb, kb, vb_rep, vb_rep, sink_rep, gb, w_o, x2)


def kernel(x, positions, norm_w, w_in, q_norm_a, k_norm_a, lambda_q1, lambda_k1, lambda_q2, lambda_k2,
           subln_w, q_norm_b, k_norm_b, sinks, w_out):
    batch, seq, _ = x.shape
    depth = norm_w.shape[0]
    m = batch * seq
    f32 = jnp.float32

    half = HEAD_DIM // 2
    inv_freq = ROPE_THETA ** (-(jnp.arange(0, HEAD_DIM, 2, dtype=f32) / HEAD_DIM))
    invf = jnp.tile(inv_freq, LANES // half).reshape(1, LANES)
    sign = jnp.tile(jnp.concatenate([-jnp.ones((half,), f32), jnp.ones((half,), f32)]),
                    LANES // HEAD_DIM).reshape(1, LANES)
    pos_rep = _rope_positions(positions, m)
    qk_scale = HEAD_DIM ** -0.5 * LOG2E
    tile = lambda w: jnp.tile(w.astype(f32), CHUNK // HEAD_DIM).reshape(1, CHUNK)

    x2 = x.reshape(m, D_MODEL)
    for layer in range(depth):
        lam_init = 0.8 - 0.6 * math.exp(-0.3 * layer)
        w_bf16 = w_in[layer].astype(jnp.bfloat16)
        score_bound = (HEAD_DIM * qk_scale * jnp.max(jnp.abs(q_norm_a[layer].astype(f32)))
                       * jnp.max(jnp.abs(k_norm_a[layer].astype(f32))))
        raw_exp = score_bound <= RAW_EXP_SCORE_BOUND
        v_shift = jnp.where(raw_exp, jnp.ceil(score_bound), 0.0).astype(jnp.int32).reshape(1, 1)
        one = jnp.ones((1, 1), f32)
        qa, ka, va_t, ga, qb, kb, vb_rep, gb = _in_proj(
            x2, pos_rep, invf, sign, norm_w[layer].reshape(1, D_MODEL), w_bf16,
            tile(q_norm_a[layer]) * qk_scale, tile(k_norm_a[layer]),
            tile(q_norm_b[layer]) * qk_scale, tile(k_norm_b[layer]), jnp.ldexp(one, -v_shift))
        vec = lambda p: p[layer].astype(f32).reshape(1, HEAD_DIM)
        attn_args = (qa, ka, va_t, ga, vec(lambda_q1), vec(lambda_k1), vec(lambda_q2), vec(lambda_k2),
                     subln_w[layer].astype(f32).reshape(1, A_VDIM), jnp.ldexp(one, v_shift))
        ya = _diff_attn(raw_exp.astype(jnp.int32).reshape(1), *attn_args, batch=batch, seq=seq, lam_init=lam_init)
        sink_rep = jnp.broadcast_to(
            jnp.repeat(sinks[layer].astype(f32) * LOG2E, WINDOW).reshape(B_KV_HEADS, B_GROUP * WINDOW, 1),
            (B_KV_HEADS, B_GROUP * WINDOW, LANES))
        x2 = _out_swa(ya, qb, kb, vb_rep, gb, sink_rep, w_out[layer].astype(jnp.bfloat16), x2, seq)
    return x2.reshape(batch, seq, D_MODEL)
```

```python
import functools
import math

import jax
import jax.numpy as jnp
from jax import lax
from jax.experimental import pallas as pl
from jax.experimental.pallas import tpu as pltpu

D_MODEL = 2048
HEAD_DIM = 64
ROPE_THETA = 10000.0
EPS = 1e-6
NEG_INF = -1e30
LOG2E = math.log2(math.e)

A_HEADS = 8
A_VDIM = 128
A_WIDTH = 1024
B_HEADS = 16
B_KV_HEADS = 2
B_GROUP = 8
B_WIDTH = 1024
WINDOW = 128
PROJ_WIDTH = 6400

OFF_QA, OFF_KA, OFF_VA, OFF_GA, OFF_QB, OFF_KB, OFF_VB, OFF_GB = 0, 1024, 2048, 3072, 4096, 5120, 5248, 5376

V7X_VMEM_LIMIT = 56 * 1024 * 1024
IN_PROJ_VMEM_LIMIT = 60 * 1024 * 1024
LANES = 128
CHUNK = 256

PROJ_BM = 512
ATT_TQ = 2048
ATT_QS = 1024
ATT_QS_DIAG = 256
OUT_BM = 512
F32_SUBLANES = 8

RAW_EXP_SCORE_BOUND = 60.0


def _rope_table_block(pos, invf, sign):
    half = HEAD_DIM // 2
    groups = LANES // half
    ang = pos * invf
    group = lax.broadcasted_iota(jnp.int32, ang.shape, 1) // half
    tables = []
    for src, sgn in ((jnp.cos(ang), None), (jnp.sin(ang), sign)):
        rolled = [src] + [pltpu.roll(src, half * k, 1) for k in range(1, groups)]
        slabs = []
        for t in range(groups):
            out = rolled[(0 - t) % groups]
            for g in range(1, groups):
                out = jnp.where(group == g, rolled[(g - t) % groups], out)
            slabs.append(out if sgn is None else out * sgn)
        tables.append(jnp.concatenate(slabs, axis=0))
    return tables


def _rope_positions(positions, m):
    half = HEAD_DIM // 2
    groups = LANES // half
    rows = PROJ_BM // groups
    pos = positions.reshape(m // PROJ_BM, groups, rows).astype(jnp.float32)
    return jnp.repeat(jnp.swapaxes(pos, 1, 2), half, axis=2).reshape(m // groups, LANES)


def _in_proj_kernel(x_ref, pos_ref, invf_ref, sign_ref, nw_ref, w_ref,
                    qna_ref, kna_ref, qnb_ref, knb_ref, vscale_ref,
                    qa_ref, ka_ref, va_ref, ga_ref, qb_ref, kb_ref, vb_ref, gb_ref):
    bm = x_ref.shape[0]
    x = x_ref[...]
    ms = jnp.mean(x * x, axis=-1, keepdims=True)
    h = ((x * lax.rsqrt(ms + EPS)) * nw_ref[...]).astype(jnp.bfloat16)

    cos_blk, sin_blk = _rope_table_block(pos_ref[...], invf_ref[...], sign_ref[...])
    cos = jnp.tile(cos_blk, (1, CHUNK // LANES))
    sin_signed = jnp.tile(sin_blk, (1, CHUNK // LANES))
    lane = lax.broadcasted_iota(jnp.int32, (bm, CHUNK), 1)
    first_half = (lane % HEAD_DIM) < (HEAD_DIM // 2)
    head_group = lane // HEAD_DIM

    def proj(c0, width=CHUNK):
        return jnp.dot(h, w_ref[:, c0:c0 + width], preferred_element_type=jnp.float32)

    def norm_rope(t, w):
        t2 = t * t
        msq = None
        for g in range(CHUNK // HEAD_DIM):
            in_g = head_group == g
            part = jnp.sum(jnp.where(in_g, t2, 0.0), axis=-1, keepdims=True) * (1.0 / HEAD_DIM)
            msq = part if msq is None else jnp.where(in_g, part, msq)
        y = t * lax.rsqrt(msq + EPS) * w
        rot = jnp.where(first_half, pltpu.roll(y, CHUNK - HEAD_DIM // 2, 1), pltpu.roll(y, HEAD_DIM // 2, 1))
        return y * cos + rot * sin_signed

    def silu(t):
        half_t = 0.5 * t
        return half_t + half_t * jnp.tanh(half_t)

    def put_qa(t, o):
        y = norm_rope(t, qna_ref[...]).astype(jnp.bfloat16)
        first_comp = (lane % A_VDIM) < HEAD_DIM
        zero = jnp.zeros_like(y)
        qa_ref[0, :, o:o + CHUNK] = jnp.where(first_comp, y, zero)
        qa_ref[1, :, o:o + CHUNK] = jnp.where(first_comp, zero, y)

    def put_ka(t, o):
        ka_ref[:, o:o + CHUNK] = norm_rope(t, kna_ref[...]).astype(jnp.bfloat16)

    def put_va(t, o):
        vat = (t * vscale_ref[...]).T.astype(jnp.bfloat16)
        for j in range(CHUNK // A_VDIM):
            va_ref[o // A_VDIM + j, 0] = vat[j * A_VDIM:(j + 1) * A_VDIM, :]

    def put_ga(t, o):
        ga_ref[:, o:o + CHUNK] = silu(t).astype(jnp.bfloat16)

    def put_gb(t, o):
        gb_ref[:, o:o + CHUNK] = silu(t).astype(jnp.bfloat16)

    def put_qb(t, o):
        qb = norm_rope(t, qnb_ref[...]).astype(jnp.bfloat16)
        for j in range(CHUNK // HEAD_DIM):
            qb_ref[o // HEAD_DIM + j] = qb[:, j * HEAD_DIM:(j + 1) * HEAD_DIM]

    def put_kvb(kv, o):
        kbn = norm_rope(kv, knb_ref[...]).astype(jnp.bfloat16)
        for j in range(B_KV_HEADS):
            kb_ref[j] = kbn[:, j * HEAD_DIM:(j + 1) * HEAD_DIM]
        v_lo = lane < CHUNK - HEAD_DIM
        v0 = jnp.where(v_lo, kv, pltpu.roll(kv, HEAD_DIM, 1))[:, LANES:]
        v1 = jnp.where(v_lo, pltpu.roll(kv, CHUNK - HEAD_DIM, 1), kv)[:, LANES:]
        vb_ref[0] = jnp.concatenate([v0, v0], axis=1).astype(jnp.bfloat16)
        vb_ref[1] = jnp.concatenate([v1, v1], axis=1).astype(jnp.bfloat16)

    wide = 2 * CHUNK
    tasks = [(OFF_VA, put_va, 0, wide), (OFF_KB, put_kvb, 0, CHUNK)]
    for c in range(A_WIDTH // wide):
        o = c * wide
        tasks += [(OFF_QA + o, put_qa, o, wide), (OFF_KA + o, put_ka, o, wide), (OFF_QB + o, put_qb, o, wide)]
        if c >= 1:
            tasks.append((OFF_VA + o, put_va, o, wide))
        tasks += [(OFF_GA + o, put_ga, o, wide), (OFF_GB + o, put_gb, o, wide)]

    def finish(put, t, o, width):
        for k in range(width // CHUNK):
            put(t[:, k * CHUNK:(k + 1) * CHUNK], o + k * CHUNK)

    pending = None
    for col, put, o, width in tasks:
        t = proj(col, width)
        if pending is not None:
            finish(*pending)
        pending = (put, t, o, width)
    finish(*pending)


def _in_proj(x2, pos_rep, invf, sign, norm_w, w_bf16, qna, kna, qnb, knb, vscale):
    m = x2.shape[0]
    bm = PROJ_BM
    row = lambda i: (i, 0)
    const = lambda i: (0, 0)
    hm = lambda i: (0, i, 0)
    bf = jnp.bfloat16
    out_shape = (
        jax.ShapeDtypeStruct((2, m, A_WIDTH), bf),
        jax.ShapeDtypeStruct((m, A_WIDTH), bf),
        jax.ShapeDtypeStruct((A_HEADS, m // ATT_TQ, A_VDIM, ATT_TQ), bf),
        jax.ShapeDtypeStruct((m, A_WIDTH), bf),
        jax.ShapeDtypeStruct((B_HEADS, m, HEAD_DIM), bf),
        jax.ShapeDtypeStruct((B_KV_HEADS, m, HEAD_DIM), bf),
        jax.ShapeDtypeStruct((B_KV_HEADS, m, 4 * HEAD_DIM), bf),
        jax.ShapeDtypeStruct((m, B_WIDTH), bf),
    )
    out_specs = (
        pl.BlockSpec((2, bm, A_WIDTH), hm), pl.BlockSpec((bm, A_WIDTH), row),
        pl.BlockSpec((A_HEADS, 1, A_VDIM, bm), lambda i: (0, i // (ATT_TQ // bm), 0, i % (ATT_TQ // bm))),
        pl.BlockSpec((bm, A_WIDTH), row),
        pl.BlockSpec((B_HEADS, bm, HEAD_DIM), hm),
        pl.BlockSpec((B_KV_HEADS, bm, HEAD_DIM), hm),
        pl.BlockSpec((B_KV_HEADS, bm, 4 * HEAD_DIM), hm),
        pl.BlockSpec((bm, B_WIDTH), row),
    )
    in_specs = [
        pl.BlockSpec((bm, D_MODEL), row),
        pl.BlockSpec((bm // (LANES // (HEAD_DIM // 2)), LANES), row),
        pl.BlockSpec((1, LANES), const),
        pl.BlockSpec((1, LANES), const),
        pl.BlockSpec((1, D_MODEL), const),
        pl.BlockSpec((D_MODEL, PROJ_WIDTH), const, pipeline_mode=pl.Buffered(1)),
        pl.BlockSpec((1, CHUNK), const), pl.BlockSpec((1, CHUNK), const),
        pl.BlockSpec((1, CHUNK), const), pl.BlockSpec((1, CHUNK), const),
        pl.BlockSpec((1, 1), const),
    ]
    return pl.pallas_call(
        _in_proj_kernel,
        grid=(m // bm,),
        in_specs=in_specs,
        out_specs=out_specs,
        out_shape=out_shape,
        compiler_params=pltpu.CompilerParams(
            dimension_semantics=("arbitrary",), vmem_limit_bytes=IN_PROJ_VMEM_LIMIT),
        name="in_proj",
    )(x2, pos_rep, invf, sign, norm_w, w_bf16, qna, kna, qnb, knb, vscale)


def _diff_attn_kernel(q_ref, k_ref, vt_ref, g_ref, lq1_ref, lk1_ref, lq2_ref, lk2_ref, sw_ref, unscale_ref,
                      o_ref, acc_ref, l_ref, m_ref, *, lam_init, online_max):
    tq = q_ref.shape[1]
    qi = pl.program_id(2)

    def reset_accumulators():
        acc_ref[...] = jnp.zeros(acc_ref.shape, jnp.float32)
        l_ref[...] = jnp.zeros(l_ref.shape, jnp.float32)
        if online_max:
            m_ref[...] = jnp.full(m_ref.shape, NEG_INF, jnp.float32)

    @pl.when((pl.program_id(0) == 0) & (pl.program_id(1) == 0) & (qi == 0))
    def _():
        reset_accumulators()

    def scores(c, cols, j, nk, masked):
        k = k_ref[pl.ds(pl.multiple_of(j * tq, tq), nk), :]
        s = lax.dot_general(k, q_ref[c, cols, :], (((1,), (1,)), ((), ())), preferred_element_type=jnp.float32)
        if masked:
            key = lax.broadcasted_iota(jnp.int32, s.shape, 0)
            qry = lax.broadcasted_iota(jnp.int32, s.shape, 1)
            s = jnp.where(key <= qry + (nk - s.shape[1]), s, NEG_INF)
        return s

    def accumulate(s, c, cols, j):
        nk, nq_unit = s.shape
        vt = vt_ref[0, j, :, 0:nk]
        if online_max:
            m_old = m_ref[c, :, cols]
            m_new = jnp.maximum(m_old, jnp.max(s, axis=0, keepdims=True))
            alpha = jnp.exp2(m_old - m_new)
            m_ref[c, :, cols] = m_new
            p = jnp.exp2(s - m_new)
            pv = jnp.dot(vt, p.astype(jnp.bfloat16), preferred_element_type=jnp.float32)
            acc_ref[c, :, cols] = acc_ref[c, :, cols] * alpha + pv
            l_ref[c, :, cols] = (l_ref[c, :, cols] * alpha
                                 + p.reshape(nk // F32_SUBLANES, F32_SUBLANES, nq_unit).sum(axis=0))
        else:
            p = jnp.exp2(s)
            pv = jnp.dot(vt, p.astype(jnp.bfloat16), preferred_element_type=jnp.float32)
            acc_ref[c, :, cols] += pv
            l_ref[c, :, cols] += p.reshape(nk // F32_SUBLANES, F32_SUBLANES, nq_unit).sum(axis=0)

    def sweep(j, width, masked):
        order = reversed(range(tq // width)) if masked else range(tq // width)
        units = [(c, slice(r * width, (r + 1) * width)) for r in order for c in range(2)]
        nk_of = (lambda cols: cols.stop) if masked else (lambda cols: tq)
        depth = 2 if masked else 1
        queue = [scores(*units[k], j, nk_of(units[k][1]), masked) for k in range(min(depth, len(units)))]
        for idx, (c, cols) in enumerate(units):
            s = queue.pop(0)
            if idx + depth < len(units):
                cn, colsn = units[idx + depth]
                queue.append(scores(cn, colsn, j, nk_of(colsn), masked))
            accumulate(s, c, cols, j)

    def body(j, carry):
        sweep(j, ATT_QS, False)
        return carry

    lax.fori_loop(0, qi, body, 0)
    sweep(qi, ATT_QS_DIAG, True)

    lam = (jnp.exp(jnp.sum(lq1_ref[...] * lk1_ref[...], axis=-1, keepdims=True))
           - jnp.exp(jnp.sum(lq2_ref[...] * lk2_ref[...], axis=-1, keepdims=True)) + lam_init)
    l1 = jnp.sum(l_ref[0], axis=0, keepdims=True)
    l2 = jnp.sum(l_ref[1], axis=0, keepdims=True)
    ot = (acc_ref[0] / l1 - lam * (acc_ref[1] / l2)) * unscale_ref[...]
    o = ot.T
    o = o * lax.rsqrt(jnp.mean(o * o, axis=-1, keepdims=True) + EPS) * sw_ref[...] * (1.0 - lam_init)
    o_ref[...] = (o * g_ref[...].astype(jnp.float32)).astype(o_ref.dtype)
    reset_accumulators()


def _diff_attn(qa, ka, va_t, ga, lq1, lk1, lq2, lk2, subln_w, unscale, *, batch, seq, lam_init, online_max):
    tq = ATT_TQ
    nq = seq // tq
    qmap = lambda b, h, i: (b * nq + i, h)
    const = lambda b, h, i: (0, 0)
    vec = pl.BlockSpec((1, HEAD_DIM), const)
    return pl.pallas_call(
        functools.partial(_diff_attn_kernel, lam_init=lam_init, online_max=online_max),
        grid=(batch, A_HEADS, nq),
        in_specs=[
            pl.BlockSpec((2, tq, A_VDIM), lambda b, h, i: (0, b * nq + i, h)),
            pl.BlockSpec((seq, A_VDIM), lambda b, h, i: (b, h)),
            pl.BlockSpec((1, nq, A_VDIM, tq), lambda b, h, i: (h, b, 0, 0)),
            pl.BlockSpec((tq, A_VDIM), qmap),
            vec, vec, vec, vec,
            pl.BlockSpec((1, A_VDIM), const),
            pl.BlockSpec((1, 1), const),
        ],
        out_specs=pl.BlockSpec((tq, A_VDIM), qmap),
        out_shape=jax.ShapeDtypeStruct((batch * seq, A_WIDTH), jnp.bfloat16),
        scratch_shapes=[
            pltpu.VMEM((2, A_VDIM, tq), jnp.float32),
            pltpu.VMEM((2, F32_SUBLANES, tq), jnp.float32),
            pltpu.VMEM((2, 1, tq), jnp.float32),
        ],
        compiler_params=pltpu.CompilerParams(
            dimension_semantics=("arbitrary", "arbitrary", "arbitrary"), vmem_limit_bytes=V7X_VMEM_LIMIT),
        name="diff_attn_online" if online_max else "diff_attn",
    )(qa, ka, va_t, ga, lq1, lk1, lq2, lk2, subln_w, unscale)


def _swa_block(q, k, v, sink_ref, kv, prev_bias):
    blk = WINDOW
    per_half = CHUNK // HEAD_DIM
    own = (lax.broadcasted_iota(jnp.int32, (blk, blk), 1) <= lax.broadcasted_iota(jnp.int32, (blk, blk), 0))
    vgroup = lax.broadcasted_iota(jnp.int32, (2 * blk, CHUNK), 1) // HEAD_DIM
    ogroup = lax.broadcasted_iota(jnp.int32, (blk, CHUNK), 1) // HEAD_DIM
    zero = jnp.zeros((blk, blk), jnp.bfloat16)
    s_all = lax.dot_general(q, k, (((1,), (1,)), ((), ())), preferred_element_type=jnp.float32)
    vstack = jnp.concatenate([jnp.where(vgroup == j, v, jnp.zeros_like(v)) for j in range(per_half)], axis=0)
    halves = []
    for hf in range(B_GROUP // per_half):
        probs = []
        scale = None
        for j in range(per_half):
            g = hf * per_half + j
            sink = sink_ref[kv, g * blk:(g + 1) * blk, :]
            s_prev = s_all[g * blk:(g + 1) * blk, :blk]
            if prev_bias is not None:
                s_prev = s_prev + prev_bias
            s = jnp.where(own, s_all[g * blk:(g + 1) * blk, blk:], s_prev)
            m = jnp.maximum(sink, jnp.max(s, axis=-1, keepdims=True))
            p = jnp.exp2(s - m)
            denom = jnp.exp2(sink - m) + jnp.sum(p, axis=-1, keepdims=True)
            pb = p.astype(jnp.bfloat16)
            probs += [jnp.where(own, zero, pb), jnp.where(own, pb, zero)]
            rg = jnp.tile(1.0 / denom, (1, 2))
            scale = rg if scale is None else jnp.where(ogroup == j, rg, scale)
        acc = jnp.dot(jnp.concatenate(probs, axis=1), vstack, preferred_element_type=jnp.float32)
        halves.append(acc * scale)
    return jnp.concatenate(halves, axis=1)


def _out_swa_kernel(ya_ref, q_ref, kp_ref, kc_ref, vp_ref, vc_ref, sink_ref, g_ref, w_ref, x_ref, o_ref, yb_ref,
                    *, steps_per_seq):
    bm = x_ref.shape[0]
    blk = WINDOW
    nblk = bm // blk
    gw = B_GROUP * HEAD_DIM
    has_prev = (pl.program_id(0) % steps_per_seq) > 0
    first_bias = jnp.where(has_prev, 0.0, NEG_INF)

    def swa(kv, n):
        q = q_ref[kv * B_GROUP:(kv + 1) * B_GROUP, n * blk:(n + 1) * blk, :].reshape(B_GROUP * blk, HEAD_DIM)
        if n == 0:
            k = jnp.concatenate([kp_ref[kv], kc_ref[kv, 0:blk, :]], axis=0)
            v = jnp.concatenate([vp_ref[kv], vc_ref[kv, 0:blk, :]], axis=0)
            bias = first_bias
        else:
            k = kc_ref[kv, (n - 1) * blk:(n + 1) * blk, :]
            v = vc_ref[kv, (n - 1) * blk:(n + 1) * blk, :]
            bias = None
        o = _swa_block(q, k, v, sink_ref, kv, bias)
        gate = g_ref[n * blk:(n + 1) * blk, kv * gw:(kv + 1) * gw].astype(jnp.float32)
        yb_ref[n * blk:(n + 1) * blk, kv * gw:(kv + 1) * gw] = (o * gate).astype(yb_ref.dtype)

    width = 2 * CHUNK
    assert nblk == D_MODEL // width

    def project(lhs, rows, first):
        def slab(idx):
            cs = slice(idx * width, (idx + 1) * width)
            part = jnp.dot(lhs, w_ref[rows, cs], preferred_element_type=jnp.float32)
            if first:
                o_ref[:, cs] = x_ref[:, cs] + part
            else:
                o_ref[:, cs] += part
        return slab

    slab_a = project(ya_ref[...], slice(0, A_WIDTH), True)
    for n in range(nblk):
        slab_a(n)
        swa(0, n)
    for kv in range(B_KV_HEADS):
        rows = slice(A_WIDTH + kv * gw, A_WIDTH + (kv + 1) * gw)
        slab_b = project(yb_ref[:, kv * gw:(kv + 1) * gw], rows, False)
        for n in range(nblk):
            slab_b(n)
            if kv + 1 < B_KV_HEADS:
                swa(kv + 1, n)


def _out_swa(ya, qb, kb, vb_rep, gb, sink_rep, w_o, x2, seq):
    m = x2.shape[0]
    bm = OUT_BM
    r = bm // WINDOW
    row = lambda i: (i, 0)
    cur = lambda i: (0, i, 0)
    prev = lambda i: (0, jnp.maximum(i * r - 1, 0), 0)
    return pl.pallas_call(
        functools.partial(_out_swa_kernel, steps_per_seq=seq // bm),
        grid=(m // bm,),
        in_specs=[
            pl.BlockSpec((bm, A_WIDTH), row),
            pl.BlockSpec((B_HEADS, bm, HEAD_DIM), cur),
            pl.BlockSpec((B_KV_HEADS, WINDOW, HEAD_DIM), prev),
            pl.BlockSpec((B_KV_HEADS, bm, HEAD_DIM), cur),
            pl.BlockSpec((B_KV_HEADS, WINDOW, CHUNK), prev),
            pl.BlockSpec((B_KV_HEADS, bm, CHUNK), cur),
            pl.BlockSpec((B_KV_HEADS, B_GROUP * WINDOW, LANES), lambda i: (0, 0, 0)),
            pl.BlockSpec((bm, B_WIDTH), row),
            pl.BlockSpec((A_WIDTH + B_WIDTH, D_MODEL), lambda i: (0, 0), pipeline_mode=pl.Buffered(1)),
            pl.BlockSpec((bm, D_MODEL), row),
        ],
        out_specs=pl.BlockSpec((bm, D_MODEL), row),
        out_shape=jax.ShapeDtypeStruct((m, D_MODEL), jnp.float32),
        scratch_shapes=[pltpu.VMEM((bm, B_WIDTH), jnp.bfloat16)],
        compiler_params=pltpu.CompilerParams(
            dimension_semantics=("arbitrary",), vmem_limit_bytes=V7X_VMEM_LIMIT),
        name="out_swa",
    )(ya, qb, kb, kb, vb_rep, vb_rep, sink_rep, gb, w_o, x2)


def kernel(x, positions, norm_w, w_in, q_norm_a, k_norm_a, lambda_q1, lambda_k1, lambda_q2, lambda_k2,
           subln_w, q_norm_b, k_norm_b, sinks, w_out):
    batch, seq, _ = x.shape
    depth = norm_w.shape[0]
    m = batch * seq
    f32 = jnp.float32

    half = HEAD_DIM // 2
    inv_freq = ROPE_THETA ** (-(jnp.arange(0, HEAD_DIM, 2, dtype=f32) / HEAD_DIM))
    invf = jnp.tile(inv_freq, LANES // half).reshape(1, LANES)
    sign = jnp.tile(jnp.concatenate([-jnp.ones((half,), f32), jnp.ones((half,), f32)]),
                    LANES // HEAD_DIM).reshape(1, LANES)
    pos_rep = _rope_positions(positions, m)
    qk_scale = HEAD_DIM ** -0.5 * LOG2E
    tile = lambda w: jnp.tile(w.astype(f32), CHUNK // HEAD_DIM).reshape(1, CHUNK)

    x2 = x.reshape(m, D_MODEL)
    for layer in range(depth):
        lam_init = 0.8 - 0.6 * math.exp(-0.3 * layer)
        w_bf16 = w_in[layer].astype(jnp.bfloat16)
        score_bound = (HEAD_DIM * qk_scale * jnp.max(jnp.abs(q_norm_a[layer].astype(f32)))
                       * jnp.max(jnp.abs(k_norm_a[layer].astype(f32))))
        raw_exp = score_bound <= RAW_EXP_SCORE_BOUND
        v_shift = jnp.where(raw_exp, jnp.ceil(score_bound), 0.0).astype(jnp.int32).reshape(1, 1)
        one = jnp.ones((1, 1), f32)
        qa, ka, va_t, ga, qb, kb, vb_rep, gb = _in_proj(
            x2, pos_rep, invf, sign, norm_w[layer].reshape(1, D_MODEL), w_bf16,
            tile(q_norm_a[layer]) * qk_scale, tile(k_norm_a[layer]),
            tile(q_norm_b[layer]) * qk_scale, tile(k_norm_b[layer]), jnp.ldexp(one, -v_shift))
        vec = lambda p: p[layer].astype(f32).reshape(1, HEAD_DIM)
        attn_args = (qa, ka, va_t, ga, vec(lambda_q1), vec(lambda_k1), vec(lambda_q2), vec(lambda_k2),
                     subln_w[layer].astype(f32).reshape(1, A_VDIM), jnp.ldexp(one, v_shift))
        attn = functools.partial(_diff_attn, batch=batch, seq=seq, lam_init=lam_init)
        ya = lax.cond(raw_exp,
                      functools.partial(attn, online_max=False),
                      functools.partial(attn, online_max=True), *attn_args)
        sink_rep = jnp.broadcast_to(
            jnp.repeat(sinks[layer].astype(f32) * LOG2E, WINDOW).reshape(B_KV_HEADS, B_GROUP * WINDOW, 1),
            (B_KV_HEADS, B_GROUP * WINDOW, LANES))
        x2 = _out_swa(ya, qb, kb, vb_rep, gb, sink_rep, w_out[layer].astype(jnp.bfloat16), x2, seq)
    return x2.reshape(batch, seq, D_MODEL)
```

```python
import functools
import math

import jax
import jax.numpy as jnp
from jax import lax
from jax.experimental import pallas as pl
from jax.experimental.pallas import tpu as pltpu

D_MODEL = 2048
HEAD_DIM = 64
ROPE_THETA = 10000.0
EPS = 1e-6
NEG_INF = -1e30
LOG2E = math.log2(math.e)

A_HEADS = 8
A_VDIM = 128
A_WIDTH = 1024
B_HEADS = 16
B_KV_HEADS = 2
B_GROUP = 8
B_WIDTH = 1024
WINDOW = 128
PROJ_WIDTH = 6400

OFF_QA, OFF_KA, OFF_VA, OFF_GA, OFF_QB, OFF_KB, OFF_VB, OFF_GB = 0, 1024, 2048, 3072, 4096, 5120, 5248, 5376

V7X_VMEM_LIMIT = 56 * 1024 * 1024
IN_PROJ_VMEM_LIMIT = 60 * 1024 * 1024
LANES = 128
CHUNK = 256

PROJ_BM = 512
ATT_TQ = 2048
ATT_QS = 1024
ATT_QS_DIAG = 256
OUT_BM = 512
F32_SUBLANES = 8

RAW_EXP_SCORE_BOUND = 60.0


def _rope_table_block(pos, invf, sign):
    half = HEAD_DIM // 2
    groups = LANES // half
    ang = pos * invf
    group = lax.broadcasted_iota(jnp.int32, ang.shape, 1) // half
    tables = []
    for src, sgn in ((jnp.cos(ang), None), (jnp.sin(ang), sign)):
        rolled = [src] + [pltpu.roll(src, half * k, 1) for k in range(1, groups)]
        slabs = []
        for t in range(groups):
            out = rolled[(0 - t) % groups]
            for g in range(1, groups):
                out = jnp.where(group == g, rolled[(g - t) % groups], out)
            slabs.append(out if sgn is None else out * sgn)
        tables.append(jnp.concatenate(slabs, axis=0))
    return tables


def _rope_positions(positions, m):
    half = HEAD_DIM // 2
    groups = LANES // half
    rows = PROJ_BM // groups
    pos = positions.reshape(m // PROJ_BM, groups, rows).astype(jnp.float32)
    return jnp.repeat(jnp.swapaxes(pos, 1, 2), half, axis=2).reshape(m // groups, LANES)


def _in_proj_kernel(x_ref, pos_ref, invf_ref, sign_ref, nw_ref, w_ref,
                    qna_ref, kna_ref, qnb_ref, knb_ref, vscale_ref,
                    qa_ref, ka_ref, va_ref, ga_ref, qb_ref, kb_ref, vb_ref, gb_ref):
    bm = x_ref.shape[0]
    x = x_ref[...]
    ms = jnp.mean(x * x, axis=-1, keepdims=True)
    h = ((x * lax.rsqrt(ms + EPS)) * nw_ref[...]).astype(jnp.bfloat16)

    cos_blk, sin_blk = _rope_table_block(pos_ref[...], invf_ref[...], sign_ref[...])
    cos = jnp.tile(cos_blk, (1, CHUNK // LANES))
    sin_signed = jnp.tile(sin_blk, (1, CHUNK // LANES))
    lane = lax.broadcasted_iota(jnp.int32, (bm, CHUNK), 1)
    first_half = (lane % HEAD_DIM) < (HEAD_DIM // 2)
    head_group = lane // HEAD_DIM

    def proj(c0, width=CHUNK):
        return jnp.dot(h, w_ref[:, c0:c0 + width], preferred_element_type=jnp.float32)

    def norm_rope(t, w):
        t2 = t * t
        msq = None
        for g in range(CHUNK // HEAD_DIM):
            in_g = head_group == g
            part = jnp.sum(jnp.where(in_g, t2, 0.0), axis=-1, keepdims=True) * (1.0 / HEAD_DIM)
            msq = part if msq is None else jnp.where(in_g, part, msq)
        y = t * lax.rsqrt(msq + EPS) * w
        rot = jnp.where(first_half, pltpu.roll(y, CHUNK - HEAD_DIM // 2, 1), pltpu.roll(y, HEAD_DIM // 2, 1))
        return y * cos + rot * sin_signed

    def silu(t):
        half_t = 0.5 * t
        return half_t + half_t * jnp.tanh(half_t)

    def put_qa(t, o):
        y = norm_rope(t, qna_ref[...]).astype(jnp.bfloat16)
        first_comp = (lane % A_VDIM) < HEAD_DIM
        zero = jnp.zeros_like(y)
        qa_ref[0, :, o:o + CHUNK] = jnp.where(first_comp, y, zero)
        qa_ref[1, :, o:o + CHUNK] = jnp.where(first_comp, zero, y)

    def put_ka(t, o):
        ka_ref[:, o:o + CHUNK] = norm_rope(t, kna_ref[...]).astype(jnp.bfloat16)

    def put_va(t, o):
        vat = (t * vscale_ref[...]).T.astype(jnp.bfloat16)
        for j in range(CHUNK // A_VDIM):
            va_ref[o // A_VDIM + j, 0] = vat[j * A_VDIM:(j + 1) * A_VDIM, :]

    def put_ga(t, o):
        ga_ref[:, o:o + CHUNK] = silu(t).astype(jnp.bfloat16)

    def put_gb(t, o):
        gb_ref[:, o:o + CHUNK] = silu(t).astype(jnp.bfloat16)

    def put_qb(t, o):
        qb = norm_rope(t, qnb_ref[...]).astype(jnp.bfloat16)
        for j in range(CHUNK // HEAD_DIM):
            qb_ref[o // HEAD_DIM + j] = qb[:, j * HEAD_DIM:(j + 1) * HEAD_DIM]

    def put_kvb(kv, o):
        kbn = norm_rope(kv, knb_ref[...]).astype(jnp.bfloat16)
        for j in range(B_KV_HEADS):
            kb_ref[j] = kbn[:, j * HEAD_DIM:(j + 1) * HEAD_DIM]
        v_lo = lane < CHUNK - HEAD_DIM
        v0 = jnp.where(v_lo, kv, pltpu.roll(kv, HEAD_DIM, 1))[:, LANES:]
        v1 = jnp.where(v_lo, pltpu.roll(kv, CHUNK - HEAD_DIM, 1), kv)[:, LANES:]
        vb_ref[0] = jnp.concatenate([v0, v0], axis=1).astype(jnp.bfloat16)
        vb_ref[1] = jnp.concatenate([v1, v1], axis=1).astype(jnp.bfloat16)

    wide = 4 * CHUNK
    tasks = [(OFF_VA, put_va, 0, wide), (OFF_KB, put_kvb, 0, CHUNK)]
    for c in range(A_WIDTH // wide):
        o = c * wide
        tasks += [(OFF_QA + o, put_qa, o, wide), (OFF_KA + o, put_ka, o, wide), (OFF_QB + o, put_qb, o, wide)]
        tasks += [(OFF_GA + o, put_ga, o, wide), (OFF_GB + o, put_gb, o, wide)]

    def finish(put, t, o, width):
        for k in range(width // CHUNK):
            put(t[:, k * CHUNK:(k + 1) * CHUNK], o + k * CHUNK)

    pending = None
    for col, put, o, width in tasks:
        t = proj(col, width)
        if pending is not None:
            finish(*pending)
        pending = (put, t, o, width)
    finish(*pending)


def _in_proj(x2, pos_rep, invf, sign, norm_w, w_bf16, qna, kna, qnb, knb, vscale):
    m = x2.shape[0]
    bm = PROJ_BM
    row = lambda i: (i, 0)
    const = lambda i: (0, 0)
    hm = lambda i: (0, i, 0)
    bf = jnp.bfloat16
    out_shape = (
        jax.ShapeDtypeStruct((2, m, A_WIDTH), bf),
        jax.ShapeDtypeStruct((m, A_WIDTH), bf),
        jax.ShapeDtypeStruct((A_HEADS, m // ATT_TQ, A_VDIM, ATT_TQ), bf),
        jax.ShapeDtypeStruct((m, A_WIDTH), bf),
        jax.ShapeDtypeStruct((B_HEADS, m, HEAD_DIM), bf),
        jax.ShapeDtypeStruct((B_KV_HEADS, m, HEAD_DIM), bf),
        jax.ShapeDtypeStruct((B_KV_HEADS, m, 4 * HEAD_DIM), bf),
        jax.ShapeDtypeStruct((m, B_WIDTH), bf),
    )
    out_specs = (
        pl.BlockSpec((2, bm, A_WIDTH), hm), pl.BlockSpec((bm, A_WIDTH), row),
        pl.BlockSpec((A_HEADS, 1, A_VDIM, bm), lambda i: (0, i // (ATT_TQ // bm), 0, i % (ATT_TQ // bm))),
        pl.BlockSpec((bm, A_WIDTH), row),
        pl.BlockSpec((B_HEADS, bm, HEAD_DIM), hm),
        pl.BlockSpec((B_KV_HEADS, bm, HEAD_DIM), hm),
        pl.BlockSpec((B_KV_HEADS, bm, 4 * HEAD_DIM), hm),
        pl.BlockSpec((bm, B_WIDTH), row),
    )
    in_specs = [
        pl.BlockSpec((bm, D_MODEL), row),
        pl.BlockSpec((bm // (LANES // (HEAD_DIM // 2)), LANES), row),
        pl.BlockSpec((1, LANES), const),
        pl.BlockSpec((1, LANES), const),
        pl.BlockSpec((1, D_MODEL), const),
        pl.BlockSpec((D_MODEL, PROJ_WIDTH), const, pipeline_mode=pl.Buffered(1)),
        pl.BlockSpec((1, CHUNK), const), pl.BlockSpec((1, CHUNK), const),
        pl.BlockSpec((1, CHUNK), const), pl.BlockSpec((1, CHUNK), const),
        pl.BlockSpec((1, 1), const),
    ]
    return pl.pallas_call(
        _in_proj_kernel,
        grid=(m // bm,),
        in_specs=in_specs,
        out_specs=out_specs,
        out_shape=out_shape,
        compiler_params=pltpu.CompilerParams(
            dimension_semantics=("arbitrary",), vmem_limit_bytes=IN_PROJ_VMEM_LIMIT),
        name="in_proj",
    )(x2, pos_rep, invf, sign, norm_w, w_bf16, qna, kna, qnb, knb, vscale)


def _diff_attn_kernel(q_ref, k_ref, vt_ref, g_ref, lq1_ref, lk1_ref, lq2_ref, lk2_ref, sw_ref, unscale_ref,
                      o_ref, acc_ref, l_ref, m_ref, *, lam_init, online_max):
    tq = q_ref.shape[1]
    qi = pl.program_id(2)

    def reset_accumulators():
        acc_ref[...] = jnp.zeros(acc_ref.shape, jnp.float32)
        l_ref[...] = jnp.zeros(l_ref.shape, jnp.float32)
        if online_max:
            m_ref[...] = jnp.full(m_ref.shape, NEG_INF, jnp.float32)

    @pl.when((pl.program_id(0) == 0) & (pl.program_id(1) == 0) & (qi == 0))
    def _():
        reset_accumulators()

    def scores(c, cols, j, nk, masked):
        k = k_ref[pl.ds(pl.multiple_of(j * tq, tq), nk), :]
        s = lax.dot_general(k, q_ref[c, cols, :], (((1,), (1,)), ((), ())), preferred_element_type=jnp.float32)
        if masked:
            key = lax.broadcasted_iota(jnp.int32, s.shape, 0)
            qry = lax.broadcasted_iota(jnp.int32, s.shape, 1)
            s = jnp.where(key <= qry + (nk - s.shape[1]), s, NEG_INF)
        return s

    def accumulate(s, c, cols, j):
        nk, nq_unit = s.shape
        vt = vt_ref[0, j, :, 0:nk]
        if online_max:
            m_old = m_ref[c, :, cols]
            m_new = jnp.maximum(m_old, jnp.max(s, axis=0, keepdims=True))
            alpha = jnp.exp2(m_old - m_new)
            m_ref[c, :, cols] = m_new
            p = jnp.exp2(s - m_new)
            pv = jnp.dot(vt, p.astype(jnp.bfloat16), preferred_element_type=jnp.float32)
            acc_ref[c, :, cols] = acc_ref[c, :, cols] * alpha + pv
            l_ref[c, :, cols] = (l_ref[c, :, cols] * alpha
                                 + p.reshape(nk // F32_SUBLANES, F32_SUBLANES, nq_unit).sum(axis=0))
        else:
            p = jnp.exp2(s)
            pv = jnp.dot(vt, p.astype(jnp.bfloat16), preferred_element_type=jnp.float32)
            acc_ref[c, :, cols] += pv
            l_ref[c, :, cols] += p.reshape(nk // F32_SUBLANES, F32_SUBLANES, nq_unit).sum(axis=0)

    def sweep(j, width, masked):
        order = reversed(range(tq // width)) if masked else range(tq // width)
        units = [(c, slice(r * width, (r + 1) * width)) for r in order for c in range(2)]
        nk_of = (lambda cols: cols.stop) if masked else (lambda cols: tq)
        depth = 2
        queue = [scores(*units[k], j, nk_of(units[k][1]), masked) for k in range(min(depth, len(units)))]
        for idx, (c, cols) in enumerate(units):
            s = queue.pop(0)
            if idx + depth < len(units):
                cn, colsn = units[idx + depth]
                queue.append(scores(cn, colsn, j, nk_of(colsn), masked))
            accumulate(s, c, cols, j)

    def body(j, carry):
        sweep(j, ATT_QS, False)
        return carry

    lax.fori_loop(0, qi, body, 0)
    sweep(qi, ATT_QS_DIAG, True)

    lam = (jnp.exp(jnp.sum(lq1_ref[...] * lk1_ref[...], axis=-1, keepdims=True))
           - jnp.exp(jnp.sum(lq2_ref[...] * lk2_ref[...], axis=-1, keepdims=True)) + lam_init)
    l1 = jnp.sum(l_ref[0], axis=0, keepdims=True)
    l2 = jnp.sum(l_ref[1], axis=0, keepdims=True)
    ot = (acc_ref[0] / l1 - lam * (acc_ref[1] / l2)) * unscale_ref[...]
    o = ot.T
    o = o * lax.rsqrt(jnp.mean(o * o, axis=-1, keepdims=True) + EPS) * sw_ref[...] * (1.0 - lam_init)
    o_ref[...] = (o * g_ref[...].astype(jnp.float32)).astype(o_ref.dtype)
    reset_accumulators()


def _diff_attn(qa, ka, va_t, ga, lq1, lk1, lq2, lk2, subln_w, unscale, *, batch, seq, lam_init, online_max):
    tq = ATT_TQ
    nq = seq // tq
    qmap = lambda b, h, i: (b * nq + i, h)
    const = lambda b, h, i: (0, 0)
    vec = pl.BlockSpec((1, HEAD_DIM), const)
    return pl.pallas_call(
        functools.partial(_diff_attn_kernel, lam_init=lam_init, online_max=online_max),
        grid=(batch, A_HEADS, nq),
        in_specs=[
            pl.BlockSpec((2, tq, A_VDIM), lambda b, h, i: (0, b * nq + i, h)),
            pl.BlockSpec((seq, A_VDIM), lambda b, h, i: (b, h)),
            pl.BlockSpec((1, nq, A_VDIM, tq), lambda b, h, i: (h, b, 0, 0)),
            pl.BlockSpec((tq, A_VDIM), qmap),
            vec, vec, vec, vec,
            pl.BlockSpec((1, A_VDIM), const),
            pl.BlockSpec((1, 1), const),
        ],
        out_specs=pl.BlockSpec((tq, A_VDIM), qmap),
        out_shape=jax.ShapeDtypeStruct((batch * seq, A_WIDTH), jnp.bfloat16),
        scratch_shapes=[
            pltpu.VMEM((2, A_VDIM, tq), jnp.float32),
            pltpu.VMEM((2, F32_SUBLANES, tq), jnp.float32),
            pltpu.VMEM((2, 1, tq), jnp.float32),
        ],
        compiler_params=pltpu.CompilerParams(
            dimension_semantics=("arbitrary", "arbitrary", "arbitrary"), vmem_limit_bytes=V7X_VMEM_LIMIT),
        name="diff_attn_online" if online_max else "diff_attn",
    )(qa, ka, va_t, ga, lq1, lk1, lq2, lk2, subln_w, unscale)


def _swa_block(q, k, v, sink_ref, kv, prev_bias):
    blk = WINDOW
    per_half = CHUNK // HEAD_DIM
    own = (lax.broadcasted_iota(jnp.int32, (blk, blk), 1) <= lax.broadcasted_iota(jnp.int32, (blk, blk), 0))
    vgroup = lax.broadcasted_iota(jnp.int32, (2 * blk, CHUNK), 1) // HEAD_DIM
    ogroup = lax.broadcasted_iota(jnp.int32, (blk, CHUNK), 1) // HEAD_DIM
    zero = jnp.zeros((blk, blk), jnp.bfloat16)
    s_all = lax.dot_general(q, k, (((1,), (1,)), ((), ())), preferred_element_type=jnp.float32)
    vstack = jnp.concatenate([jnp.where(vgroup == j, v, jnp.zeros_like(v)) for j in range(per_half)], axis=0)
    halves = []
    for hf in range(B_GROUP // per_half):
        probs = []
        scale = None
        for j in range(per_half):
            g = hf * per_half + j
            sink = sink_ref[kv, g * blk:(g + 1) * blk, :]
            s_prev = s_all[g * blk:(g + 1) * blk, :blk]
            if prev_bias is not None:
                s_prev = s_prev + prev_bias
            s = jnp.where(own, s_all[g * blk:(g + 1) * blk, blk:], s_prev)
            m = jnp.maximum(sink, jnp.max(s, axis=-1, keepdims=True))
            p = jnp.exp2(s - m)
            denom = jnp.exp2(sink - m) + jnp.sum(p, axis=-1, keepdims=True)
            pb = p.astype(jnp.bfloat16)
            probs += [jnp.where(own, zero, pb), jnp.where(own, pb, zero)]
            rg = jnp.tile(1.0 / denom, (1, 2))
            scale = rg if scale is None else jnp.where(ogroup == j, rg, scale)
        acc = jnp.dot(jnp.concatenate(probs, axis=1), vstack, preferred_element_type=jnp.float32)
        halves.append(acc * scale)
    return jnp.concatenate(halves, axis=1)


def _out_swa_kernel(ya_ref, q_ref, kp_ref, kc_ref, vp_ref, vc_ref, sink_ref, g_ref, w_ref, x_ref, o_ref, yb_ref,
                    *, steps_per_seq):
    bm = x_ref.shape[0]
    blk = WINDOW
    nblk = bm // blk
    gw = B_GROUP * HEAD_DIM
    has_prev = (pl.program_id(0) % steps_per_seq) > 0
    first_bias = jnp.where(has_prev, 0.0, NEG_INF)

    def swa(kv, n):
        q = q_ref[kv * B_GROUP:(kv + 1) * B_GROUP, n * blk:(n + 1) * blk, :].reshape(B_GROUP * blk, HEAD_DIM)
        if n == 0:
            k = jnp.concatenate([kp_ref[kv], kc_ref[kv, 0:blk, :]], axis=0)
            v = jnp.concatenate([vp_ref[kv], vc_ref[kv, 0:blk, :]], axis=0)
            bias = first_bias
        else:
            k = kc_ref[kv, (n - 1) * blk:(n + 1) * blk, :]
            v = vc_ref[kv, (n - 1) * blk:(n + 1) * blk, :]
            bias = None
        o = _swa_block(q, k, v, sink_ref, kv, bias)
        gate = g_ref[n * blk:(n + 1) * blk, kv * gw:(kv + 1) * gw].astype(jnp.float32)
        yb_ref[n * blk:(n + 1) * blk, kv * gw:(kv + 1) * gw] = (o * gate).astype(yb_ref.dtype)

    width = 2 * CHUNK
    assert nblk == D_MODEL // width

    def project(lhs, rows, first):
        def slab(idx):
            cs = slice(idx * width, (idx + 1) * width)
            part = jnp.dot(lhs, w_ref[rows, cs], preferred_element_type=jnp.float32)
            if first:
                o_ref[:, cs] = x_ref[:, cs] + part
            else:
                o_ref[:, cs] += part
        return slab

    slab_a = project(ya_ref[...], slice(0, A_WIDTH), True)
    for n in range(nblk):
        slab_a(n)
        swa(0, n)
    for kv in range(B_KV_HEADS):
        rows = slice(A_WIDTH + kv * gw, A_WIDTH + (kv + 1) * gw)
        slab_b = project(yb_ref[:, kv * gw:(kv + 1) * gw], rows, False)
        for n in range(nblk):
            slab_b(n)
            if kv + 1 < B_KV_HEADS:
                swa(kv + 1, n)


def _out_swa(ya, qb, kb, vb_rep, gb, sink_rep, w_o, x2, seq):
    m = x2.shape[0]
    bm = OUT_BM
    r = bm // WINDOW
    row = lambda i: (i, 0)
    cur = lambda i: (0, i, 0)
    prev = lambda i: (0, jnp.maximum(i * r - 1, 0), 0)
    return pl.pallas_call(
        functools.partial(_out_swa_kernel, steps_per_seq=seq // bm),
        grid=(m // bm,),
        in_specs=[
            pl.BlockSpec((bm, A_WIDTH), row),
            pl.BlockSpec((B_HEADS, bm, HEAD_DIM), cur),
            pl.BlockSpec((B_KV_HEADS, WINDOW, HEAD_DIM), prev),
            pl.BlockSpec((B_KV_HEADS, bm, HEAD_DIM), cur),
            pl.BlockSpec((B_KV_HEADS, WINDOW, CHUNK), prev),
            pl.BlockSpec((B_KV_HEADS, bm, CHUNK), cur),
            pl.BlockSpec((B_KV_HEADS, B_GROUP * WINDOW, LANES), lambda i: (0, 0, 0)),
            pl.BlockSpec((bm, B_WIDTH), row),
            pl.BlockSpec((A_WIDTH + B_WIDTH, D_MODEL), lambda i: (0, 0), pipeline_mode=pl.Buffered(1)),
            pl.BlockSpec((bm, D_MODEL), row),
        ],
        out_specs=pl.BlockSpec((bm, D_MODEL), row),
        out_shape=jax.ShapeDtypeStruct((m, D_MODEL), jnp.float32),
        scratch_shapes=[pltpu.VMEM((bm, B_WIDTH), jnp.bfloat16)],
        compiler_params=pltpu.CompilerParams(
            dimension_semantics=("arbitrary",), vmem_limit_bytes=V7X_VMEM_LIMIT),
        name="out_swa",
    )(ya, qb, kb, kb, vb_rep, vb_rep, sink_rep, gb, w_o, x2)


def kernel(x, positions, norm_w, w_in, q_norm_a, k_norm_a, lambda_q1, lambda_k1, lambda_q2, lambda_k2,
           subln_w, q_norm_b, k_norm_b, sinks, w_out):
    batch, seq, _ = x.shape
    depth = norm_w.shape[0]
    m = batch * seq
    f32 = jnp.float32

    half = HEAD_DIM // 2
    inv_freq = ROPE_THETA ** (-(jnp.arange(0, HEAD_DIM, 2, dtype=f32) / HEAD_DIM))
    invf = jnp.tile(inv_freq, LANES // half).reshape(1, LANES)
    sign = jnp.tile(jnp.concatenate([-jnp.ones((half,), f32), jnp.ones((half,), f32)]),
                    LANES // HEAD_DIM).reshape(1, LANES)
    pos_rep = _rope_positions(positions, m)
    qk_scale = HEAD_DIM ** -0.5 * LOG2E
    tile = lambda w: jnp.tile(w.astype(f32), CHUNK // HEAD_DIM).reshape(1, CHUNK)

    x2 = x.reshape(m, D_MODEL)
    for layer in range(depth):
        lam_init = 0.8 - 0.6 * math.exp(-0.3 * layer)
        w_bf16 = w_in[layer].astype(jnp.bfloat16)
        score_bound = (HEAD_DIM * qk_scale * jnp.max(jnp.abs(q_norm_a[layer].astype(f32)))
                       * jnp.max(jnp.abs(k_norm_a[layer].astype(f32))))
        raw_exp = score_bound <= RAW_EXP_SCORE_BOUND
        v_shift = jnp.where(raw_exp, jnp.ceil(score_bound), 0.0).astype(jnp.int32).reshape(1, 1)
        one = jnp.ones((1, 1), f32)
        qa, ka, va_t, ga, qb, kb, vb_rep, gb = _in_proj(
            x2, pos_rep, invf, sign, norm_w[layer].reshape(1, D_MODEL), w_bf16,
            tile(q_norm_a[layer]) * qk_scale, tile(k_norm_a[layer]),
            tile(q_norm_b[layer]) * qk_scale, tile(k_norm_b[layer]), jnp.ldexp(one, -v_shift))
        vec = lambda p: p[layer].astype(f32).reshape(1, HEAD_DIM)
        attn_args = (qa, ka, va_t, ga, vec(lambda_q1), vec(lambda_k1), vec(lambda_q2), vec(lambda_k2),
                     subln_w[layer].astype(f32).reshape(1, A_VDIM), jnp.ldexp(one, v_shift))
        attn = functools.partial(_diff_attn, batch=batch, seq=seq, lam_init=lam_init)
        ya = lax.cond(raw_exp,
                      functools.partial(attn, online_max=False),
                      functools.partial(attn, online_max=True), *attn_args)
        sink_rep = jnp.broadcast_to(
            jnp.repeat(sinks[layer].astype(f32) * LOG2E, WINDOW).reshape(B_KV_HEADS, B_GROUP * WINDOW, 1),
            (B_KV_HEADS, B_GROUP * WINDOW, LANES))
        x2 = _out_swa(ya, qb, kb, vb_rep, gb, sink_rep, w_out[layer].astype(jnp.bfloat16), x2, seq)
    return x2.reshape(batch, seq, D_MODEL)
```
